```python
import math
import jax
import jax.numpy as jnp
from jax import lax
import numpy as np

D_MODEL = 1024
BATCH = 8
SEQ = 4096
DEPTH = 2

GRID_W = 64
CTX_LEN = 256
N_EVEN = (DEPTH + 1) // 2
N_ODD = DEPTH // 2
N_MOD = 9
EPS = 1e-6
D_FF = 256 * math.ceil(8 * D_MODEL / 3 / 256)

HEAD_DIM = 64
ATTN_WIDTH = D_MODEL // 2
N_Q_HEADS = ATTN_WIDTH // HEAD_DIM
N_KV_HEADS = max(1, N_Q_HEADS // 4)
KV_GROUP = N_Q_HEADS // N_KV_HEADS
Q_WIDTH = N_Q_HEADS * HEAD_DIM
KV_WIDTH = N_KV_HEADS * HEAD_DIM
WINDOW = 128
BLOCK = 128
ROPE_BASE = 10000.0
NEG_INF = -1e30

SSM_WIDTH = D_MODEL - ATTN_WIDTH
SSM_GROUP = 16
SSM_GROUPS = SSM_WIDTH // SSM_GROUP
SSM_STATE = 64
DT_MIN = 1e-3
DT_MAX = 1e-1

IN_WIDTH = Q_WIDTH + 2 * KV_WIDTH + SSM_WIDTH

FOURIER_GROUPS = 4
FOURIER_GROUP_WIDTH = D_MODEL // FOURIER_GROUPS

kernel_name = "hybrid_swa_s5_fnet_macaron_prefix"


def rms_norm(x, gain):
    xf = x.astype(jnp.float32)
    y = xf * lax.rsqrt(jnp.mean(xf * xf, axis=-1, keepdims=True) + EPS)
    return (y * gain.astype(jnp.float32)).astype(x.dtype)


def modulate(h, shift, scale):
    return h * (1.0 + scale) + shift


def ada_mod(cond, w, b):
    m = (jax.nn.silu(cond) @ w + b)[..., None, :]
    return jnp.split(m, N_MOD, axis=-1)


def swiglu(h, w1, w3, w2):
    return (jax.nn.silu(h @ w1) * (h @ w3)) @ w2


def half_ffn(s, gain, mod3, w1, w3, w2):
    shift, scale, gate = mod3
    return s + 0.5 * gate * swiglu(modulate(rms_norm(s, gain), shift, scale), w1, w3, w2)


def axial_rope_tables(rows):
    row = jnp.repeat(jnp.arange(rows, dtype=jnp.float32), GRID_W)
    col = jnp.tile(jnp.arange(GRID_W, dtype=jnp.float32), rows)
    n_freq = HEAD_DIM // 4
    inv_freq = 1.0 / (ROPE_BASE ** (jnp.arange(n_freq, dtype=jnp.float32) / n_freq))
    ang = jnp.concatenate([row[:, None] * inv_freq, col[:, None] * inv_freq], axis=-1)
    return jnp.cos(ang), jnp.sin(ang)


def apply_rope(x, cos, sin):
    x1, x2 = jnp.split(x.astype(jnp.float32), 2, axis=-1)
    c = cos[None, :, None, :]
    s = sin[None, :, None, :]
    return jnp.concatenate([x1 * c - x2 * s, x1 * s + x2 * c], axis=-1).astype(x.dtype)


def project_heads(h, w_in, q_gain, k_gain, with_q):
    b, n = h.shape[:2]
    if with_q:
        q_part, rest = jnp.split(h @ w_in, [Q_WIDTH], axis=-1)
        q = rms_norm(q_part.reshape(b, n, N_Q_HEADS, HEAD_DIM), q_gain)
    else:
        rest = h @ w_in[:, Q_WIDTH:]
        q = None
    k, v, u = jnp.split(rest, [KV_WIDTH, 2 * KV_WIDTH], axis=-1)
    k = rms_norm(k.reshape(b, n, N_KV_HEADS, HEAD_DIM), k_gain)
    v = v.reshape(b, n, N_KV_HEADS, HEAD_DIM)
    u = u.reshape(b, n, SSM_GROUPS, SSM_GROUP)
    return q, k, v, u


def window_attention(q, k, v, kc, vc, sink):
    b, n = q.shape[:2]
    nb = n // BLOCK
    span = BLOCK + 2 * WINDOW
    scale = HEAD_DIM ** -0.5
    qb = q.reshape(b, nb, BLOCK, N_KV_HEADS, KV_GROUP, HEAD_DIM)
    starts = jnp.arange(nb) * BLOCK
    idx = starts[:, None] + jnp.arange(span)[None, :]
    pad = ((0, 0), (WINDOW, WINDOW), (0, 0), (0, 0))
    kb = jnp.take(jnp.pad(k, pad), idx, axis=1)
    vb = jnp.take(jnp.pad(v, pad), idx, axis=1)
    s_win = jnp.einsum('bnqhgd,bnkhd->bnhgqk', qb, kb).astype(jnp.float32) * scale
    s_ctx = jnp.einsum('bnqhgd,bchd->bnhgqc', qb, kc).astype(jnp.float32) * scale
    key_pos = idx - WINDOW
    q_pos = starts[:, None] + jnp.arange(BLOCK)[None, :]
    ok = ((jnp.abs(key_pos[:, None, :] - q_pos[:, :, None]) <= WINDOW)
          & (key_pos[:, None, :] >= 0) & (key_pos[:, None, :] < n))
    s_win = jnp.where(ok[None, :, None, None], s_win, NEG_INF)
    sink_col = jnp.broadcast_to(
        sink.astype(jnp.float32).reshape(1, 1, N_KV_HEADS, KV_GROUP, 1, 1), s_win.shape[:-1] + (1,))
    p = jax.nn.softmax(jnp.concatenate([s_win, s_ctx, sink_col], axis=-1), axis=-1)
    n_ctx = kc.shape[1]
    p_win = p[..., :span].astype(v.dtype)
    p_ctx = p[..., span:span + n_ctx].astype(v.dtype)
    o = (jnp.einsum('bnhgqk,bnkhd->bnqhgd', p_win, vb)
         + jnp.einsum('bnhgqc,bchd->bnqhgd', p_ctx, vc))
    return o.reshape(b, n, Q_WIDTH)


def context_attention(qc, kc, vc, sink):
    b, n = qc.shape[:2]
    qg = qc.reshape(b, n, N_KV_HEADS, KV_GROUP, HEAD_DIM)
    s = jnp.einsum('bqhgd,bkhd->bhgqk', qg, kc).astype(jnp.float32) * (HEAD_DIM ** -0.5)
    sink_col = jnp.broadcast_to(
        sink.astype(jnp.float32).reshape(1, N_KV_HEADS, KV_GROUP, 1, 1), s.shape[:-1] + (1,))
    p = jax.nn.softmax(jnp.concatenate([s, sink_col], axis=-1), axis=-1)[..., :-1]
    o = jnp.einsum('bhgqk,bkhd->bqhgd', p.astype(vc.dtype), vc)
    return o.reshape(b, n, Q_WIDTH)


def s5_discretize(lam_re, lam_im, log_dt, b_re, b_im):
    dt = jnp.exp(log_dt.astype(jnp.float32))[:, None]
    lr = lam_re.astype(jnp.float32)
    li = lam_im.astype(jnp.float32)
    mag = jnp.exp(lr * dt)
    ar = mag * jnp.cos(li * dt)
    ai = mag * jnp.sin(li * dt)
    nr = ar - 1.0
    den = lr * lr + li * li
    fr = (nr * lr + ai * li) / den
    fi = (ai * lr - nr * li) / den
    br = b_re.astype(jnp.float32)
    bi = b_im.astype(jnp.float32)
    bbr = fr[..., None] * br - fi[..., None] * bi
    bbi = fr[..., None] * bi + fi[..., None] * br
    return ar, ai, bbr, bbi


def s5_combine(e1, e2):
    a1r, a1i, b1r, b1i = e1
    a2r, a2i, b2r, b2i = e2
    return (a1r * a2r - a1i * a2i,
            a1r * a2i + a1i * a2r,
            a2r * b1r - a2i * b1i + b2r,
            a2r * b1i + a2i * b1r + b2i)


def s5_direction(u, uc, lam_re, lam_im, log_dt, b_re, b_im, c_re, c_im, reverse, want_ctx):
    ar, ai, bbr, bbi = s5_discretize(lam_re, lam_im, log_dt, b_re, b_im)
    cr = c_re.astype(jnp.float32)
    ci = c_im.astype(jnp.float32)

    def drive(x):
        xf = x.astype(jnp.float32)
        return (jnp.einsum('bngi,gpi->bngp', xf, bbr), jnp.einsum('bngi,gpi->bngp', xf, bbi))

    def scan(xr, xi):
        n = xr.shape[1]
        a_r = jnp.broadcast_to(ar[None, None], (1, n) + ar.shape)
        a_i = jnp.broadcast_to(ai[None, None], (1, n) + ai.shape)
        _, _, hr, hi = lax.associative_scan(s5_combine, (a_r, a_i, xr, xi), reverse=reverse, axis=1)
        return hr, hi

    def readout(hr, hi):
        return jnp.einsum('gjp,bngp->bngj', cr, hr) - jnp.einsum('gjp,bngp->bngj', ci, hi)

    hcr, hci = scan(*drive(uc))
    end = 0 if reverse else -1
    h0r, h0i = hcr[:, end], hci[:, end]
    xr, xi = drive(u)
    first = -1 if reverse else 0
    xr = xr.at[:, first].add(ar * h0r - ai * h0i)
    xi = xi.at[:, first].add(ar * h0i + ai * h0r)
    hr, hi = scan(xr, xi)
    y = readout(hr, hi)
    yc = readout(hcr, hci) if want_ctx else None
    return y, yc


def s5_glu(y, w_glu):
    g = jax.nn.gelu(y.reshape(y.shape[0], y.shape[1], SSM_WIDTH))
    return g * jax.nn.sigmoid(g @ w_glu)


def s5_bidirectional(u, uc, lam_re, lam_im, log_dt, b_re, b_im, c_re, c_im, d_skip, w_glu, want_ctx):
    d = d_skip.astype(jnp.float32).reshape(SSM_GROUPS, SSM_GROUP)
    y = d * u.astype(jnp.float32)
    yc = d * uc.astype(jnp.float32) if want_ctx else None
    for direction, reverse in ((0, False), (1, True)):
        yd, ycd = s5_direction(u, uc, lam_re[direction], lam_im[direction], log_dt[direction],
                               b_re[direction], b_im[direction], c_re[direction], c_im[direction],
                               reverse, want_ctx)
        y = y + yd
        if want_ctx:
            yc = yc + ycd
    out = s5_glu(y, w_glu).astype(u.dtype)
    out_c = s5_glu(yc, w_glu).astype(uc.dtype) if want_ctx else None
    return out, out_c


def attn_ssm_mixer(h, hc, cos, sin, w_in, q_gain, k_gain, sink, lam_re, lam_im, log_dt,
                   b_re, b_im, c_re, c_im, d_skip, w_glu, w_out, want_ctx):
    q, k, v, u = project_heads(h, w_in, q_gain, k_gain, True)
    qc, kc, vc, uc = project_heads(hc, w_in, q_gain, k_gain, want_ctx)
    q = apply_rope(q, cos, sin)
    k = apply_rope(k, cos, sin)
    attn = window_attention(q, k, v, kc, vc, sink)
    ssm, ssm_c = s5_bidirectional(u, uc, lam_re, lam_im, log_dt, b_re, b_im, c_re, c_im,
                                  d_skip, w_glu, want_ctx)
    y = jnp.concatenate([attn, ssm.astype(attn.dtype)], axis=-1) @ w_out
    if not want_ctx:
        return y, None
    attn_c = context_attention(qc, kc, vc, sink)
    yc = jnp.concatenate([attn_c, ssm_c.astype(attn_c.dtype)], axis=-1) @ w_out
    return y, yc


def fourier_mix(h, w):
    b, n, d = h.shape
    hg = h.astype(jnp.float32).reshape(b, n, FOURIER_GROUPS, FOURIER_GROUP_WIDTH)
    f = jnp.fft.fftn(hg, axes=(1, 3), norm="ortho").real
    return f.reshape(b, n, d).astype(h.dtype) @ w


def odd_layer_stream(s, gains, mod, w1, w3, w2, w_f):
    s = half_ffn(s, gains[0], mod[0:3], w1[0], w3[0], w2[0])
    h = modulate(rms_norm(s, gains[1]), mod[3], mod[4])
    s = s + mod[5] * fourier_mix(h, w_f)
    return half_ffn(s, gains[2], mod[6:9], w1[1], w3[1], w2[1])


def setup_inputs(seed: int = 0) -> dict:
    key = jax.random.key(seed)
    ks = jax.random.split(key, 25)
    f32 = jnp.float32
    D, F, G, P = D_MODEL, D_FF, SSM_GROUPS, SSM_STATE

    def nrm(k, shape, scale):
        return scale * jax.random.normal(k, shape, f32)

    lam_im0 = jnp.pi * jnp.arange(P, dtype=f32)
    return {
        "x": nrm(ks[0], (BATCH, SEQ, D), 1.0),
        "c": nrm(ks[1], (BATCH, D), 1.0),
        "ctx": nrm(ks[2], (BATCH, CTX_LEN, D), 1.0),
        "c_ctx": nrm(ks[3], (D,), 1.0),
        "w_ada": nrm(ks[4], (DEPTH, D, N_MOD * D), D ** -0.5),
        "b_ada": nrm(ks[5], (DEPTH, N_MOD * D), 0.01),
        "norm_gain": 1.0 + nrm(ks[6], (DEPTH, 3, D), 0.02),
        "ffn_w1": nrm(ks[7], (DEPTH, 2, D, F), D ** -0.5),
        "ffn_w3": nrm(ks[8], (DEPTH, 2, D, F), D ** -0.5),
        "ffn_w2": nrm(ks[9], (DEPTH, 2, F, D), F ** -0.5),
        "w_in": nrm(ks[10], (N_EVEN, D, IN_WIDTH), D ** -0.5),
        "q_gain": 1.0 + nrm(ks[11], (N_EVEN, HEAD_DIM), 0.02),
        "k_gain": 1.0 + nrm(ks[12], (N_EVEN, HEAD_DIM), 0.02),
        "sink_logit": nrm(ks[13], (N_EVEN, N_Q_HEADS), 0.5),
        "ssm_lam_re": -0.5 + nrm(ks[14], (N_EVEN, 2, G, P), 0.01),
        "ssm_lam_im": lam_im0 + nrm(ks[15], (N_EVEN, 2, G, P), 0.01),
        "ssm_log_dt": jax.random.uniform(ks[16], (N_EVEN, 2, G), f32, math.log(DT_MIN), math.log(DT_MAX)),
        "ssm_b_re": nrm(ks[17], (N_EVEN, 2, G, P, SSM_GROUP), (2 * SSM_GROUP) ** -0.5),
        "ssm_b_im": nrm(ks[18], (N_EVEN, 2, G, P, SSM_GROUP), (2 * SSM_GROUP) ** -0.5),
        "ssm_c_re": nrm(ks[19], (N_EVEN, 2, G, SSM_GROUP, P), P ** -0.5),
        "ssm_c_im": nrm(ks[20], (N_EVEN, 2, G, SSM_GROUP, P), P ** -0.5),
        "ssm_d": nrm(ks[21], (N_EVEN, SSM_WIDTH), 1.0),
        "ssm_w_glu": nrm(ks[22], (N_EVEN, SSM_WIDTH, SSM_WIDTH), SSM_WIDTH ** -0.5),
        "w_out": nrm(ks[23], (N_EVEN, D, D), D ** -0.5),
        "fourier_w_out": nrm(ks[24], (N_ODD, D, D), D ** -0.5),
    }


def reference(x, c, ctx, c_ctx, w_ada, b_ada, norm_gain, ffn_w1, ffn_w3, ffn_w2, w_in, q_gain, k_gain,
              sink_logit, ssm_lam_re, ssm_lam_im, ssm_log_dt, ssm_b_re, ssm_b_im, ssm_c_re, ssm_c_im,
              ssm_d, ssm_w_glu, w_out, fourier_w_out):
    n_lat = x.shape[1]
    rows = n_lat // GRID_W
    cos, sin = axial_rope_tables(rows)
    s_ctx = ctx
    for layer in range(DEPTH):
        even = layer % 2 == 0
        ctx_later = any(j % 2 == 0 for j in range(layer + 1, DEPTH))
        gains = norm_gain[layer]
        w1, w3, w2 = ffn_w1[layer], ffn_w3[layer], ffn_w2[layer]
        m = ada_mod(c, w_ada[layer], b_ada[layer])
        if even:
            e = layer // 2
            mc = ada_mod(c_ctx, w_ada[layer], b_ada[layer])
            x = half_ffn(x, gains[0], m[0:3], w1[0], w3[0], w2[0])
            s_ctx = half_ffn(s_ctx, gains[0], mc[0:3], w1[0], w3[0], w2[0])
            h = modulate(rms_norm(x, gains[1]), m[3], m[4])
            hc = modulate(rms_norm(s_ctx, gains[1]), mc[3], mc[4])
            y, yc = attn_ssm_mixer(h, hc, cos, sin, w_in[e], q_gain[e], k_gain[e], sink_logit[e],
                                   ssm_lam_re[e], ssm_lam_im[e], ssm_log_dt[e], ssm_b_re[e], ssm_b_im[e],
                                   ssm_c_re[e], ssm_c_im[e], ssm_d[e], ssm_w_glu[e], w_out[e], ctx_later)
            x = x + m[5] * y
            x = half_ffn(x, gains[2], m[6:9], w1[1], w3[1], w2[1])
            if ctx_later:
                s_ctx = s_ctx + mc[5] * yc
                s_ctx = half_ffn(s_ctx, gains[2], mc[6:9], w1[1], w3[1], w2[1])
        else:
            o = layer // 2
            x = odd_layer_stream(x, gains, m, w1, w3, w2, fourier_w_out[o])
            if ctx_later:
                mc = ada_mod(c_ctx, w_ada[layer], b_ada[layer])
                s_ctx = odd_layer_stream(s_ctx, gains, mc, w1, w3, w2, fourier_w_out[o])
    return x
```

```python
import functools
import math

import jax
import jax.numpy as jnp
from jax import lax
from jax.experimental import pallas as pl
from jax.experimental.pallas import tpu as pltpu

F32 = jnp.float32
BF16 = jnp.bfloat16

D_MODEL = 1024
N_MOD = 9
EPS = 1e-6
HEAD_DIM = 64
N_Q_HEADS = 8
N_KV_HEADS = 2
Q_WIDTH = N_Q_HEADS * HEAD_DIM
KV_WIDTH = N_KV_HEADS * HEAD_DIM
ATT_BLOCK = 128
ROPE_BASE = 10000.0
NEG_INF = -1e30
GRID_W = 64
SSM_WIDTH = 512
SSM_GROUP = 16
SSM_GROUPS = 32
SSM_STATE = 64
SSM_CHUNK = 128
FOURIER_GROUPS = 4
FOURIER_GROUP_WIDTH = 256
FFT_RADIX = 64
FFN_CHUNK = 256
LANES = 128
VMEM_LIMIT = 56 * 1024 * 1024


def _dot(a, b):
    return jnp.dot(a, b, preferred_element_type=F32)


def _dot_nt(a, b):
    return lax.dot_general(a, b, (((1,), (1,)), ((), ())), preferred_element_type=F32)


def _split(a):
    hi = a.astype(BF16)
    lo = (a - hi.astype(F32)).astype(BF16)
    return hi, lo


def _dot3(a, b):
    a_hi, a_lo = _split(a)
    b_hi, b_lo = _split(b)
    return _dot(a_hi, b_hi) + _dot(a_lo, b_hi) + _dot(a_hi, b_lo)


def _norm_mod(x, gain, shift, scale):
    ms = jnp.mean(x * x, axis=-1, keepdims=True)
    y = x * lax.rsqrt(ms + EPS) * gain
    return y * (1.0 + scale) + shift


def _params(*sem):
    return pltpu.CompilerParams(dimension_semantics=sem, vmem_limit_bytes=VMEM_LIMIT)


def _resident(shape):
    nd = len(shape)
    return pl.BlockSpec(shape, lambda *_: (0,) * nd, pipeline_mode=pl.Buffered(1))


def _ada_kernel(cond_ref, w_ref, b_ref, o_ref):
    a = cond_ref[...]
    a = a * jax.nn.sigmoid(a)
    o_ref[0] = _dot3(a, w_ref[0]) + b_ref[0]


def _ada_mod(cond, w_ada, b_ada):
    depth, d, nd = w_ada.shape
    rows = cond.shape[0]
    tn = nd // 8
    return pl.pallas_call(
        _ada_kernel,
        grid=(depth, nd // tn),
        in_specs=[
            pl.BlockSpec((rows, d), lambda l, j: (0, 0)),
            pl.BlockSpec((1, d, tn), lambda l, j: (l, 0, j)),
            pl.BlockSpec((1, 1, tn), lambda l, j: (l, 0, j)),
        ],
        out_specs=pl.BlockSpec((1, rows, tn), lambda l, j: (l, 0, j)),
        out_shape=jax.ShapeDtypeStruct((depth, rows, nd), F32),
        compiler_params=_params("parallel", "parallel"),
        name="ada_mod",
    )(cond, w_ada, b_ada.reshape(depth, 1, nd))


def _ffn_kernel(x_ref, mod_ref, gain_ref, w1_ref, w3_ref, w2_ref, o_ref, acc_ref, *, row0, grow):
    x = x_ref[0]
    shift = mod_ref[0, row0:row0 + 1, :]
    scale = mod_ref[0, row0 + 1:row0 + 2, :]
    gate = mod_ref[0, row0 + 2:row0 + 3, :]
    h = _norm_mod(x, gain_ref[grow:grow + 1, :], shift, scale).astype(BF16)
    acc_ref[...] = jnp.zeros_like(acc_ref)

    def body(c, carry):
        a = _dot(h, w1_ref[c])
        b = _dot(h, w3_ref[c])
        g = (a * jax.nn.sigmoid(a) * b).astype(BF16)
        acc_ref[...] += _dot(g, w2_ref[c])
        return carry

    lax.fori_loop(0, w1_ref.shape[0], body, 0)
    o_ref[0] = x + (0.5 * gate) * acc_ref[...]


def _half_ffn(s, mod, row0, gains, grow, w1c, w3c, w2c, mod_index=None, tm=512):
    bs, ls, d = s.shape
    tm = min(tm, ls)
    nf, _, fc = w1c.shape
    if mod_index is None:
        mod_map = lambda b, i: (b, 0, 0)
    else:
        mod_map = lambda b, i: (mod_index, 0, 0)
    return pl.pallas_call(
        functools.partial(_ffn_kernel, row0=row0, grow=grow),
        grid=(bs, ls // tm),
        in_specs=[
            pl.BlockSpec((1, tm, d), lambda b, i: (b, i, 0)),
            pl.BlockSpec((1, N_MOD, d), mod_map),
            _resident(gains.shape),
            _resident(w1c.shape),
            _resident(w3c.shape),
            _resident(w2c.shape),
        ],
        out_specs=pl.BlockSpec((1, tm, d), lambda b, i: (b, i, 0)),
        out_shape=jax.ShapeDtypeStruct(s.shape, F32),
        scratch_shapes=[pltpu.VMEM((tm, d), F32)],
        compiler_params=_params("parallel", "parallel"),
        name="half_ffn",
    )(s, mod, gains, w1c, w3c, w2c)


def _ffn_weights(w1, w3, w2):
    d, f = w1.shape
    nf = f // FFN_CHUNK
    w1c = w1.astype(BF16).reshape(d, nf, FFN_CHUNK).transpose(1, 0, 2)
    w3c = w3.astype(BF16).reshape(d, nf, FFN_CHUNK).transpose(1, 0, 2)
    w2c = w2.astype(BF16).reshape(nf, FFN_CHUNK, d)
    return w1c, w3c, w2c


def _head_norm(t, bd, gain):
    hi, lo = _split(t * t)
    ms = (_dot(hi, bd) + _dot(lo, bd)) * (1.0 / HEAD_DIM)
    return t * lax.rsqrt(ms + EPS) * gain


def _rope(t, cos, sin):
    w = t.shape[1]
    half = HEAD_DIM // 2
    lane = lax.broadcasted_iota(jnp.int32, t.shape, 1)
    first = (lane % HEAD_DIM) < half
    rot = jnp.where(first, pltpu.roll(t, w - half, 1), pltpu.roll(t, half, 1))
    reps = w // LANES
    c = jnp.concatenate([cos] * reps, axis=1) if reps > 1 else cos
    s = jnp.concatenate([sin] * reps, axis=1) if reps > 1 else sin
    return t * c + rot * s


def _dup_heads(t):
    lane = lax.broadcasted_iota(jnp.int32, t.shape, 1)
    low = lane < HEAD_DIM
    r = pltpu.roll(t, HEAD_DIM, 1)
    return jnp.concatenate([jnp.where(low, t, r), jnp.where(low, r, t)], axis=1)


def _inproj_kernel(x_ref, mod_ref, gain_ref, wqkv_ref, wut_ref, bd_ref, qg_ref, kg_ref, cos_ref, sin_ref,
                   q_ref, kk_ref, vv_ref, ut_ref):
    x = x_ref[0]
    h = _norm_mod(x, gain_ref[1:2, :], mod_ref[0, 3:4, :], mod_ref[0, 4:5, :]).astype(BF16)
    qkv = _dot(h, wqkv_ref[...])
    q = qkv[:, :Q_WIDTH]
    k = qkv[:, Q_WIDTH:Q_WIDTH + KV_WIDTH]
    v = qkv[:, Q_WIDTH + KV_WIDTH:]
    cos = cos_ref[...]
    sin = sin_ref[...]
    q = _rope(_head_norm(q, bd_ref[...], qg_ref[...]), cos, sin)
    k = _rope(_head_norm(k, bd_ref[:KV_WIDTH, :KV_WIDTH], kg_ref[...]), cos, sin)
    q_ref[0] = (q * (HEAD_DIM ** -0.5)).astype(BF16)
    kk_ref[0] = _dup_heads(k).astype(BF16)
    vv_ref[0] = _dup_heads(v).astype(BF16)
    ut_ref[...] = _dot_nt(wut_ref[...], h)


def _in_proj(s, mod, gains, wqkv, wut, bd, qg, kg, cos, sin, mod_index=None, tm=512):
    bs, ls, d = s.shape
    tm = min(tm, ls)
    nt = ls // tm
    if mod_index is None:
        mod_map = lambda b, i: (b, 0, 0)
    else:
        mod_map = lambda b, i: (mod_index, 0, 0)
    tok = lambda w: pl.BlockSpec((1, tm, w), lambda b, i: (b, i, 0))
    return pl.pallas_call(
        _inproj_kernel,
        grid=(bs, nt),
        in_specs=[
            tok(d),
            pl.BlockSpec((1, N_MOD, d), mod_map),
            _resident(gains.shape),
            _resident(wqkv.shape),
            _resident(wut.shape),
            _resident(bd.shape),
            _resident(qg.shape),
            _resident(kg.shape),
            pl.BlockSpec((tm, LANES), lambda b, i: (i, 0)),
            pl.BlockSpec((tm, LANES), lambda b, i: (i, 0)),
        ],
        out_specs=[
            tok(Q_WIDTH), tok(2 * KV_WIDTH), tok(2 * KV_WIDTH),
            pl.BlockSpec((SSM_WIDTH, tm), lambda b, i: (0, b * nt + i)),
        ],
        out_shape=[
            jax.ShapeDtypeStruct((bs, ls, Q_WIDTH), BF16),
            jax.ShapeDtypeStruct((bs, ls, 2 * KV_WIDTH), BF16),
            jax.ShapeDtypeStruct((bs, ls, 2 * KV_WIDTH), BF16),
            jax.ShapeDtypeStruct((SSM_WIDTH, bs * ls), F32),
        ],
        compiler_params=_params("parallel", "parallel"),
        name="in_proj",
    )(s, mod, gains, wqkv, wut, bd, qg, kg, cos, sin)


def _rope_tables(n_lat):
    rows = n_lat // GRID_W
    row = jnp.repeat(jnp.arange(rows, dtype=F32), GRID_W)
    col = jnp.tile(jnp.arange(GRID_W, dtype=F32), rows)
    n_freq = HEAD_DIM // 4
    inv_freq = 1.0 / (ROPE_BASE ** (jnp.arange(n_freq, dtype=F32) / n_freq))
    ang = jnp.concatenate([row[:, None] * inv_freq, col[:, None] * inv_freq], axis=-1)
    cos, sin = jnp.cos(ang), jnp.sin(ang)
    cos_t = jnp.tile(cos, (1, LANES // (HEAD_DIM // 2)))
    sin_t = jnp.tile(jnp.concatenate([-sin, sin], axis=-1), (1, LANES // HEAD_DIM))
    return cos_t, sin_t


def _attn_kernel(sink_ref, q_ref, kk_ref, vv_ref, kkc_ref, vvc_ref, o_ref):
    i = pl.program_id(1)
    nb = pl.num_programs(1)
    blk = ATT_BLOCK
    q = q_ref[0]
    lane = lax.broadcasted_iota(jnp.int32, (blk, LANES), 1)
    low = lane < HEAD_DIM
    rel = lane - lax.broadcasted_iota(jnp.int32, (blk, LANES), 0)
    ok_prev = rel >= jnp.where(i > 0, 0, 2 * blk)
    ok_next = rel <= jnp.where(i < nb - 1, 0, -2 * blk)
    j_prev = jnp.maximum(i - 1, 0)
    j_next = jnp.minimum(i + 1, nb - 1)
    zero = jnp.zeros_like(q[:, :LANES])
    cols = []
    for h in range(N_KV_HEADS):
        hs = slice(h * LANES, (h + 1) * LANES)

        def rows(ref, j):
            return ref[0, pl.ds(pl.multiple_of(j * blk, blk), blk), hs]

        keys = jnp.concatenate([rows(kk_ref, j_prev), rows(kk_ref, i), rows(kk_ref, j_next), kkc_ref[0, :, hs]], axis=0)
        vals = jnp.concatenate([rows(vv_ref, j_prev), rows(vv_ref, i), rows(vv_ref, j_next), vvc_ref[0, :, hs]], axis=0)
        qs = []
        for col in (2 * h, 2 * h + 1):
            qc = q[:, col * LANES:(col + 1) * LANES]
            qs.append(jnp.where(low, qc, zero))
            qs.append(jnp.where(low, zero, qc))
        s_all = _dot_nt(jnp.concatenate(qs, axis=0), keys)
        ps = []
        inv = []
        for g in range(4):
            s = s_all[g * blk:(g + 1) * blk]
            s = jnp.concatenate([
                jnp.where(ok_prev, s[:, :blk], NEG_INF),
                s[:, blk:2 * blk],
                jnp.where(ok_next, s[:, 2 * blk:3 * blk], NEG_INF),
                s[:, 3 * blk:],
            ], axis=1)
            sink = sink_ref[4 * h + g]
            m = jnp.maximum(jnp.max(s, axis=-1, keepdims=True), sink)
            p = jnp.exp(s - m)
            den = jnp.sum(p, axis=-1, keepdims=True) + jnp.exp(sink - m)
            ps.append(p.astype(BF16))
            inv.append(1.0 / den)
        o = _dot(jnp.concatenate(ps, axis=0), vals)
        og = [o[g * blk:(g + 1) * blk] * inv[g] for g in range(4)]
        cols.append(jnp.where(low, og[0], og[1]))
        cols.append(jnp.where(low, og[2], og[3]))
    o_ref[0] = jnp.concatenate(cols, axis=1).astype(BF16)


def _attention(q, kk, vv, kkc, vvc, sink):
    bs, ls, _ = q.shape
    nb = ls // ATT_BLOCK
    n_ctx = kkc.shape[1]
    full = lambda n: pl.BlockSpec((1, n, 2 * KV_WIDTH), lambda b, i: (b, 0, 0))
    return pl.pallas_call(
        _attn_kernel,
        grid=(bs, nb),
        in_specs=[
            pl.BlockSpec(memory_space=pltpu.SMEM),
            pl.BlockSpec((1, ATT_BLOCK, Q_WIDTH), lambda b, i: (b, i, 0)),
            full(ls), full(ls), full(n_ctx), full(n_ctx),
        ],
        out_specs=pl.BlockSpec((1, ATT_BLOCK, Q_WIDTH), lambda b, i: (b, i, 0)),
        out_shape=jax.ShapeDtypeStruct((bs, ls, Q_WIDTH), BF16),
        compiler_params=_params("parallel", "arbitrary"),
        name="window_attention",
    )(sink, q, kk, vv, kkc, vvc)


def _cpow(lr, li, dt, tau):
    mag = jnp.exp(lr * dt * tau)
    ang = li * dt * tau
    return mag * jnp.cos(ang), mag * jnp.sin(ang)


def _ssm_prep_kernel(lrr_ref, lir_ref, lrc_ref, lic_ref, ldt_ref, btr_ref, bti_ref, btrr_ref, btri_ref,
                     crr_ref, cri_ref, ctr_ref, cti_ref, kcat_ref, wst_ref, wcar_ref, arow_ref):
    p = SSM_STATE
    t = SSM_CHUNK
    lane_t = lax.broadcasted_iota(jnp.int32, (1, t), 1).astype(F32)
    sub_t = lax.broadcasted_iota(jnp.int32, (t, 1), 0).astype(F32)
    kparts = []
    for d in range(2):
        dt = jnp.exp(ldt_ref[0, d])
        lr, li = lrr_ref[0, d], lir_ref[0, d]
        lrc, lic = lrc_ref[0, d], lic_ref[0, d]
        ar, ai = _cpow(lr, li, dt, 1.0)
        nr = ar - 1.0
        den = lr * lr + li * li
        fr = (nr * lr + ai * li) / den
        fi = (ai * lr - nr * li) / den
        bt_r, bt_i = btr_ref[0, d], bti_ref[0, d]
        bbr = fr * bt_r - fi * bt_i
        bbi = fr * bt_i + fi * bt_r
        rep_r, rep_i = btrr_ref[0, d], btri_ref[0, d]
        bbr_rep = fr * rep_r - fi * rep_i
        bbi_rep = fr * rep_i + fi * rep_r
        c_r, c_i = crr_ref[0, d], cri_ref[0, d]
        m1 = c_r * bbr_rep - c_i * bbi_rep
        m2 = c_r * bbi_rep + c_i * bbr_rep
        tau = lane_t if d == 0 else (float(t) - lane_t)
        e_r, e_i = _cpow(lrc, lic, dt, tau)
        kmat = _dot3(m1, e_r) - _dot3(m2, e_i)
        if d == 0:
            kparts.append(kmat)
        else:
            lag0 = jnp.sum(m1, axis=1, keepdims=True)
            first = lax.broadcasted_iota(jnp.int32, (1, t), 1) == 0
            kparts[0] = kparts[0] + jnp.where(first, lag0, 0.0)
            kparts.insert(0, kmat)
        tau_s = (float(t - 1) - sub_t) if d == 0 else sub_t
        et_r, et_i = _cpow(lr, li, dt, tau_s)
        for i in range(SSM_GROUP):
            re = et_r * bbr[i:i + 1] - et_i * bbi[i:i + 1]
            im = et_r * bbi[i:i + 1] + et_i * bbr[i:i + 1]
            wst_ref[0, i * t:(i + 1) * t, d * 2 * p:(d + 1) * 2 * p] = jnp.concatenate([re, im], axis=1).astype(BF16)
        tau_c = (lane_t + 1.0) if d == 0 else (float(t) - lane_t)
        ec_r, ec_i = _cpow(lrc, lic, dt, tau_c)
        ct_r, ct_i = ctr_ref[0, d], cti_ref[0, d]
        for jo in range(SSM_GROUP):
            cr = ct_r[:, jo:jo + 1]
            ci = ct_i[:, jo:jo + 1]
            wcar_ref[0, d * 2 * p:d * 2 * p + p, jo * t:(jo + 1) * t] = (cr * ec_r - ci * ec_i).astype(BF16)
            wcar_ref[0, d * 2 * p + p:(d + 1) * 2 * p, jo * t:(jo + 1) * t] = (-(cr * ec_i + ci * ec_r)).astype(BF16)
        at_r, at_i = _cpow(lr, li, dt, float(t))
        arow_ref[0, d, 0:1, :] = jnp.concatenate([at_r, at_r], axis=1)
        arow_ref[0, d, 1:2, :] = jnp.concatenate([-at_i, at_i], axis=1)
    kcat_ref[0] = jnp.concatenate(kparts, axis=1)


def _ssm_prep(lam_re, lam_im, log_dt, b_re, b_im, c_re, c_im):
    g, p, n, t = SSM_GROUPS, SSM_STATE, SSM_GROUP, SSM_CHUNK
    gd = lambda a: jnp.swapaxes(a, 0, 1)
    lrr = gd(lam_re)[:, :, None, :]
    lir = gd(lam_im)[:, :, None, :]
    lrc = gd(lam_re)[:, :, :, None]
    lic = gd(lam_im)[:, :, :, None]
    ldt = gd(log_dt)[:, :, None, None]
    bt_r = jnp.swapaxes(gd(b_re), 2, 3)
    bt_i = jnp.swapaxes(gd(b_im), 2, 3)
    btr_r = jnp.tile(bt_r, (1, 1, n, 1))
    btr_i = jnp.tile(bt_i, (1, 1, n, 1))
    cr_r = jnp.repeat(gd(c_re), n, axis=2)
    cr_i = jnp.repeat(gd(c_im), n, axis=2)
    ct_r = jnp.swapaxes(gd(c_re), 2, 3)
    ct_i = jnp.swapaxes(gd(c_im), 2, 3)
    args = (lrr, lir, lrc, lic, ldt, bt_r, bt_i, btr_r, btr_i, cr_r, cr_i, ct_r, ct_i)
    spec = lambda a: pl.BlockSpec((1,) + a.shape[1:], lambda i: (i,) + (0,) * (a.ndim - 1))
    return pl.pallas_call(
        _ssm_prep_kernel,
        grid=(g,),
        in_specs=[spec(a) for a in args],
        out_specs=[
            pl.BlockSpec((1, n * n, 2 * t), lambda i: (i, 0, 0)),
            pl.BlockSpec((1, n * t, 4 * p), lambda i: (i, 0, 0)),
            pl.BlockSpec((1, 4 * p, n * t), lambda i: (i, 0, 0)),
            pl.BlockSpec((1, 2, 2, 2 * p), lambda i: (i, 0, 0, 0)),
        ],
        out_shape=[
            jax.ShapeDtypeStruct((g, n * n, 2 * t), F32),
            jax.ShapeDtypeStruct((g, n * t, 4 * p), BF16),
            jax.ShapeDtypeStruct((g, 4 * p, n * t), BF16),
            jax.ShapeDtypeStruct((g, 2, 2, 2 * p), F32),
        ],
        compiler_params=_params("parallel"),
        name="ssm_prep",
    )(*args)


def _ssm_kernel(u_ref, uc_ref, kcat_ref, wst_ref, wcar_ref, arow_ref, d_ref, y_ref,
                tz_ref, s_ref, sc_ref, h_ref, *, nb, nc, ncc):
    n, t, p = SSM_GROUP, SSM_CHUNK, SSM_STATE

    def build(i, carry):
        for jo in range(n):
            row = kcat_ref[0, pl.ds(jo * n + i, 1), :]
            blk = pltpu.roll(jnp.broadcast_to(row, (t, 2 * t)), 0, 1, stride=1, stride_axis=0)
            tz_ref[jo // 2, pl.ds(pl.multiple_of(i * t, t), t), (jo % 2) * t:(jo % 2 + 1) * t] = blk[:, t:].astype(BF16)
        return carry

    lax.fori_loop(0, n, build, 0)

    x = jnp.concatenate([u_ref[i].astype(BF16) for i in range(n)], axis=1)
    xc = jnp.concatenate([uc_ref[i].astype(BF16) for i in range(n)], axis=1)
    s_all = _dot(x, wst_ref[0])
    sc_all = _dot(xc, wst_ref[0])
    for d in range(2):
        s_ref[d] = s_all[:, d * 2 * p:(d + 1) * 2 * p]
        sc_ref[d] = sc_all[:, d * 2 * p:(d + 1) * 2 * p]

    def scan(d, ctx_order, chunk_of_step):
        a1, a2 = arow_ref[0, d, 0:1, :], arow_ref[0, d, 1:2, :]

        def cmul(h):
            return a1 * h + a2 * pltpu.roll(h, p, 1)

        h = jnp.zeros((nb, 2 * p), F32)
        for cc in ctx_order:
            h = cmul(h) + sc_ref[d, pl.ds(cc, nb, stride=ncc), :]

        def body(k, h):
            c = chunk_of_step(k)
            h_ref[d, pl.ds(c, nb, stride=nc), :] = h
            return cmul(h) + s_ref[d, pl.ds(c, nb, stride=nc), :]

        lax.fori_loop(0, nc, body, h)

    scan(0, range(ncc), lambda k: k)
    scan(1, reversed(range(ncc)), lambda k: nc - 1 - k)

    hb = jnp.concatenate([h_ref[0], h_ref[1]], axis=1).astype(BF16)
    for jp in range(n // 2):
        y = _dot(x, tz_ref[jp]) + _dot(hb, wcar_ref[0, :, jp * 2 * t:(jp + 1) * 2 * t])
        for k in range(2):
            jo = 2 * jp + k
            y_ref[jo] = y[:, k * t:(k + 1) * t] + d_ref[jo] * u_ref[jo]


def _ssm(ut, utc, kcat, wst, wcar, arow, d_skip, nb):
    n, t, p, g = SSM_GROUP, SSM_CHUNK, SSM_STATE, SSM_GROUPS
    rows = ut.shape[1]
    rows_c = utc.shape[1]
    nc, ncc = rows // nb, rows_c // nb
    d3 = jnp.broadcast_to(d_skip.astype(F32)[:, None, None], (SSM_WIDTH, 1, t))
    return pl.pallas_call(
        functools.partial(_ssm_kernel, nb=nb, nc=nc, ncc=ncc),
        grid=(g,),
        in_specs=[
            pl.BlockSpec((n, rows, t), lambda i: (i, 0, 0)),
            pl.BlockSpec((n, rows_c, t), lambda i: (i, 0, 0)),
            pl.BlockSpec((1, n * n, 2 * t), lambda i: (i, 0, 0)),
            pl.BlockSpec((1, n * t, 4 * p), lambda i: (i, 0, 0)),
            pl.BlockSpec((1, 4 * p, n * t), lambda i: (i, 0, 0)),
            pl.BlockSpec((1, 2, 2, 2 * p), lambda i: (i, 0, 0, 0)),
            pl.BlockSpec((n, 1, t), lambda i: (i, 0, 0)),
        ],
        out_specs=pl.BlockSpec((n, rows, t), lambda i: (i, 0, 0)),
        out_shape=jax.ShapeDtypeStruct((SSM_WIDTH, rows, t), F32),
        scratch_shapes=[
            pltpu.VMEM((n // 2, n * t, 2 * t), BF16),
            pltpu.VMEM((2, rows, 2 * p), F32),
            pltpu.VMEM((2, rows_c, 2 * p), F32),
            pltpu.VMEM((2, rows, 2 * p), F32),
        ],
        compiler_params=_params("parallel"),
        name="ssm_toeplitz",
    )(ut, utc, kcat, wst, wcar, arow, d3)


def _mixout_kernel(x_ref, attn_ref, yt_ref, mod_ref, wglut_ref, wout_ref, o_ref, s_ref):
    t = SSM_CHUNK
    for j in range(yt_ref.shape[1]):
        g = jax.nn.gelu(yt_ref[:, j, :])
        z = _dot(wglut_ref[...], g.astype(BF16))
        s_ref[j * t:(j + 1) * t, :] = (g * jax.nn.sigmoid(z)).T.astype(BF16)
    y = _dot(attn_ref[0], wout_ref[:Q_WIDTH, :]) + _dot(s_ref[...], wout_ref[Q_WIDTH:, :])
    o_ref[0] = x_ref[0] + mod_ref[0, 5:6, :] * y


def _mix_out(x, attn, yt, mod, wglut, wout, tm=1024):
    bs, ls, d = x.shape
    nt = ls // tm
    return pl.pallas_call(
        _mixout_kernel,
        grid=(bs, nt),
        in_specs=[
            pl.BlockSpec((1, tm, d), lambda b, i: (b, i, 0)),
            pl.BlockSpec((1, tm, Q_WIDTH), lambda b, i: (b, i, 0)),
            pl.BlockSpec((SSM_WIDTH, tm // SSM_CHUNK, SSM_CHUNK), lambda b, i: (0, b * nt + i, 0)),
            pl.BlockSpec((1, N_MOD, d), lambda b, i: (b, 0, 0)),
            _resident(wglut.shape),
            _resident(wout.shape),
        ],
        out_specs=pl.BlockSpec((1, tm, d), lambda b, i: (b, i, 0)),
        out_shape=jax.ShapeDtypeStruct(x.shape, F32),
        scratch_shapes=[pltpu.VMEM((tm, SSM_WIDTH), BF16)],
        compiler_params=_params("parallel", "parallel"),
        name="mix_out",
    )(x, attn, yt, mod, wglut, wout)


def _fourier1_kernel(x_ref, mod_ref, gain_ref, lhs_ref, twc_ref, tws_ref, are_ref, aim_ref):
    r = FFT_RADIX
    gain = gain_ref[1:2, :]
    shift, scale = mod_ref[0, 3:4, :], mod_ref[0, 4:5, :]
    reps = x_ref.shape[3] // LANES
    for jg in range(x_ref.shape[2] // 4):
        hs = [_norm_mod(x_ref[0, :, jg * 4 + j, :], gain, shift, scale).astype(BF16) for j in range(4)]
        a = _dot(lhs_ref[...], jnp.concatenate(hs, axis=0))
        ch, sh = a[:4 * r], a[4 * r:]
        tc = jnp.concatenate([twc_ref[jg * 4 * r:(jg + 1) * 4 * r, :]] * reps, axis=1)
        ts = jnp.concatenate([tws_ref[jg * 4 * r:(jg + 1) * 4 * r, :]] * reps, axis=1)
        a_re = ch * tc - sh * ts
        a_im = -(sh * tc) - ch * ts
        for j in range(4):
            are_ref[0, jg * 4 + j] = a_re[j * r:(j + 1) * r]
            aim_ref[0, jg * 4 + j] = a_im[j * r:(j + 1) * r]


def _fourier2_kernel(are_ref, aim_ref, x_ref, mod_ref, lhs_ref, cs_ref, wf_ref, o_ref):
    r = FFT_RADIX
    gw = FOURIER_GROUP_WIDTH
    gate = mod_ref[0, 5:6, :]
    for jg in range(x_ref.shape[2] // 4):
        parts = [are_ref[0, :, jg * 4 + j, :].astype(BF16) for j in range(4)]
        parts += [aim_ref[0, :, jg * 4 + j, :].astype(BF16) for j in range(4)]
        y = _dot(lhs_ref[...], jnp.concatenate(parts, axis=0))
        yr = y[:4 * r].astype(BF16)
        yi = y[4 * r:].astype(BF16)
        zs = []
        for g in range(FOURIER_GROUPS):
            yy = jnp.concatenate([yr[:, g * gw:(g + 1) * gw], yi[:, g * gw:(g + 1) * gw]], axis=1)
            zs.append(_dot(yy, cs_ref[...]))
        f = _dot(jnp.concatenate(zs, axis=1).astype(BF16), wf_ref[...])
        for j in range(4):
            o_ref[0, :, jg * 4 + j, :] = x_ref[0, :, jg * 4 + j, :] + gate * f[j * r:(j + 1) * r]


def _fourier_tables():
    r = FFT_RADIX
    n = r * r
    k = jnp.arange(r, dtype=jnp.int32)
    ang = ((k[:, None] * k[None, :]) % r).astype(F32) * (2.0 * math.pi / r)
    wc = jnp.cos(ang) / 8.0
    ws = jnp.sin(ang) / 8.0
    eye4 = jnp.eye(4, dtype=F32)
    bc, bs = jnp.kron(eye4, wc), jnp.kron(eye4, ws)
    lhs1 = jnp.concatenate([bc, bs], axis=0).astype(BF16)
    lhs2 = jnp.concatenate([jnp.concatenate([bc, bs], axis=1),
                            jnp.concatenate([-bs, bc], axis=1)], axis=0).astype(BF16)
    tw = ((k[:, None] * k[None, :]) % n).astype(F32) * (2.0 * math.pi / n)
    twc = jnp.broadcast_to(jnp.cos(tw).reshape(n, 1), (n, LANES))
    tws = jnp.broadcast_to(jnp.sin(tw).reshape(n, 1), (n, LANES))
    m = jnp.arange(FOURIER_GROUP_WIDTH, dtype=jnp.int32)
    angc = ((m[:, None] * m[None, :]) % FOURIER_GROUP_WIDTH).astype(F32) * (2.0 * math.pi / FOURIER_GROUP_WIDTH)
    cs = jnp.concatenate([jnp.cos(angc), jnp.sin(angc)], axis=0) / 16.0
    return lhs1, lhs2, twc, tws, cs.astype(BF16)


def _fourier(x, mod, gains, wf):
    bs, ls, d = x.shape
    r = FFT_RADIX
    assert ls == r * r
    lhs1, lhs2, twc, tws, cs = _fourier_tables()
    x4 = x.reshape(bs, r, r, d)
    tj = 8
    strided = pl.BlockSpec((1, r, tj, d), lambda b, i: (b, 0, i, 0))
    a_re, a_im = pl.pallas_call(
        _fourier1_kernel,
        grid=(bs, r // tj),
        in_specs=[
            strided,
            pl.BlockSpec((1, N_MOD, d), lambda b, i: (b, 0, 0)),
            _resident(gains.shape),
            _resident(lhs1.shape),
            pl.BlockSpec((tj * r, LANES), lambda b, i: (i, 0)),
            pl.BlockSpec((tj * r, LANES), lambda b, i: (i, 0)),
        ],
        out_specs=[pl.BlockSpec((1, tj, r, d), lambda b, i: (b, i, 0, 0))] * 2,
        out_shape=[jax.ShapeDtypeStruct((bs, r, r, d), F32)] * 2,
        compiler_params=_params("parallel", "parallel"),
        name="fourier_stage1",
    )(x4, mod, gains, lhs1, twc, tws)
    out = pl.pallas_call(
        _fourier2_kernel,
        grid=(bs, r // tj),
        in_specs=[
            strided, strided, strided,
            pl.BlockSpec((1, N_MOD, d), lambda b, i: (b, 0, 0)),
            _resident(lhs2.shape),
            _resident(cs.shape),
            _resident(wf.shape),
        ],
        out_specs=strided,
        out_shape=jax.ShapeDtypeStruct((bs, r, r, d), F32),
        compiler_params=_params("parallel", "parallel"),
        name="fourier_stage2",
    )(a_re, a_im, x4, mod, lhs2, cs, wf)
    return out.reshape(bs, ls, d)


def _even_layer(x, s_ctx, mod, gains, ffn_a, ffn_b, w_in, q_gain, k_gain, sink, ssm, d_skip, w_glu, w_out):
    bs, ls, d = x.shape
    ctx_index = bs
    x = _half_ffn(x, mod, 0, gains, 0, *ffn_a)
    s_ctx = _half_ffn(s_ctx, mod, 0, gains, 0, *ffn_a, mod_index=ctx_index)

    wqkv = w_in[:, :Q_WIDTH + 2 * KV_WIDTH].astype(BF16)
    wut = w_in[:, Q_WIDTH + 2 * KV_WIDTH:].T.astype(BF16)
    bd = jnp.kron(jnp.eye(N_Q_HEADS, dtype=F32), jnp.ones((HEAD_DIM, HEAD_DIM), F32)).astype(BF16)
    qg = jnp.tile(q_gain.astype(F32), N_Q_HEADS)[None, :]
    kg = jnp.tile(k_gain.astype(F32), N_KV_HEADS)[None, :]
    cos, sin = _rope_tables(ls)
    n_ctx = s_ctx.shape[1]
    one = jnp.ones((n_ctx, LANES), F32)
    q, kk, vv, ut = _in_proj(x, mod, gains, wqkv, wut, bd, qg, kg, cos, sin)
    _, kkc, vvc, utc = _in_proj(s_ctx, mod, gains, wqkv, wut, bd, qg, kg, one, 0.0 * one, mod_index=ctx_index)

    attn = _attention(q, kk, vv, kkc, vvc, sink.astype(F32))

    t = SSM_CHUNK
    kcat, wst, wcar, arow = _ssm_prep(*ssm)
    yt = _ssm(ut.reshape(SSM_WIDTH, bs * ls // t, t), utc.reshape(SSM_WIDTH, bs * n_ctx // t, t),
              kcat, wst, wcar, arow, d_skip, bs)

    x = _mix_out(x, attn, yt, mod, w_glu.T.astype(BF16), w_out.astype(BF16))
    return _half_ffn(x, mod, 6, gains, 2, *ffn_b)


def _odd_layer(x, mod, gains, ffn_a, ffn_b, w_f):
    x = _half_ffn(x, mod, 0, gains, 0, *ffn_a)
    x = _fourier(x, mod, gains, w_f.astype(BF16))
    return _half_ffn(x, mod, 6, gains, 2, *ffn_b)


def kernel(x, c, ctx, c_ctx, w_ada, b_ada, norm_gain, ffn_w1, ffn_w3, ffn_w2, w_in, q_gain, k_gain, sink_logit, ssm_lam_re, ssm_lam_im, ssm_log_dt, ssm_b_re, ssm_b_im, ssm_c_re, ssm_c_im, ssm_d, ssm_w_glu, w_out, fourier_w_out):
    bs, ls, d = x.shape
    depth = w_ada.shape[0]
    assert depth == 2 and d == D_MODEL
    rows = 16
    cond = jnp.concatenate([c, c_ctx[None, :], jnp.zeros((rows - bs - 1, d), F32)], axis=0)
    mods = _ada_mod(cond, w_ada, b_ada)[:, :bs + 1].reshape(depth, bs + 1, N_MOD, d)

    ffn = [[_ffn_weights(ffn_w1[l, k], ffn_w3[l, k], ffn_w2[l, k]) for k in range(2)] for l in range(depth)]
    ssm = (ssm_lam_re[0], ssm_lam_im[0], ssm_log_dt[0], ssm_b_re[0], ssm_b_im[0], ssm_c_re[0], ssm_c_im[0])
    x = _even_layer(x, ctx, mods[0], norm_gain[0], ffn[0][0], ffn[0][1], w_in[0], q_gain[0], k_gain[0],
                    sink_logit[0], ssm, ssm_d[0], ssm_w_glu[0], w_out[0])
    return _odd_layer(x, mods[1], norm_gain[1], ffn[1][0], ffn[1][1], fourier_w_out[0])
```

```python
import functools
import math

import jax
import jax.numpy as jnp
from jax import lax
from jax.experimental import pallas as pl
from jax.experimental.pallas import tpu as pltpu

F32 = jnp.float32
BF16 = jnp.bfloat16

D_MODEL = 1024
N_MOD = 9
EPS = 1e-6
HEAD_DIM = 64
N_Q_HEADS = 8
N_KV_HEADS = 2
Q_WIDTH = N_Q_HEADS * HEAD_DIM
KV_WIDTH = N_KV_HEADS * HEAD_DIM
ATT_BLOCK = 128
ROPE_BASE = 10000.0
NEG_INF = -1e30
GRID_W = 64
SSM_WIDTH = 512
SSM_GROUP = 16
SSM_GROUPS = 32
SSM_STATE = 64
SSM_CHUNK = 128
FOURIER_GROUPS = 4
FOURIER_GROUP_WIDTH = 256
FFT_RADIX = 64
FFT_PITCH = 72
FFN_CHUNK = 256
LANES = 128
VMEM_LIMIT = 56 * 1024 * 1024


def _dot(a, b):
    return jnp.dot(a, b, preferred_element_type=F32)


def _dot_nt(a, b):
    return lax.dot_general(a, b, (((1,), (1,)), ((), ())), preferred_element_type=F32)


def _split(a):
    hi = a.astype(BF16)
    lo = (a - hi.astype(F32)).astype(BF16)
    return hi, lo


def _dot3(a, b):
    a_hi, a_lo = _split(a)
    b_hi, b_lo = _split(b)
    return _dot(a_hi, b_hi) + _dot(a_lo, b_hi) + _dot(a_hi, b_lo)


def _norm_mod(x, gain, shift, scale):
    ms = jnp.mean(x * x, axis=-1, keepdims=True)
    y = x * lax.rsqrt(ms + EPS) * gain
    return y * (1.0 + scale) + shift


def _params(*sem):
    return pltpu.CompilerParams(dimension_semantics=sem, vmem_limit_bytes=VMEM_LIMIT)


def _resident(shape):
    nd = len(shape)
    return pl.BlockSpec(shape, lambda *_: (0,) * nd, pipeline_mode=pl.Buffered(1))


def _ada_kernel(cond_ref, w_ref, b_ref, o_ref):
    a = cond_ref[...]
    a = a * jax.nn.sigmoid(a)
    o_ref[0] = _dot3(a, w_ref[0]) + b_ref[0]


def _ada_mod(cond, w_ada, b_ada):
    depth, d, nd = w_ada.shape
    rows = cond.shape[0]
    tn = nd // 8
    return pl.pallas_call(
        _ada_kernel,
        grid=(depth, nd // tn),
        in_specs=[
            pl.BlockSpec((rows, d), lambda l, j: (0, 0)),
            pl.BlockSpec((1, d, tn), lambda l, j: (l, 0, j)),
            pl.BlockSpec((1, 1, tn), lambda l, j: (l, 0, j)),
        ],
        out_specs=pl.BlockSpec((1, rows, tn), lambda l, j: (l, 0, j)),
        out_shape=jax.ShapeDtypeStruct((depth, rows, nd), F32),
        compiler_params=_params("parallel", "parallel"),
        name="ada_mod",
    )(cond, w_ada, b_ada.reshape(depth, 1, nd))


def _ffn_kernel(*refs, row0, grow, n_mix, emit_rstd):
    x_ref, mod_ref, gain_ref, w1_ref, w3_ref, w2_ref = refs[:6]
    mix_refs = refs[6:6 + 2 * n_mix]
    out_refs = refs[6 + 2 * n_mix:-1]
    acc_ref = refs[-1]
    o_ref = out_refs[0]
    x = x_ref[0]
    if n_mix:
        y = _dot(mix_refs[0][0], mix_refs[1][...])
        for k in range(1, n_mix):
            y = y + _dot(mix_refs[2 * k][0], mix_refs[2 * k + 1][...])
        x = x + mod_ref[0, 5:6, :] * y
        o_ref[0] = x
    shift = mod_ref[0, row0:row0 + 1, :]
    scale = mod_ref[0, row0 + 1:row0 + 2, :]
    gate = mod_ref[0, row0 + 2:row0 + 3, :]
    h = _norm_mod(x, gain_ref[grow:grow + 1, :], shift, scale).astype(BF16)
    acc_ref[...] = jnp.zeros_like(acc_ref)

    def body(c, carry):
        a = _dot(h, w1_ref[c])
        b = _dot(h, w3_ref[c])
        g = (a * jax.nn.sigmoid(a) * b).astype(BF16)
        acc_ref[...] += _dot(g, w2_ref[c])
        return carry

    lax.fori_loop(0, w1_ref.shape[0], body, 0, unroll=True)
    base = o_ref[0] if n_mix else x_ref[0]
    o = base + (0.5 * gate) * acc_ref[...]
    o_ref[0] = o
    if emit_rstd:
        rstd = lax.rsqrt(jnp.mean(o * o, axis=-1, keepdims=True) + EPS)
        out_refs[1][0] = jnp.broadcast_to(rstd, out_refs[1].shape[1:])


def _half_ffn(s, mod, row0, gains, grow, w1c, w3c, w2c, mod_index=None, mix=(), emit_rstd=False, tm=512):
    bs, ls, d = s.shape
    tm = min(tm, ls)
    if mod_index is None:
        mod_map = lambda b, i: (b, 0, 0)
    else:
        mod_map = lambda b, i: (mod_index, 0, 0)
    tok = lambda w: pl.BlockSpec((1, tm, w), lambda b, i: (b, i, 0))
    in_specs = [tok(d), pl.BlockSpec((1, N_MOD, d), mod_map), _resident(gains.shape),
                _resident(w1c.shape), _resident(w3c.shape), _resident(w2c.shape)]
    args = [s, mod, gains, w1c, w3c, w2c]
    for z, w in mix:
        in_specs += [tok(z.shape[2]), _resident(w.shape)]
        args += [z, w]
    out_specs = [tok(d)]
    out_shape = [jax.ShapeDtypeStruct(s.shape, F32)]
    if emit_rstd:
        out_specs.append(tok(LANES))
        out_shape.append(jax.ShapeDtypeStruct((bs, ls, LANES), F32))
    outs = pl.pallas_call(
        functools.partial(_ffn_kernel, row0=row0, grow=grow, n_mix=len(mix), emit_rstd=emit_rstd),
        grid=(bs, ls // tm),
        in_specs=in_specs,
        out_specs=out_specs,
        out_shape=out_shape,
        scratch_shapes=[pltpu.VMEM((tm, d), F32)],
        compiler_params=_params("parallel", "parallel"),
        name="half_ffn",
    )(*args)
    return outs if emit_rstd else outs[0]


def _ffn_weights(w1, w3, w2):
    d, f = w1.shape
    nf = f // FFN_CHUNK
    w1c = w1.astype(BF16).reshape(d, nf, FFN_CHUNK).transpose(1, 0, 2)
    w3c = w3.astype(BF16).reshape(d, nf, FFN_CHUNK).transpose(1, 0, 2)
    w2c = w2.astype(BF16).reshape(nf, FFN_CHUNK, d)
    return w1c, w3c, w2c


def _head_norm(t, bd, gain):
    hi, lo = _split(t * t)
    ms = (_dot(hi, bd) + _dot(lo, bd)) * (1.0 / HEAD_DIM)
    return t * lax.rsqrt(ms + EPS) * gain


def _rope(t, cos, sin):
    w = t.shape[1]
    half = HEAD_DIM // 2
    lane = lax.broadcasted_iota(jnp.int32, t.shape, 1)
    first = (lane % HEAD_DIM) < half
    rot = jnp.where(first, pltpu.roll(t, w - half, 1), pltpu.roll(t, half, 1))
    reps = w // LANES
    c = jnp.concatenate([cos] * reps, axis=1) if reps > 1 else cos
    s = jnp.concatenate([sin] * reps, axis=1) if reps > 1 else sin
    return t * c + rot * s


def _dup_heads(t):
    lane = lax.broadcasted_iota(jnp.int32, t.shape, 1)
    low = lane < HEAD_DIM
    r = pltpu.roll(t, HEAD_DIM, 1)
    return jnp.concatenate([jnp.where(low, t, r), jnp.where(low, r, t)], axis=1)


def _inproj_kernel(x_ref, mod_ref, gain_ref, wqkv_ref, wut_ref, bd_ref, qg_ref, kg_ref, cos_ref, sin_ref,
                   q_ref, kk_ref, vv_ref, ut_ref):
    x = x_ref[0]
    h = _norm_mod(x, gain_ref[1:2, :], mod_ref[0, 3:4, :], mod_ref[0, 4:5, :]).astype(BF16)
    qkv = _dot(h, wqkv_ref[...])
    q = qkv[:, :Q_WIDTH]
    k = qkv[:, Q_WIDTH:Q_WIDTH + KV_WIDTH]
    v = qkv[:, Q_WIDTH + KV_WIDTH:]
    cos = cos_ref[...]
    sin = sin_ref[...]
    q = _rope(_head_norm(q, bd_ref[...], qg_ref[...]), cos, sin)
    k = _rope(_head_norm(k, bd_ref[:KV_WIDTH, :KV_WIDTH], kg_ref[...]), cos, sin)
    q_ref[0] = (q * (HEAD_DIM ** -0.5)).astype(BF16)
    kk_ref[0] = _dup_heads(k).astype(BF16)
    vv_ref[0] = _dup_heads(v).astype(BF16)
    ut_ref[...] = _dot_nt(wut_ref[...], h)


def _in_proj(s, mod, gains, wqkv, wut, bd, qg, kg, cos, sin, mod_index=None, tm=512):
    bs, ls, d = s.shape
    tm = min(tm, ls)
    nt = ls // tm
    if mod_index is None:
        mod_map = lambda b, i: (b, 0, 0)
    else:
        mod_map = lambda b, i: (mod_index, 0, 0)
    tok = lambda w: pl.BlockSpec((1, tm, w), lambda b, i: (b, i, 0))
    return pl.pallas_call(
        _inproj_kernel,
        grid=(bs, nt),
        in_specs=[
            tok(d),
            pl.BlockSpec((1, N_MOD, d), mod_map),
            _resident(gains.shape),
            _resident(wqkv.shape),
            _resident(wut.shape),
            _resident(bd.shape),
            _resident(qg.shape),
            _resident(kg.shape),
            pl.BlockSpec((tm, LANES), lambda b, i: (i, 0)),
            pl.BlockSpec((tm, LANES), lambda b, i: (i, 0)),
        ],
        out_specs=[
            tok(Q_WIDTH), tok(2 * KV_WIDTH), tok(2 * KV_WIDTH),
            pl.BlockSpec((SSM_WIDTH, tm), lambda b, i: (0, b * nt + i)),
        ],
        out_shape=[
            jax.ShapeDtypeStruct((bs, ls, Q_WIDTH), BF16),
            jax.ShapeDtypeStruct((bs, ls, 2 * KV_WIDTH), BF16),
            jax.ShapeDtypeStruct((bs, ls, 2 * KV_WIDTH), BF16),
            jax.ShapeDtypeStruct((SSM_WIDTH, bs * ls), F32),
        ],
        compiler_params=_params("parallel", "parallel"),
        name="in_proj",
    )(s, mod, gains, wqkv, wut, bd, qg, kg, cos, sin)


def _rope_tables(n_lat):
    rows = n_lat // GRID_W
    row = jnp.repeat(jnp.arange(rows, dtype=F32), GRID_W)
    col = jnp.tile(jnp.arange(GRID_W, dtype=F32), rows)
    n_freq = HEAD_DIM // 4
    inv_freq = 1.0 / (ROPE_BASE ** (jnp.arange(n_freq, dtype=F32) / n_freq))
    ang = jnp.concatenate([row[:, None] * inv_freq, col[:, None] * inv_freq], axis=-1)
    cos, sin = jnp.cos(ang), jnp.sin(ang)
    cos_t = jnp.tile(cos, (1, LANES // (HEAD_DIM // 2)))
    sin_t = jnp.tile(jnp.concatenate([-sin, sin], axis=-1), (1, LANES // HEAD_DIM))
    return cos_t, sin_t


def _attn_kernel(sink_ref, q_ref, kk_ref, vv_ref, kkc_ref, vvc_ref, o_ref):
    i = pl.program_id(1)
    nb = pl.num_programs(1)
    blk = ATT_BLOCK
    q = q_ref[0]
    lane = lax.broadcasted_iota(jnp.int32, (blk, LANES), 1)
    low = lane < HEAD_DIM
    rel = lane - lax.broadcasted_iota(jnp.int32, (blk, LANES), 0)
    ok_prev = rel >= jnp.where(i > 0, 0, 2 * blk)
    ok_next = rel <= jnp.where(i < nb - 1, 0, -2 * blk)
    j_prev = jnp.maximum(i - 1, 0)
    j_next = jnp.minimum(i + 1, nb - 1)
    zero = jnp.zeros_like(q[:, :LANES])
    cols = []
    for h in range(N_KV_HEADS):
        hs = slice(h * LANES, (h + 1) * LANES)

        def rows(ref, j):
            return ref[0, pl.ds(pl.multiple_of(j * blk, blk), blk), hs]

        keys = jnp.concatenate([rows(kk_ref, j_prev), rows(kk_ref, i), rows(kk_ref, j_next), kkc_ref[0, :, hs]], axis=0)
        vals = jnp.concatenate([rows(vv_ref, j_prev), rows(vv_ref, i), rows(vv_ref, j_next), vvc_ref[0, :, hs]], axis=0)
        qs = []
        for col in (2 * h, 2 * h + 1):
            qc = q[:, col * LANES:(col + 1) * LANES]
            qs.append(jnp.where(low, qc, zero))
            qs.append(jnp.where(low, zero, qc))
        s_all = _dot_nt(jnp.concatenate(qs, axis=0), keys)
        ps = []
        inv = []
        for g in range(4):
            s = s_all[g * blk:(g + 1) * blk]
            s = jnp.concatenate([
                jnp.where(ok_prev, s[:, :blk], NEG_INF),
                s[:, blk:2 * blk],
                jnp.where(ok_next, s[:, 2 * blk:3 * blk], NEG_INF),
                s[:, 3 * blk:],
            ], axis=1)
            sink = sink_ref[4 * h + g]
            m = jnp.maximum(jnp.max(s, axis=-1, keepdims=True), sink)
            p = jnp.exp(s - m)
            den = jnp.sum(p, axis=-1, keepdims=True) + jnp.exp(sink - m)
            ps.append(p.astype(BF16))
            inv.append(1.0 / den)
        o = _dot(jnp.concatenate(ps, axis=0), vals)
        og = [o[g * blk:(g + 1) * blk] * inv[g] for g in range(4)]
        cols.append(jnp.where(low, og[0], og[1]))
        cols.append(jnp.where(low, og[2], og[3]))
    o_ref[0] = jnp.concatenate(cols, axis=1).astype(BF16)


def _attention(q, kk, vv, kkc, vvc, sink):
    bs, ls, _ = q.shape
    nb = ls // ATT_BLOCK
    n_ctx = kkc.shape[1]
    full = lambda n: pl.BlockSpec((1, n, 2 * KV_WIDTH), lambda b, i: (b, 0, 0))
    return pl.pallas_call(
        _attn_kernel,
        grid=(bs, nb),
        in_specs=[
            pl.BlockSpec(memory_space=pltpu.SMEM),
            pl.BlockSpec((1, ATT_BLOCK, Q_WIDTH), lambda b, i: (b, i, 0)),
            full(ls), full(ls), full(n_ctx), full(n_ctx),
        ],
        out_specs=pl.BlockSpec((1, ATT_BLOCK, Q_WIDTH), lambda b, i: (b, i, 0)),
        out_shape=jax.ShapeDtypeStruct((bs, ls, Q_WIDTH), BF16),
        compiler_params=_params("parallel", "arbitrary"),
        name="window_attention",
    )(sink, q, kk, vv, kkc, vvc)


def _cpow(lr, li, dt, tau):
    mag = jnp.exp(lr * dt * tau)
    ang = li * dt * tau
    return mag * jnp.cos(ang), mag * jnp.sin(ang)


def _ssm_prep_kernel(lrr_ref, lir_ref, lrc_ref, lic_ref, ldt_ref, btr_ref, bti_ref, btrr_ref, btri_ref,
                     crr_ref, cri_ref, ctr_ref, cti_ref, kcat_ref, wst_ref, wcar_ref, arow_ref):
    p = SSM_STATE
    t = SSM_CHUNK
    lane_t = lax.broadcasted_iota(jnp.int32, (1, t), 1).astype(F32)
    sub_t = lax.broadcasted_iota(jnp.int32, (t, 1), 0).astype(F32)
    kparts = []
    for d in range(2):
        dt = jnp.exp(ldt_ref[0, d])
        lr, li = lrr_ref[0, d], lir_ref[0, d]
        lrc, lic = lrc_ref[0, d], lic_ref[0, d]
        ar, ai = _cpow(lr, li, dt, 1.0)
        nr = ar - 1.0
        den = lr * lr + li * li
        fr = (nr * lr + ai * li) / den
        fi = (ai * lr - nr * li) / den
        bt_r, bt_i = btr_ref[0, d], bti_ref[0, d]
        bbr = fr * bt_r - fi * bt_i
        bbi = fr * bt_i + fi * bt_r
        rep_r, rep_i = btrr_ref[0, d], btri_ref[0, d]
        bbr_rep = fr * rep_r - fi * rep_i
        bbi_rep = fr * rep_i + fi * rep_r
        c_r, c_i = crr_ref[0, d], cri_ref[0, d]
        m1 = c_r * bbr_rep - c_i * bbi_rep
        m2 = c_r * bbi_rep + c_i * bbr_rep
        tau = lane_t if d == 0 else (float(t) - lane_t)
        e_r, e_i = _cpow(lrc, lic, dt, tau)
        kmat = _dot3(m1, e_r) - _dot3(m2, e_i)
        if d == 0:
            kparts.append(kmat)
        else:
            lag0 = jnp.sum(m1, axis=1, keepdims=True)
            first = lax.broadcasted_iota(jnp.int32, (1, t), 1) == 0
            kparts[0] = kparts[0] + jnp.where(first, lag0, 0.0)
            kparts.insert(0, kmat)
        tau_s = (float(t - 1) - sub_t) if d == 0 else sub_t
        et_r, et_i = _cpow(lr, li, dt, tau_s)
        for i in range(SSM_GROUP):
            re = et_r * bbr[i:i + 1] - et_i * bbi[i:i + 1]
            im = et_r * bbi[i:i + 1] + et_i * bbr[i:i + 1]
            wst_ref[0, i * t:(i + 1) * t, d * 2 * p:(d + 1) * 2 * p] = jnp.concatenate([re, im], axis=1).astype(BF16)
        tau_c = (lane_t + 1.0) if d == 0 else (float(t) - lane_t)
        ec_r, ec_i = _cpow(lrc, lic, dt, tau_c)
        ct_r, ct_i = ctr_ref[0, d], cti_ref[0, d]
        for jo in range(SSM_GROUP):
            cr = ct_r[:, jo:jo + 1]
            ci = ct_i[:, jo:jo + 1]
            wcar_ref[0, d * 2 * p:d * 2 * p + p, jo * t:(jo + 1) * t] = (cr * ec_r - ci * ec_i).astype(BF16)
            wcar_ref[0, d * 2 * p + p:(d + 1) * 2 * p, jo * t:(jo + 1) * t] = (-(cr * ec_i + ci * ec_r)).astype(BF16)
        at_r, at_i = _cpow(lr, li, dt, float(t))
        arow_ref[0, d, 0:1, :] = jnp.concatenate([at_r, at_r], axis=1)
        arow_ref[0, d, 1:2, :] = jnp.concatenate([-at_i, at_i], axis=1)
    kcat_ref[0] = jnp.concatenate(kparts, axis=1)


def _ssm_prep(lam_re, lam_im, log_dt, b_re, b_im, c_re, c_im):
    g, p, n, t = SSM_GROUPS, SSM_STATE, SSM_GROUP, SSM_CHUNK
    gd = lambda a: jnp.swapaxes(a, 0, 1)
    lrr = gd(lam_re)[:, :, None, :]
    lir = gd(lam_im)[:, :, None, :]
    lrc = gd(lam_re)[:, :, :, None]
    lic = gd(lam_im)[:, :, :, None]
    ldt = gd(log_dt)[:, :, None, None]
    bt_r = jnp.swapaxes(gd(b_re), 2, 3)
    bt_i = jnp.swapaxes(gd(b_im), 2, 3)
    btr_r = jnp.tile(bt_r, (1, 1, n, 1))
    btr_i = jnp.tile(bt_i, (1, 1, n, 1))
    cr_r = jnp.repeat(gd(c_re), n, axis=2)
    cr_i = jnp.repeat(gd(c_im), n, axis=2)
    ct_r = jnp.swapaxes(gd(c_re), 2, 3)
    ct_i = jnp.swapaxes(gd(c_im), 2, 3)
    args = (lrr, lir, lrc, lic, ldt, bt_r, bt_i, btr_r, btr_i, cr_r, cr_i, ct_r, ct_i)
    spec = lambda a: pl.BlockSpec((1,) + a.shape[1:], lambda i: (i,) + (0,) * (a.ndim - 1))
    return pl.pallas_call(
        _ssm_prep_kernel,
        grid=(g,),
        in_specs=[spec(a) for a in args],
        out_specs=[
            pl.BlockSpec((1, n * n, 2 * t), lambda i: (i, 0, 0)),
            pl.BlockSpec((1, n * t, 4 * p), lambda i: (i, 0, 0)),
            pl.BlockSpec((1, 4 * p, n * t), lambda i: (i, 0, 0)),
            pl.BlockSpec((1, 2, 2, 2 * p), lambda i: (i, 0, 0, 0)),
        ],
        out_shape=[
            jax.ShapeDtypeStruct((g, n * n, 2 * t), F32),
            jax.ShapeDtypeStruct((g, n * t, 4 * p), BF16),
            jax.ShapeDtypeStruct((g, 4 * p, n * t), BF16),
            jax.ShapeDtypeStruct((g, 2, 2, 2 * p), F32),
        ],
        compiler_params=_params("parallel"),
        name="ssm_prep",
    )(*args)


def _ssm_kernel(u_ref, uc_ref, kcat_ref, wst_ref, wcar_ref, arow_ref, d_ref, y_ref,
                tz_ref, s_ref, sc_ref, h_ref, *, nb, nc, ncc):
    n, t, p = SSM_GROUP, SSM_CHUNK, SSM_STATE

    def build(i, carry):
        for jo in range(n):
            row = kcat_ref[0, pl.ds(jo * n + i, 1), :]
            blk = pltpu.roll(jnp.broadcast_to(row, (t, 2 * t)), 0, 1, stride=1, stride_axis=0)
            tz_ref[jo // 2, pl.ds(pl.multiple_of(i * t, t), t), (jo % 2) * t:(jo % 2 + 1) * t] = blk[:, t:].astype(BF16)
        return carry

    lax.fori_loop(0, n, build, 0)

    x = jnp.concatenate([u_ref[i].astype(BF16) for i in range(n)], axis=1)
    xc = jnp.concatenate([uc_ref[i].astype(BF16) for i in range(n)], axis=1)
    s_all = _dot(x, wst_ref[0])
    sc_all = _dot(xc, wst_ref[0])
    for d in range(2):
        s_ref[d] = s_all[:, d * 2 * p:(d + 1) * 2 * p]
        sc_ref[d] = sc_all[:, d * 2 * p:(d + 1) * 2 * p]

    def scan(d, ctx_order, chunk_of_step):
        a1, a2 = arow_ref[0, d, 0:1, :], arow_ref[0, d, 1:2, :]

        def cmul(h):
            return a1 * h + a2 * pltpu.roll(h, p, 1)

        h = jnp.zeros((nb, 2 * p), F32)
        for cc in ctx_order:
            h = cmul(h) + sc_ref[d, pl.ds(cc, nb, stride=ncc), :]

        def body(k, h):
            c = chunk_of_step(k)
            h_ref[d, pl.ds(c, nb, stride=nc), :] = h
            return cmul(h) + s_ref[d, pl.ds(c, nb, stride=nc), :]

        lax.fori_loop(0, nc, body, h)

    scan(0, range(ncc), lambda k: k)
    scan(1, reversed(range(ncc)), lambda k: nc - 1 - k)

    hb = jnp.concatenate([h_ref[0], h_ref[1]], axis=1).astype(BF16)
    for jp in range(n // 2):
        y = _dot(x, tz_ref[jp]) + _dot(hb, wcar_ref[0, :, jp * 2 * t:(jp + 1) * 2 * t])
        for k in range(2):
            jo = 2 * jp + k
            y_ref[jo] = y[:, k * t:(k + 1) * t] + d_ref[jo] * u_ref[jo]


def _ssm(ut, utc, kcat, wst, wcar, arow, d_skip, nb):
    n, t, p, g = SSM_GROUP, SSM_CHUNK, SSM_STATE, SSM_GROUPS
    rows = ut.shape[1]
    rows_c = utc.shape[1]
    nc, ncc = rows // nb, rows_c // nb
    d3 = jnp.broadcast_to(d_skip.astype(F32)[:, None, None], (SSM_WIDTH, 1, t))
    return pl.pallas_call(
        functools.partial(_ssm_kernel, nb=nb, nc=nc, ncc=ncc),
        grid=(g,),
        in_specs=[
            pl.BlockSpec((n, rows, t), lambda i: (i, 0, 0)),
            pl.BlockSpec((n, rows_c, t), lambda i: (i, 0, 0)),
            pl.BlockSpec((1, n * n, 2 * t), lambda i: (i, 0, 0)),
            pl.BlockSpec((1, n * t, 4 * p), lambda i: (i, 0, 0)),
            pl.BlockSpec((1, 4 * p, n * t), lambda i: (i, 0, 0)),
            pl.BlockSpec((1, 2, 2, 2 * p), lambda i: (i, 0, 0, 0)),
            pl.BlockSpec((n, 1, t), lambda i: (i, 0, 0)),
        ],
        out_specs=pl.BlockSpec((n, rows, t), lambda i: (i, 0, 0)),
        out_shape=jax.ShapeDtypeStruct((SSM_WIDTH, rows, t), F32),
        scratch_shapes=[
            pltpu.VMEM((n // 2, n * t, 2 * t), BF16),
            pltpu.VMEM((2, rows, 2 * p), F32),
            pltpu.VMEM((2, rows_c, 2 * p), F32),
            pltpu.VMEM((2, rows, 2 * p), F32),
        ],
        compiler_params=_params("parallel"),
        name="ssm_toeplitz",
    )(ut, utc, kcat, wst, wcar, arow, d3)


def _glu_kernel(yt_ref, wglut_ref, s_ref, flat_ref):
    t = SSM_CHUNK
    nck = yt_ref.shape[1]

    def flatten(ch, carry):
        flat_ref[pl.ds(pl.multiple_of(ch * nck, nck), nck), :] = yt_ref[ch]
        return carry

    lax.fori_loop(0, yt_ref.shape[0], flatten, 0)
    for j in range(nck):
        g = jax.nn.gelu(flat_ref[pl.ds(j, SSM_WIDTH, stride=nck), :])
        z = _dot(wglut_ref[...], g.astype(BF16))
        s_ref[j * t:(j + 1) * t, :] = (g * jax.nn.sigmoid(z)).T.astype(BF16)


def _ssm_glu(yt, wglut, bs, ls):
    nck = 8
    rows = yt.shape[1]
    s = pl.pallas_call(
        _glu_kernel,
        grid=(rows // nck,),
        in_specs=[
            pl.BlockSpec((SSM_WIDTH, nck, SSM_CHUNK), lambda i: (0, i, 0)),
            _resident(wglut.shape),
        ],
        out_specs=pl.BlockSpec((nck * SSM_CHUNK, SSM_WIDTH), lambda i: (i, 0)),
        out_shape=jax.ShapeDtypeStruct((rows * SSM_CHUNK, SSM_WIDTH), BF16),
        scratch_shapes=[pltpu.VMEM((SSM_WIDTH * nck, SSM_CHUNK), F32)],
        compiler_params=_params("parallel"),
        name="ssm_glu",
    )(yt, wglut)
    return s.reshape(bs, ls, SSM_WIDTH)


def _fourier_kernel(x_ref, r_ref, mod_ref, gain_ref, lhs1_ref, twc_ref, tws_ref, lhs2_ref, cs_ref, z_ref,
                    h_ref, are_ref, aim_ref, y_ref):
    r, pitch = FFT_RADIX, FFT_PITCH
    halves = FOURIER_GROUP_WIDTH // LANES

    def slab(i):
        return pl.ds(pl.multiple_of(i * pitch, 8), r)

    def gather(ref, i):
        return jnp.concatenate([ref[hf, pl.ds(i, r, stride=pitch), :] for hf in range(halves)], axis=1)

    def put(ref, i, val):
        for hf in range(halves):
            ref[hf, slab(i), :] = val[:, hf * LANES:(hf + 1) * LANES]

    gain = gain_ref[1:2, :]
    shift, scale = mod_ref[0, 3:4, :], mod_ref[0, 4:5, :]

    def fill(n1, carry):
        rows = pl.ds(pl.multiple_of(n1 * r, r), r)
        rstd = jnp.concatenate([r_ref[0, rows, :]] * halves, axis=1)
        put(h_ref, n1, (x_ref[0, rows, :] * rstd * gain) * (1.0 + scale) + shift)
        return carry

    lax.fori_loop(0, r, fill, 0, unroll=4)

    def stage1(jg, carry):
        cols = [gather(h_ref, jg * 4 + j).astype(BF16) for j in range(4)]
        a = _dot(lhs1_ref[...], jnp.concatenate(cols, axis=0))
        ch, sh = a[:4 * r], a[4 * r:]
        trow = pl.ds(pl.multiple_of(jg * 4 * r, 4 * r), 4 * r)
        tc = jnp.concatenate([twc_ref[trow, :]] * halves, axis=1)
        ts = jnp.concatenate([tws_ref[trow, :]] * halves, axis=1)
        a_re = ch * tc - sh * ts
        a_im = -(sh * tc) - ch * ts
        for j in range(4):
            put(are_ref, jg * 4 + j, a_re[j * r:(j + 1) * r])
            put(aim_ref, jg * 4 + j, a_im[j * r:(j + 1) * r])
        return carry

    lax.fori_loop(0, r // 4, stage1, 0, unroll=4)

    def stage2(kg, carry):
        parts = [gather(are_ref, kg * 4 + j).astype(BF16) for j in range(4)]
        parts += [gather(aim_ref, kg * 4 + j).astype(BF16) for j in range(4)]
        y = _dot(lhs2_ref[...], jnp.concatenate(parts, axis=0))
        rows = pl.ds(pl.multiple_of(kg * 4 * r, 4 * r), 4 * r)
        y_ref[rows, :] = jnp.concatenate([y[:4 * r], y[4 * r:]], axis=1).astype(BF16)
        return carry

    lax.fori_loop(0, r // 4, stage2, 0, unroll=4)

    def channels(kb, carry):
        rows = pl.ds(pl.multiple_of(kb * 8 * r, 8 * r), 8 * r)
        z = _dot(y_ref[rows, :], cs_ref[...])
        for j in range(8):
            put(h_ref, kb * 8 + j, z[j * r:(j + 1) * r])
        return carry

    lax.fori_loop(0, r // 8, channels, 0, unroll=2)

    def emit(k2, carry):
        z_ref[0, pl.ds(pl.multiple_of(k2 * r, r), r), :] = gather(h_ref, k2).astype(BF16)
        return carry

    lax.fori_loop(0, r, emit, 0, unroll=4)


def _fourier_tables():
    r = FFT_RADIX
    n = r * r
    k = jnp.arange(r, dtype=jnp.int32)
    ang = ((k[:, None] * k[None, :]) % r).astype(F32) * (2.0 * math.pi / r)
    wc = jnp.cos(ang) / 8.0
    ws = jnp.sin(ang) / 8.0
    eye4 = jnp.eye(4, dtype=F32)
    bc, bs = jnp.kron(eye4, wc), jnp.kron(eye4, ws)
    lhs1 = jnp.concatenate([bc, bs], axis=0).astype(BF16)
    lhs2 = jnp.concatenate([jnp.concatenate([bc, bs], axis=1),
                            jnp.concatenate([-bs, bc], axis=1)], axis=0).astype(BF16)
    tw = ((k[:, None] * k[None, :]) % n).astype(F32) * (2.0 * math.pi / n)
    twc = jnp.broadcast_to(jnp.cos(tw).reshape(n, 1), (n, LANES))
    tws = jnp.broadcast_to(jnp.sin(tw).reshape(n, 1), (n, LANES))
    m = jnp.arange(FOURIER_GROUP_WIDTH, dtype=jnp.int32)
    angc = ((m[:, None] * m[None, :]) % FOURIER_GROUP_WIDTH).astype(F32) * (2.0 * math.pi / FOURIER_GROUP_WIDTH)
    cs = jnp.concatenate([jnp.cos(angc), jnp.sin(angc)], axis=0) / 16.0
    return lhs1, lhs2, twc, tws, cs.astype(BF16)


def _fourier(x, rstd, mod, gains):
    bs, ls, d = x.shape
    r, gw = FFT_RADIX, FOURIER_GROUP_WIDTH
    assert ls == r * r
    lhs1, lhs2, twc, tws, cs = _fourier_tables()
    scratch = pltpu.VMEM((gw // LANES, r * FFT_PITCH, LANES), F32)
    return pl.pallas_call(
        _fourier_kernel,
        grid=(bs, d // gw),
        in_specs=[
            pl.BlockSpec((1, ls, gw), lambda b, g: (b, 0, g)),
            pl.BlockSpec((1, ls, LANES), lambda b, g: (b, 0, 0)),
            pl.BlockSpec((1, N_MOD, gw), lambda b, g: (b, 0, g)),
            pl.BlockSpec((gains.shape[0], gw), lambda b, g: (0, g)),
            _resident(lhs1.shape), _resident(twc.shape), _resident(tws.shape),
            _resident(lhs2.shape), _resident(cs.shape),
        ],
        out_specs=pl.BlockSpec((1, ls, gw), lambda b, g: (b, 0, g)),
        out_shape=jax.ShapeDtypeStruct((bs, ls, d), BF16),
        scratch_shapes=[scratch, scratch, scratch, pltpu.VMEM((ls, 2 * gw), BF16)],
        compiler_params=_params("parallel", "parallel"),
        name="fourier_mix",
    )(x, rstd, mod, gains, lhs1, twc, tws, lhs2, cs)


def _even_layer(x, s_ctx, mod, gains, ffn_a, ffn_b, w_in, q_gain, k_gain, sink, ssm, d_skip, w_glu, w_out):
    bs, ls, d = x.shape
    ctx_index = bs
    x = _half_ffn(x, mod, 0, gains, 0, *ffn_a)
    s_ctx = _half_ffn(s_ctx, mod, 0, gains, 0, *ffn_a, mod_index=ctx_index)

    wqkv = w_in[:, :Q_WIDTH + 2 * KV_WIDTH].astype(BF16)
    wut = w_in[:, Q_WIDTH + 2 * KV_WIDTH:].T.astype(BF16)
    bd = jnp.kron(jnp.eye(N_Q_HEADS, dtype=F32), jnp.ones((HEAD_DIM, HEAD_DIM), F32)).astype(BF16)
    qg = jnp.tile(q_gain.astype(F32), N_Q_HEADS)[None, :]
    kg = jnp.tile(k_gain.astype(F32), N_KV_HEADS)[None, :]
    cos, sin = _rope_tables(ls)
    n_ctx = s_ctx.shape[1]
    one = jnp.ones((n_ctx, LANES), F32)
    q, kk, vv, ut = _in_proj(x, mod, gains, wqkv, wut, bd, qg, kg, cos, sin)
    _, kkc, vvc, utc = _in_proj(s_ctx, mod, gains, wqkv, wut, bd, qg, kg, one, 0.0 * one, mod_index=ctx_index)

    attn = _attention(q, kk, vv, kkc, vvc, sink.astype(F32))

    t = SSM_CHUNK
    kcat, wst, wcar, arow = _ssm_prep(*ssm)
    yt = _ssm(ut.reshape(SSM_WIDTH, bs * ls // t, t), utc.reshape(SSM_WIDTH, bs * n_ctx // t, t),
              kcat, wst, wcar, arow, d_skip, bs)

    s = _ssm_glu(yt, w_glu.T.astype(BF16), bs, ls)
    wo = w_out.astype(BF16)
    return _half_ffn(x, mod, 6, gains, 2, *ffn_b, mix=((attn, wo[:Q_WIDTH]), (s, wo[Q_WIDTH:])))


def _odd_layer(x, mod, gains, ffn_a, ffn_b, w_f):
    x, rstd = _half_ffn(x, mod, 0, gains, 0, *ffn_a, emit_rstd=True)
    z = _fourier(x, rstd, mod, gains)
    return _half_ffn(x, mod, 6, gains, 2, *ffn_b, mix=((z, w_f.astype(BF16)),))


def kernel(x, c, ctx, c_ctx, w_ada, b_ada, norm_gain, ffn_w1, ffn_w3, ffn_w2, w_in, q_gain, k_gain, sink_logit, ssm_lam_re, ssm_lam_im, ssm_log_dt, ssm_b_re, ssm_b_im, ssm_c_re, ssm_c_im, ssm_d, ssm_w_glu, w_out, fourier_w_out):
    bs, ls, d = x.shape
    depth = w_ada.shape[0]
    assert depth == 2 and d == D_MODEL
    rows = 16
    cond = jnp.concatenate([c, c_ctx[None, :], jnp.zeros((rows - bs - 1, d), F32)], axis=0)
    mods = _ada_mod(cond, w_ada, b_ada)[:, :bs + 1].reshape(depth, bs + 1, N_MOD, d)

    ffn = [[_ffn_weights(ffn_w1[l, k], ffn_w3[l, k], ffn_w2[l, k]) for k in range(2)] for l in range(depth)]
    ssm = (ssm_lam_re[0], ssm_lam_im[0], ssm_log_dt[0], ssm_b_re[0], ssm_b_im[0], ssm_c_re[0], ssm_c_im[0])
    x = _even_layer(x, ctx, mods[0], norm_gain[0], ffn[0][0], ffn[0][1], w_in[0], q_gain[0], k_gain[0],
                    sink_logit[0], ssm, ssm_d[0], ssm_w_glu[0], w_out[0])
    return _odd_layer(x, mods[1], norm_gain[1], ffn[1][0], ffn[1][1], fourier_w_out[0])
```

```python
import functools
import math

import jax
import jax.numpy as jnp
from jax import lax
from jax.experimental import pallas as pl
from jax.experimental.pallas import tpu as pltpu

F32 = jnp.float32
BF16 = jnp.bfloat16

D_MODEL = 1024
N_MOD = 9
EPS = 1e-6
HEAD_DIM = 64
N_Q_HEADS = 8
N_KV_HEADS = 2
Q_WIDTH = N_Q_HEADS * HEAD_DIM
KV_WIDTH = N_KV_HEADS * HEAD_DIM
ATT_BLOCK = 128
ROPE_BASE = 10000.0
NEG_INF = -1e30
GRID_W = 64
SSM_WIDTH = 512
SSM_GROUP = 16
SSM_GROUPS = 32
SSM_STATE = 64
SSM_ROW = 128
SSM_CHUNK = 64
FOURIER_GROUPS = 4
FOURIER_GROUP_WIDTH = 256
FFT_RADIX = 64
FFT_PITCH = 72
FFN_CHUNK = 256
LANES = 128
VMEM_LIMIT = 56 * 1024 * 1024


def _dot(a, b):
    return jnp.dot(a, b, preferred_element_type=F32)


def _dot_w(a, w):
    return lax.dot_general(a, w, (((1,), (0,)), ((), ())), preferred_element_type=F32)


def _dot_nt(a, b):
    return lax.dot_general(a, b, (((1,), (1,)), ((), ())), preferred_element_type=F32)


def _split(a):
    hi = a.astype(BF16)
    lo = (a - hi.astype(F32)).astype(BF16)
    return hi, lo


def _dot3(a, b):
    a_hi, a_lo = _split(a)
    b_hi, b_lo = _split(b)
    return _dot(a_hi, b_hi) + _dot(a_lo, b_hi) + _dot(a_hi, b_lo)


def _norm_mod(x, gain, shift, scale):
    ms = jnp.mean(x * x, axis=-1, keepdims=True)
    y = x * lax.rsqrt(ms + EPS) * gain
    return y * (1.0 + scale) + shift


def _params(*sem):
    return pltpu.CompilerParams(dimension_semantics=sem, vmem_limit_bytes=VMEM_LIMIT)


def _resident(shape):
    nd = len(shape)
    return pl.BlockSpec(shape, lambda *_: (0,) * nd, pipeline_mode=pl.Buffered(1))


def _ada_kernel(cond_ref, w_ref, b_ref, o_ref):
    a = cond_ref[...]
    a = a * jax.nn.sigmoid(a)
    o_ref[0] = _dot3(a, w_ref[0]) + b_ref[0]


def _ada_mod(cond, w_ada, b_ada):
    depth, d, nd = w_ada.shape
    rows = cond.shape[0]
    tn = nd // 8
    return pl.pallas_call(
        _ada_kernel,
        grid=(depth, nd // tn),
        in_specs=[
            pl.BlockSpec((rows, d), lambda l, j: (0, 0)),
            pl.BlockSpec((1, d, tn), lambda l, j: (l, 0, j)),
            pl.BlockSpec((1, 1, tn), lambda l, j: (l, 0, j)),
        ],
        out_specs=pl.BlockSpec((1, rows, tn), lambda l, j: (l, 0, j)),
        out_shape=jax.ShapeDtypeStruct((depth, rows, nd), F32),
        compiler_params=_params("parallel", "parallel"),
        name="ada_mod",
    )(cond, w_ada, b_ada.reshape(depth, 1, nd))


def _ffn_kernel(*refs, row0, grow, n_mix, emit_rstd):
    x_ref, mod_ref, gain_ref, w1_ref, w3_ref, w2_ref = refs[:6]
    mix_refs = refs[6:6 + 2 * n_mix]
    out_refs = refs[6 + 2 * n_mix:-1]
    acc_ref = refs[-1]
    o_ref = out_refs[0]
    x = x_ref[0]
    if n_mix:
        y = _dot_w(mix_refs[0][0], mix_refs[1][...])
        for k in range(1, n_mix):
            y = y + _dot_w(mix_refs[2 * k][0], mix_refs[2 * k + 1][...])
        x = x + mod_ref[0, 5:6, :] * y
        o_ref[0] = x
    shift = mod_ref[0, row0:row0 + 1, :]
    scale = mod_ref[0, row0 + 1:row0 + 2, :]
    gate = mod_ref[0, row0 + 2:row0 + 3, :]
    h = _norm_mod(x, gain_ref[grow:grow + 1, :], shift, scale)
    fc = FFN_CHUNK
    for c in range(w1_ref.shape[1] // fc):
        a = _dot_w(h, w1_ref[:, c * fc:(c + 1) * fc])
        b = _dot_w(h, w3_ref[:, c * fc:(c + 1) * fc])
        g = a * jax.nn.sigmoid(a) * b
        part = _dot_w(g, w2_ref[c * fc:(c + 1) * fc, :])
        if c == 0:
            acc_ref[...] = part
        else:
            acc_ref[...] += part
    base = o_ref[0] if n_mix else x_ref[0]
    o = base + (0.5 * gate) * acc_ref[...]
    o_ref[0] = o
    if emit_rstd:
        rstd = lax.rsqrt(jnp.mean(o * o, axis=-1, keepdims=True) + EPS)
        out_refs[1][0] = jnp.broadcast_to(rstd, out_refs[1].shape[1:])


def _picked(arr, lead, shape, start=None):
    start = start or (0,) * len(shape)
    return pl.BlockSpec((None,) * len(lead) + tuple(shape), lambda *_: tuple(lead) + tuple(start),
                        pipeline_mode=pl.Buffered(1))


def _half_ffn(s, mod, row0, gains, grow, ffn_w, lk, mod_index=None, mix=(), emit_rstd=False, tm=512):
    bs, ls, d = s.shape
    tm = min(tm, ls)
    if mod_index is None:
        mod_map = lambda b, i: (b, 0, 0)
    else:
        mod_map = lambda b, i: (mod_index, 0, 0)
    tok = lambda w: pl.BlockSpec((1, tm, w), lambda b, i: (b, i, 0))
    in_specs = [tok(d), pl.BlockSpec((1, N_MOD, d), mod_map), _resident(gains.shape)]
    in_specs += [_picked(w, lk, w.shape[2:]) for w in ffn_w]
    args = [s, mod, gains, *ffn_w]
    for z, w, w_spec in mix:
        in_specs += [tok(z.shape[2]), w_spec]
        args += [z, w]
    out_specs = [tok(d)]
    out_shape = [jax.ShapeDtypeStruct(s.shape, F32)]
    if emit_rstd:
        out_specs.append(tok(LANES))
        out_shape.append(jax.ShapeDtypeStruct((bs, ls, LANES), F32))
    outs = pl.pallas_call(
        functools.partial(_ffn_kernel, row0=row0, grow=grow, n_mix=len(mix), emit_rstd=emit_rstd),
        grid=(bs, ls // tm),
        in_specs=in_specs,
        out_specs=out_specs,
        out_shape=out_shape,
        scratch_shapes=[pltpu.VMEM((tm, d), F32)],
        compiler_params=_params("parallel", "parallel"),
        name="half_ffn",
    )(*args)
    return outs if emit_rstd else outs[0]


def _head_norm(t, bd, gain):
    ms = _dot((t * t).astype(BF16), bd) * (1.0 / HEAD_DIM)
    return t * lax.rsqrt(ms + EPS) * gain


def _rope(t, cos, sin):
    w = t.shape[1]
    half = HEAD_DIM // 2
    lane = lax.broadcasted_iota(jnp.int32, t.shape, 1)
    first = (lane % HEAD_DIM) < half
    rot = jnp.where(first, pltpu.roll(t, w - half, 1), pltpu.roll(t, half, 1))
    reps = w // LANES
    c = jnp.concatenate([cos] * reps, axis=1) if reps > 1 else cos
    s = jnp.concatenate([sin] * reps, axis=1) if reps > 1 else sin
    return t * c + rot * s


def _dup_heads(t):
    lane = lax.broadcasted_iota(jnp.int32, t.shape, 1)
    low = lane < HEAD_DIM
    r = pltpu.roll(t, HEAD_DIM, 1)
    return jnp.concatenate([jnp.where(low, t, r), jnp.where(low, r, t)], axis=1)


def _inproj_kernel(x_ref, mod_ref, gain_ref, wqkv_ref, wut_ref, bd_ref, qg_ref, kg_ref, cos_ref, sin_ref,
                   q_ref, kk_ref, vv_ref, ut_ref, *flat):
    x = x_ref[0]
    h = _norm_mod(x, gain_ref[1:2, :], mod_ref[0, 3:4, :], mod_ref[0, 4:5, :]).astype(BF16)
    qkv = _dot_w(h, wqkv_ref[...])
    q = qkv[:, :Q_WIDTH]
    k = qkv[:, Q_WIDTH:Q_WIDTH + KV_WIDTH]
    v = qkv[:, Q_WIDTH + KV_WIDTH:]
    cos = cos_ref[...]
    sin = sin_ref[...]
    q = _rope(_head_norm(q, bd_ref[...], qg_ref[...]), cos, sin)
    k = _rope(_head_norm(k, bd_ref[:KV_WIDTH, :KV_WIDTH], kg_ref[...]), cos, sin)
    q_ref[0] = (q * (HEAD_DIM ** -0.5)).astype(BF16)
    kk_ref[0] = _dup_heads(k).astype(BF16)
    vv_ref[0] = _dup_heads(v).astype(BF16)
    ut = _dot_nt(wut_ref[...], h)
    if not flat:
        ut_ref[...] = ut
        return
    flat_ref, = flat
    nr = ut_ref.shape[1]
    for j in range(nr):
        flat_ref[pl.ds(j, SSM_WIDTH, stride=nr), :] = ut[:, j * SSM_ROW:(j + 1) * SSM_ROW]

    def unflatten(ch, carry):
        ut_ref[ch] = flat_ref[pl.ds(pl.multiple_of(ch * nr, nr), nr), :]
        return carry

    lax.fori_loop(0, SSM_WIDTH, unflatten, 0, unroll=8)


def _in_proj(s, mod, gains, wqkv, wut, bd, qg, kg, cos, sin, mod_index=None, tm=1024):
    bs, ls, d = s.shape
    tm = min(tm, ls)
    nt = ls // tm
    nr = tm // SSM_ROW
    by_rows = nr % 8 == 0
    if by_rows:
        ut_spec = pl.BlockSpec((SSM_WIDTH, nr, SSM_ROW), lambda b, i: (0, b * nt + i, 0))
        ut_shape = jax.ShapeDtypeStruct((SSM_WIDTH, bs * ls // SSM_ROW, SSM_ROW), F32)
        scratch = [pltpu.VMEM((SSM_WIDTH * nr, SSM_ROW), F32)]
    else:
        ut_spec = pl.BlockSpec((SSM_WIDTH, tm), lambda b, i: (0, b * nt + i))
        ut_shape = jax.ShapeDtypeStruct((SSM_WIDTH, bs * ls), F32)
        scratch = []
    if mod_index is None:
        mod_map = lambda b, i: (b, 0, 0)
    else:
        mod_map = lambda b, i: (mod_index, 0, 0)
    tok = lambda w: pl.BlockSpec((1, tm, w), lambda b, i: (b, i, 0))
    return pl.pallas_call(
        _inproj_kernel,
        grid=(bs, nt),
        in_specs=[
            tok(d),
            pl.BlockSpec((1, N_MOD, d), mod_map),
            _resident(gains.shape),
            _resident(wqkv.shape),
            _resident(wut.shape),
            _resident(bd.shape),
            _resident(qg.shape),
            _resident(kg.shape),
            pl.BlockSpec((tm, LANES), lambda b, i: (i, 0)),
            pl.BlockSpec((tm, LANES), lambda b, i: (i, 0)),
        ],
        out_specs=[tok(Q_WIDTH), tok(2 * KV_WIDTH), tok(2 * KV_WIDTH), ut_spec],
        out_shape=[
            jax.ShapeDtypeStruct((bs, ls, Q_WIDTH), BF16),
            jax.ShapeDtypeStruct((bs, ls, 2 * KV_WIDTH), BF16),
            jax.ShapeDtypeStruct((bs, ls, 2 * KV_WIDTH), BF16),
            ut_shape,
        ],
        scratch_shapes=scratch,
        compiler_params=_params("parallel", "parallel"),
        name="in_proj",
    )(s, mod, gains, wqkv, wut, bd, qg, kg, cos, sin)


def _rope_tables(n_lat):
    rows = n_lat // GRID_W
    row = jnp.repeat(jnp.arange(rows, dtype=F32), GRID_W)
    col = jnp.tile(jnp.arange(GRID_W, dtype=F32), rows)
    n_freq = HEAD_DIM // 4
    inv_freq = 1.0 / (ROPE_BASE ** (jnp.arange(n_freq, dtype=F32) / n_freq))
    ang = jnp.concatenate([row[:, None] * inv_freq, col[:, None] * inv_freq], axis=-1)
    cos, sin = jnp.cos(ang), jnp.sin(ang)
    cos_t = jnp.tile(cos, (1, LANES // (HEAD_DIM // 2)))
    sin_t = jnp.tile(jnp.concatenate([-sin, sin], axis=-1), (1, LANES // HEAD_DIM))
    return cos_t, sin_t


def _attn_kernel(sink_ref, q_ref, kk_ref, vv_ref, kkc_ref, vvc_ref, o_ref):
    i = pl.program_id(1)
    nb = pl.num_programs(1)
    blk = ATT_BLOCK
    q = q_ref[0]
    lane = lax.broadcasted_iota(jnp.int32, (blk, LANES), 1)
    low = lane < HEAD_DIM
    rel = lane - lax.broadcasted_iota(jnp.int32, (blk, LANES), 0)
    ok_prev = rel >= jnp.where(i > 0, 0, 2 * blk)
    ok_next = rel <= jnp.where(i < nb - 1, 0, -2 * blk)
    j_prev = jnp.maximum(i - 1, 0)
    j_next = jnp.minimum(i + 1, nb - 1)
    zero = jnp.zeros_like(q[:, :LANES])
    cols = []
    for h in range(N_KV_HEADS):
        hs = slice(h * LANES, (h + 1) * LANES)

        def rows(ref, j):
            return ref[0, pl.ds(pl.multiple_of(j * blk, blk), blk), hs]

        keys = jnp.concatenate([rows(kk_ref, j_prev), rows(kk_ref, i), rows(kk_ref, j_next), kkc_ref[0, :, hs]], axis=0)
        vals = jnp.concatenate([rows(vv_ref, j_prev), rows(vv_ref, i), rows(vv_ref, j_next), vvc_ref[0, :, hs]], axis=0)
        qs = []
        for col in (2 * h, 2 * h + 1):
            qc = q[:, col * LANES:(col + 1) * LANES]
            qs.append(jnp.where(low, qc, zero))
            qs.append(jnp.where(low, zero, qc))
        s_all = _dot_nt(jnp.concatenate(qs, axis=0), keys)
        ps = []
        inv = []
        for g in range(4):
            s = s_all[g * blk:(g + 1) * blk]
            s = jnp.concatenate([
                jnp.where(ok_prev, s[:, :blk], NEG_INF),
                s[:, blk:2 * blk],
                jnp.where(ok_next, s[:, 2 * blk:3 * blk], NEG_INF),
                s[:, 3 * blk:],
            ], axis=1)
            sink = sink_ref[4 * h + g]
            m = jnp.maximum(jnp.max(s, axis=-1, keepdims=True), sink)
            p = jnp.exp(s - m)
            den = jnp.sum(p, axis=-1, keepdims=True) + jnp.exp(sink - m)
            ps.append(p.astype(BF16))
            inv.append(1.0 / den)
        o = _dot(jnp.concatenate(ps, axis=0), vals)
        og = [o[g * blk:(g + 1) * blk] * inv[g] for g in range(4)]
        cols.append(jnp.where(low, og[0], og[1]))
        cols.append(jnp.where(low, og[2], og[3]))
    o_ref[0] = jnp.concatenate(cols, axis=1).astype(BF16)


def _attention(q, kk, vv, kkc, vvc, sink):
    bs, ls, _ = q.shape
    nb = ls // ATT_BLOCK
    n_ctx = kkc.shape[1]
    full = lambda n: pl.BlockSpec((1, n, 2 * KV_WIDTH), lambda b, i: (b, 0, 0))
    return pl.pallas_call(
        _attn_kernel,
        grid=(bs, nb),
        in_specs=[
            pl.BlockSpec(memory_space=pltpu.SMEM),
            pl.BlockSpec((1, ATT_BLOCK, Q_WIDTH), lambda b, i: (b, i, 0)),
            full(ls), full(ls), full(n_ctx), full(n_ctx),
        ],
        out_specs=pl.BlockSpec((1, ATT_BLOCK, Q_WIDTH), lambda b, i: (b, i, 0)),
        out_shape=jax.ShapeDtypeStruct((bs, ls, Q_WIDTH), BF16),
        compiler_params=_params("parallel", "arbitrary"),
        name="window_attention",
    )(sink, q, kk, vv, kkc, vvc)


def _cpow(lr, li, dt, tau):
    mag = jnp.exp(lr * dt * tau)
    ang = li * dt * tau
    return mag * jnp.cos(ang), mag * jnp.sin(ang)


def _ssm_prep_kernel(lrr_ref, lir_ref, lrc_ref, lic_ref, ldt_ref, btr_ref, bti_ref, btrr_ref, btri_ref,
                     crr_ref, cri_ref, ctr_ref, cti_ref, kcat_ref, wst_ref, wcar_ref, arow_ref):
    p = SSM_STATE
    t = SSM_CHUNK
    lane_t = lax.broadcasted_iota(jnp.int32, (1, t), 1).astype(F32)
    sub_t = lax.broadcasted_iota(jnp.int32, (t, 1), 0).astype(F32)
    kparts = []
    for d in range(2):
        dt = jnp.exp(ldt_ref[0, d])
        lr, li = lrr_ref[0, d], lir_ref[0, d]
        lrc, lic = lrc_ref[0, d], lic_ref[0, d]
        ar, ai = _cpow(lr, li, dt, 1.0)
        nr = ar - 1.0
        den = lr * lr + li * li
        fr = (nr * lr + ai * li) / den
        fi = (ai * lr - nr * li) / den
        bt_r, bt_i = btr_ref[0, d], bti_ref[0, d]
        bbr = fr * bt_r - fi * bt_i
        bbi = fr * bt_i + fi * bt_r
        rep_r, rep_i = btrr_ref[0, d], btri_ref[0, d]
        bbr_rep = fr * rep_r - fi * rep_i
        bbi_rep = fr * rep_i + fi * rep_r
        c_r, c_i = crr_ref[0, d], cri_ref[0, d]
        m1 = c_r * bbr_rep - c_i * bbi_rep
        m2 = c_r * bbi_rep + c_i * bbr_rep
        tau = lane_t if d == 0 else (float(t) - lane_t)
        e_r, e_i = _cpow(lrc, lic, dt, tau)
        kmat = _dot3(m1, e_r) - _dot3(m2, e_i)
        if d == 0:
            kparts.append(kmat)
        else:
            lag0 = jnp.sum(m1, axis=1, keepdims=True)
            first = lax.broadcasted_iota(jnp.int32, (1, t), 1) == 0
            kparts[0] = kparts[0] + jnp.where(first, lag0, 0.0)
            kparts.insert(0, kmat)
        tau_s = (float(t - 1) - sub_t) if d == 0 else sub_t
        et_r, et_i = _cpow(lr, li, dt, tau_s)
        for i in range(SSM_GROUP):
            re = et_r * bbr[i:i + 1] - et_i * bbi[i:i + 1]
            im = et_r * bbi[i:i + 1] + et_i * bbr[i:i + 1]
            wst_ref[0, i * t:(i + 1) * t, d * 2 * p:(d + 1) * 2 * p] = jnp.concatenate([re, im], axis=1).astype(BF16)
        tau_c = (lane_t + 1.0) if d == 0 else (float(t) - lane_t)
        ec_r, ec_i = _cpow(lrc, lic, dt, tau_c)
        ct_r, ct_i = ctr_ref[0, d], cti_ref[0, d]
        car_re, car_im = [], []
        for jo in range(SSM_GROUP):
            cr = ct_r[:, jo:jo + 1]
            ci = ct_i[:, jo:jo + 1]
            car_re.append(cr * ec_r - ci * ec_i)
            car_im.append(-(cr * ec_i + ci * ec_r))
        wcar_ref[0, d * 2 * p:d * 2 * p + p, :] = jnp.concatenate(car_re, axis=1).astype(BF16)
        wcar_ref[0, d * 2 * p + p:(d + 1) * 2 * p, :] = jnp.concatenate(car_im, axis=1).astype(BF16)
        at_r, at_i = _cpow(lr, li, dt, float(t))
        arow_ref[0, d, 0:1, :] = jnp.concatenate([at_r, at_r], axis=1)
        arow_ref[0, d, 1:2, :] = jnp.concatenate([-at_i, at_i], axis=1)
    kcat_ref[0] = jnp.concatenate(kparts, axis=1)


def _ssm_prep(lam_re, lam_im, log_dt, b_re, b_im, c_re, c_im):
    g, p, n, t = SSM_GROUPS, SSM_STATE, SSM_GROUP, SSM_CHUNK
    gd = lambda a: jnp.swapaxes(a, 0, 1)
    lrr = gd(lam_re)[:, :, None, :]
    lir = gd(lam_im)[:, :, None, :]
    lrc = gd(lam_re)[:, :, :, None]
    lic = gd(lam_im)[:, :, :, None]
    ldt = gd(log_dt)[:, :, None, None]
    bt_r = jnp.swapaxes(gd(b_re), 2, 3)
    bt_i = jnp.swapaxes(gd(b_im), 2, 3)
    btr_r = jnp.tile(bt_r, (1, 1, n, 1))
    btr_i = jnp.tile(bt_i, (1, 1, n, 1))
    cr_r = jnp.repeat(gd(c_re), n, axis=2)
    cr_i = jnp.repeat(gd(c_im), n, axis=2)
    ct_r = jnp.swapaxes(gd(c_re), 2, 3)
    ct_i = jnp.swapaxes(gd(c_im), 2, 3)
    args = (lrr, lir, lrc, lic, ldt, bt_r, bt_i, btr_r, btr_i, cr_r, cr_i, ct_r, ct_i)
    spec = lambda a: pl.BlockSpec((1,) + a.shape[1:], lambda i: (i,) + (0,) * (a.ndim - 1))
    return pl.pallas_call(
        _ssm_prep_kernel,
        grid=(g,),
        in_specs=[spec(a) for a in args],
        out_specs=[
            pl.BlockSpec((1, n * n, 2 * t), lambda i: (i, 0, 0)),
            pl.BlockSpec((1, n * t, 4 * p), lambda i: (i, 0, 0)),
            pl.BlockSpec((1, 4 * p, n * t), lambda i: (i, 0, 0)),
            pl.BlockSpec((1, 2, 2, 2 * p), lambda i: (i, 0, 0, 0)),
        ],
        out_shape=[
            jax.ShapeDtypeStruct((g, n * n, 2 * t), F32),
            jax.ShapeDtypeStruct((g, n * t, 4 * p), BF16),
            jax.ShapeDtypeStruct((g, 4 * p, n * t), BF16),
            jax.ShapeDtypeStruct((g, 2, 2, 2 * p), F32),
        ],
        compiler_params=_params("parallel"),
        name="ssm_prep",
    )(*args)


def _halves_to_rows(a, b, low):
    return jnp.where(low, a, pltpu.roll(b, SSM_CHUNK, 1)), jnp.where(low, pltpu.roll(a, SSM_CHUNK, 1), b)


def _ssm_kernel(u_ref, uc_ref, kcat_ref, wst_ref, wcar_ref, arow_ref, d_ref, y_ref,
                tz_ref, s_ref, sc_ref, h_ref, *, nb, nc, ncc):
    n, t, p = SSM_GROUP, SSM_CHUNK, SSM_STATE
    rows, rows_c = u_ref.shape[1], uc_ref.shape[1]
    low = lax.broadcasted_iota(jnp.int32, (1, SSM_ROW), 1) < t

    def build(i, carry):
        for m in range(n // 2):
            even = jnp.broadcast_to(kcat_ref[0, pl.ds(2 * m * n + i, 1), :], (t, SSM_ROW))
            odd = jnp.broadcast_to(kcat_ref[0, pl.ds((2 * m + 1) * n + i, 1), :], (t, SSM_ROW))
            even = pltpu.roll(even, t, 1, stride=1, stride_axis=0)
            odd = pltpu.roll(odd, 0, 1, stride=1, stride_axis=0)
            tz_ref[pl.ds(pl.multiple_of(i * t, t), t), m * SSM_ROW:(m + 1) * SSM_ROW] = jnp.where(low, even, odd).astype(BF16)
        return carry

    lax.fori_loop(0, n, build, 0)

    def chunk_rows(ref):
        first, second = [], []
        for m in range(n // 2):
            a, b = _halves_to_rows(ref[2 * m], ref[2 * m + 1], low)
            first.append(a.astype(BF16))
            second.append(b.astype(BF16))
        return jnp.concatenate([jnp.concatenate(first, axis=1), jnp.concatenate(second, axis=1)], axis=0)

    x = chunk_rows(u_ref)
    xc = chunk_rows(uc_ref)
    s_all = _dot(x, wst_ref[0])
    sc_all = _dot(xc, wst_ref[0])
    for d in range(2):
        s_ref[d, 0] = s_all[:, d * 2 * p:(d + 1) * 2 * p]
        s_ref[d, 1] = pltpu.roll(s_all[:, d * 2 * p:(d + 1) * 2 * p], p, 1)
        sc_ref[d, 0] = sc_all[:, d * 2 * p:(d + 1) * 2 * p]
        sc_ref[d, 1] = pltpu.roll(sc_all[:, d * 2 * p:(d + 1) * 2 * p], p, 1)

    coef = [(arow_ref[0, d, 0:1, :], arow_ref[0, d, 1:2, :]) for d in range(2)]

    def advance(d, h, hs, s, ss):
        a1, a2 = coef[d]
        return a1 * h + a2 * hs + s, a1 * hs - a2 * h + ss

    state = []
    for d in range(2):
        h = hs = jnp.zeros((nb, 2 * p), F32)
        order = range(2 * ncc) if d == 0 else reversed(range(2 * ncc))
        for chunk in order:
            sel = pl.ds((chunk % 2) * rows_c + chunk // 2, nb, stride=ncc)
            h, hs = advance(d, h, hs, sc_ref[d, 0, sel, :], sc_ref[d, 1, sel, :])
        state += [h, hs]

    def scan(k, carry):
        out = []
        for d in range(2):
            h, hs = carry[2 * d], carry[2 * d + 1]
            row = k if d == 0 else nc - 1 - k
            for half in ((0, 1) if d == 0 else (1, 0)):
                sel = pl.ds(half * rows + row, nb, stride=nc)
                h_ref[d, sel, :] = h
                h, hs = advance(d, h, hs, s_ref[d, 0, sel, :], s_ref[d, 1, sel, :])
            out += [h, hs]
        return tuple(out)

    lax.fori_loop(0, nc, scan, tuple(state), unroll=2)

    hb = jnp.concatenate([h_ref[0], h_ref[1]], axis=1).astype(BF16)
    nblk = 2 * SSM_ROW
    for q in range(n * t // nblk):
        y = _dot(x, tz_ref[:, q * nblk:(q + 1) * nblk]) + _dot(hb, wcar_ref[0, :, q * nblk:(q + 1) * nblk])
        for mm in range(2):
            m = 2 * q + mm
            even, odd = _halves_to_rows(y[:rows, mm * SSM_ROW:(mm + 1) * SSM_ROW],
                                        y[rows:, mm * SSM_ROW:(mm + 1) * SSM_ROW], low)
            y_ref[2 * m] = even + d_ref[2 * m] * u_ref[2 * m]
            y_ref[2 * m + 1] = odd + d_ref[2 * m + 1] * u_ref[2 * m + 1]


def _ssm(ut, utc, kcat, wst, wcar, arow, d_skip, nb):
    n, t, p, g = SSM_GROUP, SSM_CHUNK, SSM_STATE, SSM_GROUPS
    rows = ut.shape[1]
    rows_c = utc.shape[1]
    nc, ncc = rows // nb, rows_c // nb
    d3 = jnp.broadcast_to(d_skip.astype(F32)[:, None, None], (SSM_WIDTH, 1, SSM_ROW))
    return pl.pallas_call(
        functools.partial(_ssm_kernel, nb=nb, nc=nc, ncc=ncc),
        grid=(g,),
        in_specs=[
            pl.BlockSpec((n, rows, SSM_ROW), lambda i: (i, 0, 0)),
            pl.BlockSpec((n, rows_c, SSM_ROW), lambda i: (i, 0, 0)),
            pl.BlockSpec((1, n * n, 2 * t), lambda i: (i, 0, 0)),
            pl.BlockSpec((1, n * t, 4 * p), lambda i: (i, 0, 0)),
            pl.BlockSpec((1, 4 * p, n * t), lambda i: (i, 0, 0)),
            pl.BlockSpec((1, 2, 2, 2 * p), lambda i: (i, 0, 0, 0)),
            pl.BlockSpec((n, 1, SSM_ROW), lambda i: (i, 0, 0)),
        ],
        out_specs=pl.BlockSpec((n, rows, SSM_ROW), lambda i: (i, 0, 0)),
        out_shape=jax.ShapeDtypeStruct((SSM_WIDTH, rows, SSM_ROW), F32),
        scratch_shapes=[
            pltpu.VMEM((n * t, n * t), BF16),
            pltpu.VMEM((2, 2, 2 * rows, 2 * p), F32),
            pltpu.VMEM((2, 2, 2 * rows_c, 2 * p), F32),
            pltpu.VMEM((2, 2 * rows, 2 * p), F32),
        ],
        compiler_params=_params("parallel"),
        name="ssm_toeplitz",
    )(ut, utc, kcat, wst, wcar, arow, d3)


def _glu_kernel(yt_ref, wglut_ref, s_ref, flat_ref):
    t = SSM_ROW
    nck = yt_ref.shape[1]

    def flatten(ch, carry):
        flat_ref[pl.ds(pl.multiple_of(ch * nck, nck), nck), :] = yt_ref[ch]
        return carry

    lax.fori_loop(0, yt_ref.shape[0], flatten, 0)
    for j in range(nck):
        g = jax.nn.gelu(flat_ref[pl.ds(j, SSM_WIDTH, stride=nck), :])
        z = _dot(wglut_ref[...], g.astype(BF16))
        s_ref[j * t:(j + 1) * t, :] = (g * jax.nn.sigmoid(z)).T.astype(BF16)


def _ssm_glu(yt, wglut, bs, ls):
    nck = 8
    rows = yt.shape[1]
    s = pl.pallas_call(
        _glu_kernel,
        grid=(rows // nck,),
        in_specs=[
            pl.BlockSpec((SSM_WIDTH, nck, SSM_ROW), lambda i: (0, i, 0)),
            _resident(wglut.shape),
        ],
        out_specs=pl.BlockSpec((nck * SSM_ROW, SSM_WIDTH), lambda i: (i, 0)),
        out_shape=jax.ShapeDtypeStruct((rows * SSM_ROW, SSM_WIDTH), BF16),
        scratch_shapes=[pltpu.VMEM((SSM_WIDTH * nck, SSM_ROW), F32)],
        compiler_params=_params("parallel"),
        name="ssm_glu",
    )(yt, wglut)
    return s.reshape(bs, ls, SSM_WIDTH)


def _fourier_kernel(x_ref, r_ref, mod_ref, gain_ref, lhs1_ref, twc_ref, tws_ref, lhs2_ref, cs_ref, z_ref,
                    h_ref, are_ref, aim_ref, y_ref):
    r, pitch = FFT_RADIX, FFT_PITCH
    halves = FOURIER_GROUP_WIDTH // LANES

    def slab(i):
        return pl.ds(pl.multiple_of(i * pitch, 8), r)

    def gather(ref, i):
        return jnp.concatenate([ref[hf, pl.ds(i, r, stride=pitch), :] for hf in range(halves)], axis=1)

    def put(ref, i, val):
        for hf in range(halves):
            ref[hf, slab(i), :] = val[:, hf * LANES:(hf + 1) * LANES]

    gain = gain_ref[1:2, :]
    shift, scale = mod_ref[0, 3:4, :], mod_ref[0, 4:5, :]

    def fill(n1, carry):
        rows = pl.ds(pl.multiple_of(n1 * r, r), r)
        rstd = jnp.concatenate([r_ref[0, rows, :]] * halves, axis=1)
        put(h_ref, n1, (x_ref[0, rows, :] * rstd * gain) * (1.0 + scale) + shift)
        return carry

    lax.fori_loop(0, r, fill, 0, unroll=4)

    def stage1(jg, carry):
        cols = [gather(h_ref, jg * 4 + j).astype(BF16) for j in range(4)]
        a = _dot(lhs1_ref[...], jnp.concatenate(cols, axis=0))
        ch, sh = a[:4 * r], a[4 * r:]
        trow = pl.ds(pl.multiple_of(jg * 4 * r, 4 * r), 4 * r)
        tc = jnp.concatenate([twc_ref[trow, :]] * halves, axis=1)
        ts = jnp.concatenate([tws_ref[trow, :]] * halves, axis=1)
        a_re = ch * tc - sh * ts
        a_im = -(sh * tc) - ch * ts
        for j in range(4):
            put(are_ref, jg * 4 + j, a_re[j * r:(j + 1) * r])
            put(aim_ref, jg * 4 + j, a_im[j * r:(j + 1) * r])
        return carry

    lax.fori_loop(0, r // 4, stage1, 0, unroll=4)

    def stage2(kg, carry):
        parts = [gather(are_ref, kg * 4 + j).astype(BF16) for j in range(4)]
        parts += [gather(aim_ref, kg * 4 + j).astype(BF16) for j in range(4)]
        y = _dot(lhs2_ref[...], jnp.concatenate(parts, axis=0))
        rows = pl.ds(pl.multiple_of(kg * 4 * r, 4 * r), 4 * r)
        y_ref[rows, :] = jnp.concatenate([y[:4 * r], y[4 * r:]], axis=1).astype(BF16)
        return carry

    lax.fori_loop(0, r // 4, stage2, 0, unroll=4)

    def channels(kb, carry):
        rows = pl.ds(pl.multiple_of(kb * 8 * r, 8 * r), 8 * r)
        z = _dot(y_ref[rows, :], cs_ref[...])
        for j in range(8):
            put(h_ref, kb * 8 + j, z[j * r:(j + 1) * r])
        return carry

    lax.fori_loop(0, r // 8, channels, 0, unroll=2)

    def emit(k2, carry):
        z_ref[0, pl.ds(pl.multiple_of(k2 * r, r), r), :] = gather(h_ref, k2).astype(BF16)
        return carry

    lax.fori_loop(0, r, emit, 0, unroll=4)


def _fourier_tables():
    r = FFT_RADIX
    n = r * r
    k = jnp.arange(r, dtype=jnp.int32)
    ang = ((k[:, None] * k[None, :]) % r).astype(F32) * (2.0 * math.pi / r)
    wc = jnp.cos(ang) / 8.0
    ws = jnp.sin(ang) / 8.0
    eye4 = jnp.eye(4, dtype=F32)
    bc, bs = jnp.kron(eye4, wc), jnp.kron(eye4, ws)
    lhs1 = jnp.concatenate([bc, bs], axis=0).astype(BF16)
    lhs2 = jnp.concatenate([jnp.concatenate([bc, bs], axis=1),
                            jnp.concatenate([-bs, bc], axis=1)], axis=0).astype(BF16)
    tw = ((k[:, None] * k[None, :]) % n).astype(F32) * (2.0 * math.pi / n)
    twc = jnp.broadcast_to(jnp.cos(tw).reshape(n, 1), (n, LANES))
    tws = jnp.broadcast_to(jnp.sin(tw).reshape(n, 1), (n, LANES))
    m = jnp.arange(FOURIER_GROUP_WIDTH, dtype=jnp.int32)
    angc = ((m[:, None] * m[None, :]) % FOURIER_GROUP_WIDTH).astype(F32) * (2.0 * math.pi / FOURIER_GROUP_WIDTH)
    cs = jnp.concatenate([jnp.cos(angc), jnp.sin(angc)], axis=0) / 16.0
    return lhs1, lhs2, twc, tws, cs.astype(BF16)


def _fourier(x, rstd, mod, gains):
    bs, ls, d = x.shape
    r, gw = FFT_RADIX, FOURIER_GROUP_WIDTH
    assert ls == r * r
    lhs1, lhs2, twc, tws, cs = _fourier_tables()
    scratch = pltpu.VMEM((gw // LANES, r * FFT_PITCH, LANES), F32)
    return pl.pallas_call(
        _fourier_kernel,
        grid=(bs, d // gw),
        in_specs=[
            pl.BlockSpec((1, ls, gw), lambda b, g: (b, 0, g)),
            pl.BlockSpec((1, ls, LANES), lambda b, g: (b, 0, 0)),
            pl.BlockSpec((1, N_MOD, gw), lambda b, g: (b, 0, g)),
            pl.BlockSpec((gains.shape[0], gw), lambda b, g: (0, g)),
            _resident(lhs1.shape), _resident(twc.shape), _resident(tws.shape),
            _resident(lhs2.shape), _resident(cs.shape),
        ],
        out_specs=pl.BlockSpec((1, ls, gw), lambda b, g: (b, 0, g)),
        out_shape=jax.ShapeDtypeStruct((bs, ls, d), BF16),
        scratch_shapes=[scratch, scratch, scratch, pltpu.VMEM((ls, 2 * gw), BF16)],
        compiler_params=_params("parallel", "parallel"),
        name="fourier_mix",
    )(x, rstd, mod, gains, lhs1, twc, tws, lhs2, cs)


def _even_layer(x, s_ctx, mod, gains, ffn_w, layer, w_in, q_gain, k_gain, sink, ssm, d_skip, w_glu, w_out, e):
    bs, ls, d = x.shape
    ctx_index = bs
    x = _half_ffn(x, mod, 0, gains, 0, ffn_w, (layer, 0))
    s_ctx = _half_ffn(s_ctx, mod, 0, gains, 0, ffn_w, (layer, 0), mod_index=ctx_index)

    wqkv = w_in[:, :Q_WIDTH + 2 * KV_WIDTH]
    wut = w_in[:, Q_WIDTH + 2 * KV_WIDTH:].T.astype(BF16)
    bd = jnp.kron(jnp.eye(N_Q_HEADS, dtype=F32), jnp.ones((HEAD_DIM, HEAD_DIM), F32)).astype(BF16)
    qg = jnp.tile(q_gain.astype(F32), N_Q_HEADS)[None, :]
    kg = jnp.tile(k_gain.astype(F32), N_KV_HEADS)[None, :]
    cos, sin = _rope_tables(ls)
    n_ctx = s_ctx.shape[1]
    one = jnp.ones((n_ctx, LANES), F32)
    q, kk, vv, ut = _in_proj(x, mod, gains, wqkv, wut, bd, qg, kg, cos, sin)
    _, kkc, vvc, utc = _in_proj(s_ctx, mod, gains, wqkv, wut, bd, qg, kg, one, 0.0 * one, mod_index=ctx_index)

    attn = _attention(q, kk, vv, kkc, vvc, sink.astype(F32))

    t = SSM_ROW
    kcat, wst, wcar, arow = _ssm_prep(*ssm)
    yt = _ssm(ut, utc.reshape(SSM_WIDTH, bs * n_ctx // t, t), kcat, wst, wcar, arow, d_skip, bs)

    s = _ssm_glu(yt, w_glu.T.astype(BF16), bs, ls)
    mix = ((attn, w_out, _picked(w_out, (e,), (Q_WIDTH, d), (0, 0))),
           (s, w_out, _picked(w_out, (e,), (SSM_WIDTH, d), (1, 0))))
    return _half_ffn(x, mod, 6, gains, 2, ffn_w, (layer, 1), mix=mix)


def _odd_layer(x, mod, gains, ffn_w, layer, w_f, o):
    x, rstd = _half_ffn(x, mod, 0, gains, 0, ffn_w, (layer, 0), emit_rstd=True)
    z = _fourier(x, rstd, mod, gains)
    mix = ((z, w_f, _picked(w_f, (o,), w_f.shape[1:])),)
    return _half_ffn(x, mod, 6, gains, 2, ffn_w, (layer, 1), mix=mix)


def kernel(x, c, ctx, c_ctx, w_ada, b_ada, norm_gain, ffn_w1, ffn_w3, ffn_w2, w_in, q_gain, k_gain, sink_logit, ssm_lam_re, ssm_lam_im, ssm_log_dt, ssm_b_re, ssm_b_im, ssm_c_re, ssm_c_im, ssm_d, ssm_w_glu, w_out, fourier_w_out):
    bs, ls, d = x.shape
    depth = w_ada.shape[0]
    assert depth == 2 and d == D_MODEL
    rows = 16
    cond = jnp.concatenate([c, c_ctx[None, :], jnp.zeros((rows - bs - 1, d), F32)], axis=0)
    mods = _ada_mod(cond, w_ada, b_ada)[:, :bs + 1].reshape(depth, bs + 1, N_MOD, d)

    ffn_w = (ffn_w1, ffn_w3, ffn_w2)
    ssm = (ssm_lam_re[0], ssm_lam_im[0], ssm_log_dt[0], ssm_b_re[0], ssm_b_im[0], ssm_c_re[0], ssm_c_im[0])
    x = _even_layer(x, ctx, mods[0], norm_gain[0], ffn_w, 0, w_in[0], q_gain[0], k_gain[0],
                    sink_logit[0], ssm, ssm_d[0], ssm_w_glu[0], w_out, 0)
    return _odd_layer(x, mods[1], norm_gain[1], ffn_w, 1, fourier_w_out, 0)
```

```python
import functools
import math

import jax
import jax.numpy as jnp
from jax import lax
from jax.experimental import pallas as pl
from jax.experimental.pallas import tpu as pltpu

F32 = jnp.float32
BF16 = jnp.bfloat16

D_MODEL = 1024
N_MOD = 9
EPS = 1e-6
HEAD_DIM = 64
N_Q_HEADS = 8
N_KV_HEADS = 2
Q_WIDTH = N_Q_HEADS * HEAD_DIM
KV_WIDTH = N_KV_HEADS * HEAD_DIM
ATT_BLOCK = 128
ROPE_BASE = 10000.0
NEG_INF = -1e30
GRID_W = 64
SSM_WIDTH = 512
SSM_GROUP = 16
SSM_GROUPS = 32
SSM_STATE = 64
SSM_ROW = 128
SSM_CHUNK = 64
FOURIER_GROUPS = 4
FOURIER_GROUP_WIDTH = 256
FFT_RADIX = 64
FFT_PITCH = 72
FFN_CHUNK = 256
LANES = 128
VMEM_LIMIT = 56 * 1024 * 1024


def _dot(a, b):
    return jnp.dot(a, b, preferred_element_type=F32)


def _dot_w(a, w):
    return lax.dot_general(a, w, (((1,), (0,)), ((), ())), preferred_element_type=F32)


def _dot_nt(a, b):
    return lax.dot_general(a, b, (((1,), (1,)), ((), ())), preferred_element_type=F32)


def _split(a):
    hi = a.astype(BF16)
    lo = (a - hi.astype(F32)).astype(BF16)
    return hi, lo


def _dot3(a, b):
    a_hi, a_lo = _split(a)
    b_hi, b_lo = _split(b)
    return _dot(a_hi, b_hi) + _dot(a_lo, b_hi) + _dot(a_hi, b_lo)


def _norm_mod(x, gain, shift, scale):
    ms = jnp.mean(x * x, axis=-1, keepdims=True)
    y = x * lax.rsqrt(ms + EPS) * gain
    return y * (1.0 + scale) + shift


def _params(*sem):
    return pltpu.CompilerParams(dimension_semantics=sem, vmem_limit_bytes=VMEM_LIMIT)


def _resident(shape):
    nd = len(shape)
    return pl.BlockSpec(shape, lambda *_: (0,) * nd, pipeline_mode=pl.Buffered(1))


def _ada_kernel(cond_ref, w_ref, b_ref, o_ref):
    a = cond_ref[...]
    a = a * jax.nn.sigmoid(a)
    o_ref[0] = _dot3(a, w_ref[0]) + b_ref[0]


def _ada_mod(cond, w_ada, b_ada):
    depth, d, nd = w_ada.shape
    rows = cond.shape[0]
    tn = nd // 8
    return pl.pallas_call(
        _ada_kernel,
        grid=(depth, nd // tn),
        in_specs=[
            pl.BlockSpec((rows, d), lambda l, j: (0, 0)),
            pl.BlockSpec((1, d, tn), lambda l, j: (l, 0, j)),
            pl.BlockSpec((1, 1, tn), lambda l, j: (l, 0, j)),
        ],
        out_specs=pl.BlockSpec((1, rows, tn), lambda l, j: (l, 0, j)),
        out_shape=jax.ShapeDtypeStruct((depth, rows, nd), F32),
        compiler_params=_params("parallel", "parallel"),
        name="ada_mod",
    )(cond, w_ada, b_ada.reshape(depth, 1, nd))


def _ffn_kernel(*refs, row0, grow, n_mix, emit_rstd):
    x_ref, mod_ref, gain_ref, w1_ref, w3_ref, w2_ref = refs[:6]
    mix_refs = refs[6:6 + 2 * n_mix]
    out_refs = refs[6 + 2 * n_mix:-1]
    acc_ref = refs[-1]
    o_ref = out_refs[0]
    x = x_ref[0]
    if n_mix:
        y = _dot_w(mix_refs[0][0], mix_refs[1][...])
        for k in range(1, n_mix):
            y = y + _dot_w(mix_refs[2 * k][0], mix_refs[2 * k + 1][...])
        x = x + mod_ref[0, 5:6, :] * y
        o_ref[0] = x
    shift = mod_ref[0, row0:row0 + 1, :]
    scale = mod_ref[0, row0 + 1:row0 + 2, :]
    gate = mod_ref[0, row0 + 2:row0 + 3, :]
    h = _norm_mod(x, gain_ref[grow:grow + 1, :], shift, scale)
    fc = FFN_CHUNK
    for c in range(w1_ref.shape[1] // fc):
        a = _dot_w(h, w1_ref[:, c * fc:(c + 1) * fc])
        b = _dot_w(h, w3_ref[:, c * fc:(c + 1) * fc])
        g = a * jax.nn.sigmoid(a) * b
        part = _dot_w(g, w2_ref[c * fc:(c + 1) * fc, :])
        if c == 0:
            acc_ref[...] = part
        else:
            acc_ref[...] += part
    base = o_ref[0] if n_mix else x_ref[0]
    o = base + (0.5 * gate) * acc_ref[...]
    o_ref[0] = o
    if emit_rstd:
        rstd = lax.rsqrt(jnp.mean(o * o, axis=-1, keepdims=True) + EPS)
        out_refs[1][0] = jnp.broadcast_to(rstd, out_refs[1].shape[1:])


def _picked(arr, lead, shape, start=None):
    start = start or (0,) * len(shape)
    return pl.BlockSpec((None,) * len(lead) + tuple(shape), lambda *_: tuple(lead) + tuple(start),
                        pipeline_mode=pl.Buffered(1))


def _half_ffn(s, mod, row0, gains, grow, ffn_w, lk, mod_index=None, mix=(), emit_rstd=False, tm=512):
    bs, ls, d = s.shape
    tm = min(tm, ls)
    if mod_index is None:
        mod_map = lambda b, i: (b, 0, 0)
    else:
        mod_map = lambda b, i: (mod_index, 0, 0)
    tok = lambda w: pl.BlockSpec((1, tm, w), lambda b, i: (b, i, 0))
    in_specs = [tok(d), pl.BlockSpec((1, N_MOD, d), mod_map), _resident(gains.shape)]
    in_specs += [_picked(w, lk, w.shape[2:]) for w in ffn_w]
    args = [s, mod, gains, *ffn_w]
    for z, w, w_spec in mix:
        in_specs += [tok(z.shape[2]), w_spec]
        args += [z, w]
    out_specs = [tok(d)]
    out_shape = [jax.ShapeDtypeStruct(s.shape, F32)]
    if emit_rstd:
        out_specs.append(tok(LANES))
        out_shape.append(jax.ShapeDtypeStruct((bs, ls, LANES), F32))
    outs = pl.pallas_call(
        functools.partial(_ffn_kernel, row0=row0, grow=grow, n_mix=len(mix), emit_rstd=emit_rstd),
        grid=(bs, ls // tm),
        in_specs=in_specs,
        out_specs=out_specs,
        out_shape=out_shape,
        scratch_shapes=[pltpu.VMEM((tm, d), F32)],
        compiler_params=_params("parallel", "parallel"),
        name="half_ffn",
    )(*args)
    return outs if emit_rstd else outs[0]


def _head_norm(t, bd, gain):
    ms = _dot((t * t).astype(BF16), bd) * (1.0 / HEAD_DIM)
    return t * lax.rsqrt(ms + EPS) * gain


def _rope(t, cos, sin):
    w = t.shape[1]
    half = HEAD_DIM // 2
    lane = lax.broadcasted_iota(jnp.int32, t.shape, 1)
    first = (lane % HEAD_DIM) < half
    rot = jnp.where(first, pltpu.roll(t, w - half, 1), pltpu.roll(t, half, 1))
    reps = w // LANES
    c = jnp.concatenate([cos] * reps, axis=1) if reps > 1 else cos
    s = jnp.concatenate([sin] * reps, axis=1) if reps > 1 else sin
    return t * c + rot * s


def _dup_heads(t):
    lane = lax.broadcasted_iota(jnp.int32, t.shape, 1)
    low = lane < HEAD_DIM
    r = pltpu.roll(t, HEAD_DIM, 1)
    return jnp.concatenate([jnp.where(low, t, r), jnp.where(low, r, t)], axis=1)


def _inproj_kernel(x_ref, mod_ref, gain_ref, wqkv_ref, wut_ref, bd_ref, qg_ref, kg_ref, cos_ref, sin_ref,
                   q_ref, kk_ref, vv_ref, ut_ref, *flat):
    x = x_ref[0]
    h = _norm_mod(x, gain_ref[1:2, :], mod_ref[0, 3:4, :], mod_ref[0, 4:5, :]).astype(BF16)
    qkv = _dot_w(h, wqkv_ref[...])
    q = qkv[:, :Q_WIDTH]
    k = qkv[:, Q_WIDTH:Q_WIDTH + KV_WIDTH]
    v = qkv[:, Q_WIDTH + KV_WIDTH:]
    cos = cos_ref[...]
    sin = sin_ref[...]
    q = _rope(_head_norm(q, bd_ref[...], qg_ref[...]), cos, sin)
    k = _rope(_head_norm(k, bd_ref[:KV_WIDTH, :KV_WIDTH], kg_ref[...]), cos, sin)
    q_ref[0] = (q * (HEAD_DIM ** -0.5)).astype(BF16)
    kk_ref[0] = _dup_heads(k).astype(BF16)
    vv_ref[0] = _dup_heads(v).astype(BF16)
    ut = _dot_nt(wut_ref[...], h)
    if not flat:
        ut_ref[...] = ut
        return
    flat_ref, = flat
    nr = ut_ref.shape[1]
    for j in range(nr):
        flat_ref[pl.ds(j, SSM_WIDTH, stride=nr), :] = ut[:, j * SSM_ROW:(j + 1) * SSM_ROW]

    ut_ref[...] = flat_ref[...].reshape(ut_ref.shape)


def _in_proj(s, mod, gains, wqkv, wut, bd, qg, kg, cos, sin, mod_index=None, tm=1024):
    bs, ls, d = s.shape
    tm = min(tm, ls)
    nt = ls // tm
    nr = tm // SSM_ROW
    by_rows = nr % 8 == 0
    if by_rows:
        ut_spec = pl.BlockSpec((SSM_WIDTH, nr, SSM_ROW), lambda b, i: (0, b * nt + i, 0))
        ut_shape = jax.ShapeDtypeStruct((SSM_WIDTH, bs * ls // SSM_ROW, SSM_ROW), F32)
        scratch = [pltpu.VMEM((SSM_WIDTH * nr, SSM_ROW), F32)]
    else:
        ut_spec = pl.BlockSpec((SSM_WIDTH, tm), lambda b, i: (0, b * nt + i))
        ut_shape = jax.ShapeDtypeStruct((SSM_WIDTH, bs * ls), F32)
        scratch = []
    if mod_index is None:
        mod_map = lambda b, i: (b, 0, 0)
    else:
        mod_map = lambda b, i: (mod_index, 0, 0)
    tok = lambda w: pl.BlockSpec((1, tm, w), lambda b, i: (b, i, 0))
    return pl.pallas_call(
        _inproj_kernel,
        grid=(bs, nt),
        in_specs=[
            tok(d),
            pl.BlockSpec((1, N_MOD, d), mod_map),
            _resident(gains.shape),
            _resident(wqkv.shape),
            _resident(wut.shape),
            _resident(bd.shape),
            _resident(qg.shape),
            _resident(kg.shape),
            pl.BlockSpec((tm, LANES), lambda b, i: (i, 0)),
            pl.BlockSpec((tm, LANES), lambda b, i: (i, 0)),
        ],
        out_specs=[tok(Q_WIDTH), tok(2 * KV_WIDTH), tok(2 * KV_WIDTH), ut_spec],
        out_shape=[
            jax.ShapeDtypeStruct((bs, ls, Q_WIDTH), BF16),
            jax.ShapeDtypeStruct((bs, ls, 2 * KV_WIDTH), BF16),
            jax.ShapeDtypeStruct((bs, ls, 2 * KV_WIDTH), BF16),
            ut_shape,
        ],
        scratch_shapes=scratch,
        compiler_params=_params("parallel", "parallel"),
        name="in_proj",
    )(s, mod, gains, wqkv, wut, bd, qg, kg, cos, sin)


def _rope_tables(n_lat):
    rows = n_lat // GRID_W
    row = jnp.repeat(jnp.arange(rows, dtype=F32), GRID_W)
    col = jnp.tile(jnp.arange(GRID_W, dtype=F32), rows)
    n_freq = HEAD_DIM // 4
    inv_freq = 1.0 / (ROPE_BASE ** (jnp.arange(n_freq, dtype=F32) / n_freq))
    ang = jnp.concatenate([row[:, None] * inv_freq, col[:, None] * inv_freq], axis=-1)
    cos, sin = jnp.cos(ang), jnp.sin(ang)
    cos_t = jnp.tile(cos, (1, LANES // (HEAD_DIM // 2)))
    sin_t = jnp.tile(jnp.concatenate([-sin, sin], axis=-1), (1, LANES // HEAD_DIM))
    return cos_t, sin_t


def _attn_kernel(sink_ref, q_ref, kk_ref, vv_ref, kkc_ref, vvc_ref, o_ref):
    i = pl.program_id(1)
    nb = pl.num_programs(1)
    blk = ATT_BLOCK
    q = q_ref[0]
    lane = lax.broadcasted_iota(jnp.int32, (blk, LANES), 1)
    low = lane < HEAD_DIM
    rel = lane - lax.broadcasted_iota(jnp.int32, (blk, LANES), 0)
    ok_prev = rel >= jnp.where(i > 0, 0, 2 * blk)
    ok_next = rel <= jnp.where(i < nb - 1, 0, -2 * blk)
    j_prev = jnp.maximum(i - 1, 0)
    j_next = jnp.minimum(i + 1, nb - 1)
    zero = jnp.zeros_like(q[:, :LANES])
    cols = []
    for h in range(N_KV_HEADS):
        hs = slice(h * LANES, (h + 1) * LANES)

        def rows(ref, j):
            return ref[0, pl.ds(pl.multiple_of(j * blk, blk), blk), hs]

        keys = jnp.concatenate([rows(kk_ref, j_prev), rows(kk_ref, i), rows(kk_ref, j_next), kkc_ref[0, :, hs]], axis=0)
        vals = jnp.concatenate([rows(vv_ref, j_prev), rows(vv_ref, i), rows(vv_ref, j_next), vvc_ref[0, :, hs]], axis=0)
        qs = []
        for col in (2 * h, 2 * h + 1):
            qc = q[:, col * LANES:(col + 1) * LANES]
            qs.append(jnp.where(low, qc, zero))
            qs.append(jnp.where(low, zero, qc))
        s_all = _dot_nt(jnp.concatenate(qs, axis=0), keys)
        low_k = lax.broadcasted_iota(jnp.int32, vals.shape, 1) < HEAD_DIM
        v_ones = (jnp.where(low_k, vals, jnp.ones_like(vals)), jnp.where(low_k, jnp.ones_like(vals), vals))
        ps, tail = [], []
        for g in range(4):
            s = s_all[g * blk:(g + 1) * blk]
            parts = [jnp.where(ok_prev, s[:, :blk], NEG_INF), s[:, blk:2 * blk],
                     jnp.where(ok_next, s[:, 2 * blk:3 * blk], NEG_INF)]
            parts += [s[:, j * blk:(j + 1) * blk] for j in range(3, s.shape[1] // blk)]
            peak = parts[0]
            for part in parts[1:]:
                peak = jnp.maximum(peak, part)
            sink = sink_ref[4 * h + g]
            m = jnp.maximum(jnp.max(peak, axis=-1, keepdims=True), sink)
            ps.append(jnp.exp(jnp.concatenate([part - m for part in parts], axis=1).astype(BF16)))
            tail.append(jnp.exp(sink - m))
        og = []
        for par in range(2):
            o = _dot(jnp.concatenate([ps[par], ps[2 + par]], axis=0), v_ones[par])
            for k in range(2):
                ok = o[k * blk:(k + 1) * blk]
                den = (ok[:, HEAD_DIM:HEAD_DIM + 1] if par == 0 else ok[:, 0:1]) + tail[2 * k + par]
                og.append(ok * (1.0 / den))
        cols.append(jnp.where(low, og[0], og[2]))
        cols.append(jnp.where(low, og[1], og[3]))
    o_ref[0] = jnp.concatenate(cols, axis=1).astype(BF16)


def _attention(q, kk, vv, kkc, vvc, sink):
    bs, ls, _ = q.shape
    nb = ls // ATT_BLOCK
    n_ctx = kkc.shape[1]
    full = lambda n: pl.BlockSpec((1, n, 2 * KV_WIDTH), lambda b, i: (b, 0, 0))
    return pl.pallas_call(
        _attn_kernel,
        grid=(bs, nb),
        in_specs=[
            pl.BlockSpec(memory_space=pltpu.SMEM),
            pl.BlockSpec((1, ATT_BLOCK, Q_WIDTH), lambda b, i: (b, i, 0)),
            full(ls), full(ls), full(n_ctx), full(n_ctx),
        ],
        out_specs=pl.BlockSpec((1, ATT_BLOCK, Q_WIDTH), lambda b, i: (b, i, 0)),
        out_shape=jax.ShapeDtypeStruct((bs, ls, Q_WIDTH), BF16),
        compiler_params=_params("parallel", "arbitrary"),
        name="window_attention",
    )(sink, q, kk, vv, kkc, vvc)


def _cpow(lr, li, dt, tau):
    mag = jnp.exp(lr * dt * tau)
    ang = li * dt * tau
    return mag * jnp.cos(ang), mag * jnp.sin(ang)


def _ssm_prep_kernel(lrr_ref, lir_ref, lrc_ref, lic_ref, ldt_ref, btr_ref, bti_ref, btrr_ref, btri_ref,
                     crr_ref, cri_ref, ctr_ref, cti_ref, kcat_ref, wst_ref, wcar_ref, arow_ref):
    p = SSM_STATE
    t = SSM_CHUNK
    lane_t = lax.broadcasted_iota(jnp.int32, (1, t), 1).astype(F32)
    sub_t = lax.broadcasted_iota(jnp.int32, (t, 1), 0).astype(F32)
    kparts = []
    for d in range(2):
        dt = jnp.exp(ldt_ref[0, d])
        lr, li = lrr_ref[0, d], lir_ref[0, d]
        lrc, lic = lrc_ref[0, d], lic_ref[0, d]
        ar, ai = _cpow(lr, li, dt, 1.0)
        nr = ar - 1.0
        den = lr * lr + li * li
        fr = (nr * lr + ai * li) / den
        fi = (ai * lr - nr * li) / den
        bt_r, bt_i = btr_ref[0, d], bti_ref[0, d]
        bbr = fr * bt_r - fi * bt_i
        bbi = fr * bt_i + fi * bt_r
        rep_r, rep_i = btrr_ref[0, d], btri_ref[0, d]
        bbr_rep = fr * rep_r - fi * rep_i
        bbi_rep = fr * rep_i + fi * rep_r
        c_r, c_i = crr_ref[0, d], cri_ref[0, d]
        m1 = c_r * bbr_rep - c_i * bbi_rep
        m2 = c_r * bbi_rep + c_i * bbr_rep
        tau = lane_t if d == 0 else (float(t) - lane_t)
        e_r, e_i = _cpow(lrc, lic, dt, tau)
        kmat = _dot3(m1, e_r) - _dot3(m2, e_i)
        if d == 0:
            kparts.append(kmat)
        else:
            lag0 = jnp.sum(m1, axis=1, keepdims=True)
            first = lax.broadcasted_iota(jnp.int32, (1, t), 1) == 0
            kparts[0] = kparts[0] + jnp.where(first, lag0, 0.0)
            kparts.insert(0, kmat)
        tau_s = (float(t - 1) - sub_t) if d == 0 else sub_t
        et_r, et_i = _cpow(lr, li, dt, tau_s)
        for i in range(SSM_GROUP):
            re = et_r * bbr[i:i + 1] - et_i * bbi[i:i + 1]
            im = et_r * bbi[i:i + 1] + et_i * bbr[i:i + 1]
            wst_ref[0, i * t:(i + 1) * t, d * 2 * p:(d + 1) * 2 * p] = jnp.concatenate([re, im], axis=1).astype(BF16)
        if d == 0:
            ec_r, ec_i = _cpow(lrc, lic, dt, lane_t + 1.0)
        else:
            ec_r, ec_i = e_r, e_i
        ct_r, ct_i = ctr_ref[0, d], cti_ref[0, d]
        car_re, car_im = [], []
        for jo in range(SSM_GROUP):
            cr = ct_r[:, jo:jo + 1]
            ci = ct_i[:, jo:jo + 1]
            car_re.append(cr * ec_r - ci * ec_i)
            car_im.append(-(cr * ec_i + ci * ec_r))
        wcar_ref[0, d * 2 * p:d * 2 * p + p, :] = jnp.concatenate(car_re, axis=1).astype(BF16)
        wcar_ref[0, d * 2 * p + p:(d + 1) * 2 * p, :] = jnp.concatenate(car_im, axis=1).astype(BF16)
        at_r, at_i = _cpow(lr, li, dt, float(t))
        arow_ref[0, d, 0:1, :] = jnp.concatenate([at_r, at_r], axis=1)
        arow_ref[0, d, 1:2, :] = jnp.concatenate([-at_i, at_i], axis=1)
    kcat_ref[0] = jnp.concatenate(kparts, axis=1)


def _ssm_prep(lam_re, lam_im, log_dt, b_re, b_im, c_re, c_im):
    g, p, n, t = SSM_GROUPS, SSM_STATE, SSM_GROUP, SSM_CHUNK
    gd = lambda a: jnp.swapaxes(a, 0, 1)
    lrr = gd(lam_re)[:, :, None, :]
    lir = gd(lam_im)[:, :, None, :]
    lrc = gd(lam_re)[:, :, :, None]
    lic = gd(lam_im)[:, :, :, None]
    ldt = gd(log_dt)[:, :, None, None]
    bt_r = jnp.swapaxes(gd(b_re), 2, 3)
    bt_i = jnp.swapaxes(gd(b_im), 2, 3)
    btr_r = jnp.tile(bt_r, (1, 1, n, 1))
    btr_i = jnp.tile(bt_i, (1, 1, n, 1))
    cr_r = jnp.repeat(gd(c_re), n, axis=2)
    cr_i = jnp.repeat(gd(c_im), n, axis=2)
    ct_r = jnp.swapaxes(gd(c_re), 2, 3)
    ct_i = jnp.swapaxes(gd(c_im), 2, 3)
    args = (lrr, lir, lrc, lic, ldt, bt_r, bt_i, btr_r, btr_i, cr_r, cr_i, ct_r, ct_i)
    spec = lambda a: pl.BlockSpec((1,) + a.shape[1:], lambda i: (i,) + (0,) * (a.ndim - 1))
    return pl.pallas_call(
        _ssm_prep_kernel,
        grid=(g,),
        in_specs=[spec(a) for a in args],
        out_specs=[
            pl.BlockSpec((1, n * n, 2 * t), lambda i: (i, 0, 0)),
            pl.BlockSpec((1, n * t, 4 * p), lambda i: (i, 0, 0)),
            pl.BlockSpec((1, 4 * p, n * t), lambda i: (i, 0, 0)),
            pl.BlockSpec((1, 2, 2, 2 * p), lambda i: (i, 0, 0, 0)),
        ],
        out_shape=[
            jax.ShapeDtypeStruct((g, n * n, 2 * t), F32),
            jax.ShapeDtypeStruct((g, n * t, 4 * p), BF16),
            jax.ShapeDtypeStruct((g, 4 * p, n * t), BF16),
            jax.ShapeDtypeStruct((g, 2, 2, 2 * p), F32),
        ],
        compiler_params=_params("parallel"),
        name="ssm_prep",
    )(*args)


def _halves_to_rows(a, b, low):
    return jnp.where(low, a, pltpu.roll(b, SSM_CHUNK, 1)), jnp.where(low, pltpu.roll(a, SSM_CHUNK, 1), b)


def _ssm_kernel(u_ref, uc_ref, kcat_ref, wst_ref, wcar_ref, arow_ref, d_ref, y_ref,
                tz_ref, s_ref, sc_ref, h_ref, *, nb, nc, ncc):
    n, t, p = SSM_GROUP, SSM_CHUNK, SSM_STATE
    rows, rows_c = u_ref.shape[1], uc_ref.shape[1]
    low = lax.broadcasted_iota(jnp.int32, (1, SSM_ROW), 1) < t

    def build(i, carry):
        for m in range(n // 2):
            even = jnp.broadcast_to(kcat_ref[0, pl.ds(2 * m * n + i, 1), :], (t, SSM_ROW))
            odd = jnp.broadcast_to(kcat_ref[0, pl.ds((2 * m + 1) * n + i, 1), :], (t, SSM_ROW))
            even = pltpu.roll(even, t, 1, stride=1, stride_axis=0)
            odd = pltpu.roll(odd, 0, 1, stride=1, stride_axis=0)
            tz_ref[pl.ds(pl.multiple_of(i * t, t), t), m * SSM_ROW:(m + 1) * SSM_ROW] = jnp.where(low, even, odd).astype(BF16)
        return carry

    lax.fori_loop(0, n, build, 0, unroll=True)

    def chunk_rows(ref):
        first, second = [], []
        for m in range(n // 2):
            a, b = _halves_to_rows(ref[2 * m], ref[2 * m + 1], low)
            first.append(a.astype(BF16))
            second.append(b.astype(BF16))
        return jnp.concatenate([jnp.concatenate(first, axis=1), jnp.concatenate(second, axis=1)], axis=0)

    x = chunk_rows(u_ref)
    xc = chunk_rows(uc_ref)
    s_all = _dot(x, wst_ref[0])
    sc_all = _dot(xc, wst_ref[0])
    for d in range(2):
        s_ref[d, 0] = s_all[:, d * 2 * p:(d + 1) * 2 * p]
        s_ref[d, 1] = pltpu.roll(s_all[:, d * 2 * p:(d + 1) * 2 * p], p, 1)
        sc_ref[d, 0] = sc_all[:, d * 2 * p:(d + 1) * 2 * p]
        sc_ref[d, 1] = pltpu.roll(sc_all[:, d * 2 * p:(d + 1) * 2 * p], p, 1)

    coef = [(arow_ref[0, d, 0:1, :], arow_ref[0, d, 1:2, :]) for d in range(2)]

    def advance(d, h, hs, s, ss):
        a1, a2 = coef[d]
        return a1 * h + a2 * hs + s, a1 * hs - a2 * h + ss

    state = []
    for d in range(2):
        h = hs = jnp.zeros((nb, 2 * p), F32)
        order = range(2 * ncc) if d == 0 else reversed(range(2 * ncc))
        for chunk in order:
            sel = pl.ds((chunk % 2) * rows_c + chunk // 2, nb, stride=ncc)
            h, hs = advance(d, h, hs, sc_ref[d, 0, sel, :], sc_ref[d, 1, sel, :])
        state += [h, hs]

    def scan(k, carry):
        out = []
        for d in range(2):
            h, hs = carry[2 * d], carry[2 * d + 1]
            row = k if d == 0 else nc - 1 - k
            for half in ((0, 1) if d == 0 else (1, 0)):
                sel = pl.ds(half * rows + row, nb, stride=nc)
                h_ref[d, sel, :] = h
                h, hs = advance(d, h, hs, s_ref[d, 0, sel, :], s_ref[d, 1, sel, :])
            out += [h, hs]
        return tuple(out)

    lax.fori_loop(0, nc, scan, tuple(state), unroll=True)

    hb = jnp.concatenate([h_ref[0], h_ref[1]], axis=1).astype(BF16)
    nblk = 2 * SSM_ROW
    for q in range(n * t // nblk):
        y = _dot(x, tz_ref[:, q * nblk:(q + 1) * nblk]) + _dot(hb, wcar_ref[0, :, q * nblk:(q + 1) * nblk])
        for mm in range(2):
            m = 2 * q + mm
            even, odd = _halves_to_rows(y[:rows, mm * SSM_ROW:(mm + 1) * SSM_ROW],
                                        y[rows:, mm * SSM_ROW:(mm + 1) * SSM_ROW], low)
            y_ref[2 * m] = even + d_ref[2 * m] * u_ref[2 * m]
            y_ref[2 * m + 1] = odd + d_ref[2 * m + 1] * u_ref[2 * m + 1]


def _ssm(ut, utc, kcat, wst, wcar, arow, d_skip, nb):
    n, t, p, g = SSM_GROUP, SSM_CHUNK, SSM_STATE, SSM_GROUPS
    rows = ut.shape[1]
    rows_c = utc.shape[1]
    nc, ncc = rows // nb, rows_c // nb
    d3 = jnp.broadcast_to(d_skip.astype(F32)[:, None, None], (SSM_WIDTH, 1, SSM_ROW))
    return pl.pallas_call(
        functools.partial(_ssm_kernel, nb=nb, nc=nc, ncc=ncc),
        grid=(g,),
        in_specs=[
            pl.BlockSpec((n, rows, SSM_ROW), lambda i: (i, 0, 0)),
            pl.BlockSpec((n, rows_c, SSM_ROW), lambda i: (i, 0, 0)),
            pl.BlockSpec((1, n * n, 2 * t), lambda i: (i, 0, 0)),
            pl.BlockSpec((1, n * t, 4 * p), lambda i: (i, 0, 0)),
            pl.BlockSpec((1, 4 * p, n * t), lambda i: (i, 0, 0)),
            pl.BlockSpec((1, 2, 2, 2 * p), lambda i: (i, 0, 0, 0)),
            pl.BlockSpec((n, 1, SSM_ROW), lambda i: (i, 0, 0)),
        ],
        out_specs=pl.BlockSpec((n, rows, SSM_ROW), lambda i: (i, 0, 0)),
        out_shape=jax.ShapeDtypeStruct((SSM_WIDTH, rows, SSM_ROW), F32),
        scratch_shapes=[
            pltpu.VMEM((n * t, n * t), BF16),
            pltpu.VMEM((2, 2, 2 * rows, 2 * p), F32),
            pltpu.VMEM((2, 2, 2 * rows_c, 2 * p), F32),
            pltpu.VMEM((2, 2 * rows, 2 * p), F32),
        ],
        compiler_params=_params("parallel"),
        name="ssm_toeplitz",
    )(ut, utc, kcat, wst, wcar, arow, d3)


def _glu_kernel(yt_ref, wglut_ref, s_ref, flat_ref):
    t = SSM_ROW
    nck = yt_ref.shape[1]

    flat_ref[...] = yt_ref[...].reshape(flat_ref.shape)
    for j in range(nck):
        g = jax.nn.gelu(flat_ref[pl.ds(j, SSM_WIDTH, stride=nck), :])
        z = _dot(wglut_ref[...], g.astype(BF16))
        gate = 0.5 * (1.0 + jnp.tanh(0.5 * z))
        s_ref[j * t:(j + 1) * t, :] = (g * gate).T.astype(BF16)


def _ssm_glu(yt, wglut, bs, ls):
    nck = 8
    rows = yt.shape[1]
    s = pl.pallas_call(
        _glu_kernel,
        grid=(rows // nck,),
        in_specs=[
            pl.BlockSpec((SSM_WIDTH, nck, SSM_ROW), lambda i: (0, i, 0)),
            _resident(wglut.shape),
        ],
        out_specs=pl.BlockSpec((nck * SSM_ROW, SSM_WIDTH), lambda i: (i, 0)),
        out_shape=jax.ShapeDtypeStruct((rows * SSM_ROW, SSM_WIDTH), BF16),
        scratch_shapes=[pltpu.VMEM((SSM_WIDTH * nck, SSM_ROW), F32)],
        compiler_params=_params("parallel"),
        name="ssm_glu",
    )(yt, wglut)
    return s.reshape(bs, ls, SSM_WIDTH)


def _fourier_kernel(x_ref, r_ref, mod_ref, gain_ref, lhs1_ref, twc_ref, tws_ref, lhs2_ref, cs_ref, z_ref,
                    h_ref, are_ref, aim_ref, y_ref):
    r, pitch = FFT_RADIX, FFT_PITCH
    halves = FOURIER_GROUP_WIDTH // LANES

    def slab(i):
        return pl.ds(pl.multiple_of(i * pitch, 8), r)

    def gather(ref, i):
        return jnp.concatenate([ref[hf, pl.ds(i, r, stride=pitch), :] for hf in range(halves)], axis=1)

    def put(ref, i, val):
        for hf in range(halves):
            ref[hf, slab(i), :] = val[:, hf * LANES:(hf + 1) * LANES]

    gain = gain_ref[1:2, :]
    shift, scale = mod_ref[0, 3:4, :], mod_ref[0, 4:5, :]

    def fill(n1, carry):
        rows = pl.ds(pl.multiple_of(n1 * r, r), r)
        rstd = jnp.concatenate([r_ref[0, rows, :]] * halves, axis=1)
        put(h_ref, n1, (x_ref[0, rows, :] * rstd * gain) * (1.0 + scale) + shift)
        return carry

    lax.fori_loop(0, r, fill, 0, unroll=4)

    def stage1(jg, carry):
        cols = [gather(h_ref, jg * 4 + j).astype(BF16) for j in range(4)]
        a = _dot(lhs1_ref[...], jnp.concatenate(cols, axis=0))
        ch, sh = a[:4 * r], a[4 * r:]
        trow = pl.ds(pl.multiple_of(jg * 4 * r, 4 * r), 4 * r)
        tc = jnp.concatenate([twc_ref[trow, :]] * halves, axis=1)
        ts = jnp.concatenate([tws_ref[trow, :]] * halves, axis=1)
        a_re = ch * tc - sh * ts
        a_im = sh * tc + ch * ts
        for j in range(4):
            put(are_ref, jg * 4 + j, a_re[j * r:(j + 1) * r])
            put(aim_ref, jg * 4 + j, a_im[j * r:(j + 1) * r])
        return carry

    lax.fori_loop(0, r // 4, stage1, 0, unroll=4)

    def stage2(kg, carry):
        parts = [gather(are_ref, kg * 4 + j).astype(BF16) for j in range(4)]
        parts += [gather(aim_ref, kg * 4 + j).astype(BF16) for j in range(4)]
        y = _dot(lhs2_ref[...], jnp.concatenate(parts, axis=0))
        rows = pl.ds(pl.multiple_of(kg * 4 * r, 4 * r), 4 * r)
        y_ref[rows, :] = jnp.concatenate([y[:4 * r], y[4 * r:]], axis=1).astype(BF16)
        return carry

    lax.fori_loop(0, r // 4, stage2, 0, unroll=4)

    def channels(kb, carry):
        rows = pl.ds(pl.multiple_of(kb * 8 * r, 8 * r), 8 * r)
        z = _dot(y_ref[rows, :], cs_ref[...])
        for j in range(8):
            put(h_ref, kb * 8 + j, z[j * r:(j + 1) * r])
        return carry

    lax.fori_loop(0, r // 8, channels, 0, unroll=2)

    def emit(k2, carry):
        z_ref[0, pl.ds(pl.multiple_of(k2 * r, r), r), :] = gather(h_ref, k2).astype(BF16)
        return carry

    lax.fori_loop(0, r, emit, 0, unroll=4)


def _fourier_tables():
    r = FFT_RADIX
    n = r * r
    k = jnp.arange(r, dtype=jnp.int32)
    ang = ((k[:, None] * k[None, :]) % r).astype(F32) * (2.0 * math.pi / r)
    wc = jnp.cos(ang) / 8.0
    ws = jnp.sin(ang) / 8.0
    eye4 = jnp.eye(4, dtype=F32)
    bc, bs = jnp.kron(eye4, wc), jnp.kron(eye4, ws)
    lhs1 = jnp.concatenate([bc, bs], axis=0).astype(BF16)
    lhs2 = jnp.concatenate([jnp.concatenate([bc, -bs], axis=1),
                            jnp.concatenate([-bs, -bc], axis=1)], axis=0).astype(BF16)
    tw = ((k[:, None] * k[None, :]) % n).astype(F32) * (2.0 * math.pi / n)
    twc = jnp.broadcast_to(jnp.cos(tw).reshape(n, 1), (n, LANES))
    tws = jnp.broadcast_to(jnp.sin(tw).reshape(n, 1), (n, LANES))
    m = jnp.arange(FOURIER_GROUP_WIDTH, dtype=jnp.int32)
    angc = ((m[:, None] * m[None, :]) % FOURIER_GROUP_WIDTH).astype(F32) * (2.0 * math.pi / FOURIER_GROUP_WIDTH)
    cs = jnp.concatenate([jnp.cos(angc), jnp.sin(angc)], axis=0) / 16.0
    return lhs1, lhs2, twc, tws, cs.astype(BF16)


def _fourier(x, rstd, mod, gains):
    bs, ls, d = x.shape
    r, gw = FFT_RADIX, FOURIER_GROUP_WIDTH
    assert ls == r * r
    lhs1, lhs2, twc, tws, cs = _fourier_tables()
    scratch = pltpu.VMEM((gw // LANES, r * FFT_PITCH, LANES), F32)
    return pl.pallas_call(
        _fourier_kernel,
        grid=(bs, d // gw),
        in_specs=[
            pl.BlockSpec((1, ls, gw), lambda b, g: (b, 0, g)),
            pl.BlockSpec((1, ls, LANES), lambda b, g: (b, 0, 0)),
            pl.BlockSpec((1, N_MOD, gw), lambda b, g: (b, 0, g)),
            pl.BlockSpec((gains.shape[0], gw), lambda b, g: (0, g)),
            _resident(lhs1.shape), _resident(twc.shape), _resident(tws.shape),
            _resident(lhs2.shape), _resident(cs.shape),
        ],
        out_specs=pl.BlockSpec((1, ls, gw), lambda b, g: (b, 0, g)),
        out_shape=jax.ShapeDtypeStruct((bs, ls, d), BF16),
        scratch_shapes=[scratch, scratch, scratch, pltpu.VMEM((ls, 2 * gw), BF16)],
        compiler_params=_params("parallel", "parallel"),
        name="fourier_mix",
    )(x, rstd, mod, gains, lhs1, twc, tws, lhs2, cs)


def _even_layer(x, s_ctx, mod, gains, ffn_w, layer, w_in, q_gain, k_gain, sink, ssm, d_skip, w_glu, w_out, e):
    bs, ls, d = x.shape
    ctx_index = bs
    x = _half_ffn(x, mod, 0, gains, 0, ffn_w, (layer, 0))
    n_ctx = s_ctx.shape[1]
    s_ctx = _half_ffn(s_ctx.reshape(1, bs * n_ctx, d), mod, 0, gains, 0, ffn_w, (layer, 0),
                      mod_index=ctx_index).reshape(bs, n_ctx, d)

    wqkv = w_in[:, :Q_WIDTH + 2 * KV_WIDTH]
    wut = w_in[:, Q_WIDTH + 2 * KV_WIDTH:].T.astype(BF16)
    bd = jnp.kron(jnp.eye(N_Q_HEADS, dtype=F32), jnp.ones((HEAD_DIM, HEAD_DIM), F32)).astype(BF16)
    qg = jnp.tile(q_gain.astype(F32), N_Q_HEADS)[None, :]
    kg = jnp.tile(k_gain.astype(F32), N_KV_HEADS)[None, :]
    cos, sin = _rope_tables(ls)
    n_ctx = s_ctx.shape[1]
    one = jnp.ones((n_ctx, LANES), F32)
    q, kk, vv, ut = _in_proj(x, mod, gains, wqkv, wut, bd, qg, kg, cos, sin)
    _, kkc, vvc, utc = _in_proj(s_ctx, mod, gains, wqkv, wut, bd, qg, kg, one, 0.0 * one, mod_index=ctx_index)

    attn = _attention(q, kk, vv, kkc, vvc, sink.astype(F32))

    t = SSM_ROW
    kcat, wst, wcar, arow = _ssm_prep(*ssm)
    yt = _ssm(ut, utc.reshape(SSM_WIDTH, bs * n_ctx // t, t), kcat, wst, wcar, arow, d_skip, bs)

    s = _ssm_glu(yt, w_glu.T.astype(BF16), bs, ls)
    mix = ((attn, w_out, _picked(w_out, (e,), (Q_WIDTH, d), (0, 0))),
           (s, w_out, _picked(w_out, (e,), (SSM_WIDTH, d), (1, 0))))
    return _half_ffn(x, mod, 6, gains, 2, ffn_w, (layer, 1), mix=mix)


def _odd_layer(x, mod, gains, ffn_w, layer, w_f, o):
    x, rstd = _half_ffn(x, mod, 0, gains, 0, ffn_w, (layer, 0), emit_rstd=True)
    z = _fourier(x, rstd, mod, gains)
    mix = ((z, w_f, _picked(w_f, (o,), w_f.shape[1:])),)
    return _half_ffn(x, mod, 6, gains, 2, ffn_w, (layer, 1), mix=mix)


def kernel(x, c, ctx, c_ctx, w_ada, b_ada, norm_gain, ffn_w1, ffn_w3, ffn_w2, w_in, q_gain, k_gain, sink_logit, ssm_lam_re, ssm_lam_im, ssm_log_dt, ssm_b_re, ssm_b_im, ssm_c_re, ssm_c_im, ssm_d, ssm_w_glu, w_out, fourier_w_out):
    bs, ls, d = x.shape
    depth = w_ada.shape[0]
    assert depth == 2 and d == D_MODEL
    rows = 16
    cond = jnp.concatenate([c, c_ctx[None, :], jnp.zeros((rows - bs - 1, d), F32)], axis=0)
    mods = _ada_mod(cond, w_ada, b_ada)[:, :bs + 1].reshape(depth, bs + 1, N_MOD, d)

    ffn_w = (ffn_w1, ffn_w3, ffn_w2)
    ssm = (ssm_lam_re[0], ssm_lam_im[0], ssm_log_dt[0], ssm_b_re[0], ssm_b_im[0], ssm_c_re[0], ssm_c_im[0])
    x = _even_layer(x, ctx, mods[0], norm_gain[0], ffn_w, 0, w_in[0], q_gain[0], k_gain[0],
                    sink_logit[0], ssm, ssm_d[0], ssm_w_glu[0], w_out, 0)
    return _odd_layer(x, mods[1], norm_gain[1], ffn_w, 1, fourier_w_out, 0)
```

```python
import functools
import math

import jax
import jax.numpy as jnp
from jax import lax
from jax.experimental import pallas as pl
from jax.experimental.pallas import tpu as pltpu

F32 = jnp.float32
BF16 = jnp.bfloat16

D_MODEL = 1024
N_MOD = 9
EPS = 1e-6
HEAD_DIM = 64
N_Q_HEADS = 8
N_KV_HEADS = 2
Q_WIDTH = N_Q_HEADS * HEAD_DIM
KV_WIDTH = N_KV_HEADS * HEAD_DIM
ATT_BLOCK = 128
ROPE_BASE = 10000.0
NEG_INF = -1e30
GRID_W = 64
SSM_WIDTH = 512
SSM_GROUP = 16
SSM_GROUPS = 32
SSM_STATE = 64
SSM_ROW = 128
SSM_CHUNK = 64
FOURIER_GROUPS = 4
FOURIER_GROUP_WIDTH = 256
FFT_RADIX = 64
FFT_PITCH = 72
FFN_CHUNK = 256
LANES = 128
VMEM_LIMIT = 56 * 1024 * 1024


def _dot(a, b):
    return jnp.dot(a, b, preferred_element_type=F32)


def _dot_w(a, w):
    return lax.dot_general(a, w, (((1,), (0,)), ((), ())), preferred_element_type=F32)


def _dot_nt(a, b):
    return lax.dot_general(a, b, (((1,), (1,)), ((), ())), preferred_element_type=F32)


def _split(a):
    hi = a.astype(BF16)
    lo = (a - hi.astype(F32)).astype(BF16)
    return hi, lo


def _dot3(a, b):
    a_hi, a_lo = _split(a)
    b_hi, b_lo = _split(b)
    return _dot(a_hi, b_hi) + _dot(a_lo, b_hi) + _dot(a_hi, b_lo)


def _norm_mod(x, gain, shift, scale):
    ms = jnp.mean(x * x, axis=-1, keepdims=True)
    y = x * lax.rsqrt(ms + EPS) * gain
    return y * (1.0 + scale) + shift


def _params(*sem):
    return pltpu.CompilerParams(dimension_semantics=sem, vmem_limit_bytes=VMEM_LIMIT)


def _resident(shape):
    nd = len(shape)
    return pl.BlockSpec(shape, lambda *_: (0,) * nd, pipeline_mode=pl.Buffered(1))


def _ada_kernel(cond_ref, w_ref, b_ref, o_ref):
    a = cond_ref[...]
    a = a * jax.nn.sigmoid(a)
    o_ref[0] = _dot3(a, w_ref[0]) + b_ref[0]


def _ada_mod(cond, w_ada, b_ada):
    depth, d, nd = w_ada.shape
    rows = cond.shape[0]
    tn = nd // 8
    return pl.pallas_call(
        _ada_kernel,
        grid=(depth, nd // tn),
        in_specs=[
            pl.BlockSpec((rows, d), lambda l, j: (0, 0)),
            pl.BlockSpec((1, d, tn), lambda l, j: (l, 0, j)),
            pl.BlockSpec((1, 1, tn), lambda l, j: (l, 0, j)),
        ],
        out_specs=pl.BlockSpec((1, rows, tn), lambda l, j: (l, 0, j)),
        out_shape=jax.ShapeDtypeStruct((depth, rows, nd), F32),
        compiler_params=_params("parallel", "parallel"),
        name="ada_mod",
    )(cond, w_ada, b_ada.reshape(depth, 1, nd))


def _ffn_kernel(*refs, row0, grow, n_mix, emit_rstd):
    x_ref, mod_ref, gain_ref, w1_ref, w3_ref, w2_ref = refs[:6]
    mix_refs = refs[6:6 + 2 * n_mix]
    out_refs = refs[6 + 2 * n_mix:-1]
    acc_ref = refs[-1]
    o_ref = out_refs[0]
    x = x_ref[0]
    if n_mix:
        y = _dot_w(mix_refs[0][0], mix_refs[1][...])
        for k in range(1, n_mix):
            y = y + _dot_w(mix_refs[2 * k][0], mix_refs[2 * k + 1][...])
        x = x + mod_ref[0, 5:6, :] * y
        o_ref[0] = x
    shift = mod_ref[0, row0:row0 + 1, :]
    scale = mod_ref[0, row0 + 1:row0 + 2, :]
    gate = mod_ref[0, row0 + 2:row0 + 3, :]
    h = _norm_mod(x, gain_ref[grow:grow + 1, :], shift, scale)
    fc = FFN_CHUNK
    for c in range(w1_ref.shape[1] // fc):
        a = _dot_w(h, w1_ref[:, c * fc:(c + 1) * fc])
        b = _dot_w(h, w3_ref[:, c * fc:(c + 1) * fc])
        g = a * jax.nn.sigmoid(a) * b
        part = _dot_w(g, w2_ref[c * fc:(c + 1) * fc, :])
        if c == 0:
            acc_ref[...] = part
        else:
            acc_ref[...] += part
    base = o_ref[0] if n_mix else x_ref[0]
    o = base + (0.5 * gate) * acc_ref[...]
    o_ref[0] = o
    if emit_rstd:
        rstd = lax.rsqrt(jnp.mean(o * o, axis=-1, keepdims=True) + EPS)
        out_refs[1][0] = jnp.broadcast_to(rstd, out_refs[1].shape[1:])


def _picked(arr, lead, shape, start=None):
    start = start or (0,) * len(shape)
    return pl.BlockSpec((None,) * len(lead) + tuple(shape), lambda *_: tuple(lead) + tuple(start),
                        pipeline_mode=pl.Buffered(1))


def _half_ffn(s, mod, row0, gains, grow, ffn_w, lk, mod_index=None, mix=(), emit_rstd=False, tm=512):
    bs, ls, d = s.shape
    tm = min(tm, ls)
    if mod_index is None:
        mod_map = lambda b, i: (b, 0, 0)
    else:
        mod_map = lambda b, i: (mod_index, 0, 0)
    tok = lambda w: pl.BlockSpec((1, tm, w), lambda b, i: (b, i, 0))
    in_specs = [tok(d), pl.BlockSpec((1, N_MOD, d), mod_map), _resident(gains.shape)]
    in_specs += [_picked(w, lk, w.shape[2:]) for w in ffn_w]
    args = [s, mod, gains, *ffn_w]
    for z, w, w_spec in mix:
        in_specs += [tok(z.shape[2]), w_spec]
        args += [z, w]
    out_specs = [tok(d)]
    out_shape = [jax.ShapeDtypeStruct(s.shape, F32)]
    if emit_rstd:
        out_specs.append(tok(LANES))
        out_shape.append(jax.ShapeDtypeStruct((bs, ls, LANES), F32))
    outs = pl.pallas_call(
        functools.partial(_ffn_kernel, row0=row0, grow=grow, n_mix=len(mix), emit_rstd=emit_rstd),
        grid=(bs, ls // tm),
        in_specs=in_specs,
        out_specs=out_specs,
        out_shape=out_shape,
        scratch_shapes=[pltpu.VMEM((tm, d), F32)],
        compiler_params=_params("parallel", "parallel"),
        name="half_ffn",
    )(*args)
    return outs if emit_rstd else outs[0]


def _head_norm(t, bd, gain):
    ms = _dot((t * t).astype(BF16), bd) * (1.0 / HEAD_DIM)
    return t * lax.rsqrt(ms + EPS) * gain


def _rope(t, cos, sin):
    w = t.shape[1]
    half = HEAD_DIM // 2
    lane = lax.broadcasted_iota(jnp.int32, t.shape, 1)
    first = (lane % HEAD_DIM) < half
    rot = jnp.where(first, pltpu.roll(t, w - half, 1), pltpu.roll(t, half, 1))
    reps = w // LANES
    c = jnp.concatenate([cos] * reps, axis=1) if reps > 1 else cos
    s = jnp.concatenate([sin] * reps, axis=1) if reps > 1 else sin
    return t * c + rot * s


def _dup_heads(t):
    lane = lax.broadcasted_iota(jnp.int32, t.shape, 1)
    low = lane < HEAD_DIM
    r = pltpu.roll(t, HEAD_DIM, 1)
    return jnp.concatenate([jnp.where(low, t, r), jnp.where(low, r, t)], axis=1)


def _inproj_kernel(x_ref, mod_ref, gain_ref, wqkv_ref, wut_ref, bd_ref, qg_ref, kg_ref, cos_ref, sin_ref,
                   q_ref, kk_ref, vv_ref, ut_ref, *flat):
    x = x_ref[0]
    h = _norm_mod(x, gain_ref[1:2, :], mod_ref[0, 3:4, :], mod_ref[0, 4:5, :]).astype(BF16)
    qkv = _dot_w(h, wqkv_ref[...])
    q = qkv[:, :Q_WIDTH]
    k = qkv[:, Q_WIDTH:Q_WIDTH + KV_WIDTH]
    v = qkv[:, Q_WIDTH + KV_WIDTH:]
    cos = cos_ref[...]
    sin = sin_ref[...]
    q = _rope(_head_norm(q, bd_ref[...], qg_ref[...]), cos, sin)
    k = _rope(_head_norm(k, bd_ref[:KV_WIDTH, :KV_WIDTH], kg_ref[...]), cos, sin)
    q_ref[0] = (q * (HEAD_DIM ** -0.5)).astype(BF16)
    kk_ref[0] = _dup_heads(k).astype(BF16)
    vv_ref[0] = _dup_heads(v).astype(BF16)
    ut = _dot_nt(wut_ref[...], h)
    if not flat:
        ut_ref[...] = ut
        return
    flat_ref, = flat
    nr = ut_ref.shape[1]
    for j in range(nr):
        flat_ref[pl.ds(j, SSM_WIDTH, stride=nr), :] = ut[:, j * SSM_ROW:(j + 1) * SSM_ROW]

    ut_ref[...] = flat_ref[...].reshape(ut_ref.shape)


def _in_proj(s, mod, gains, wqkv, wut, bd, qg, kg, cos, sin, mod_index=None, tm=1024):
    bs, ls, d = s.shape
    tm = min(tm, ls)
    nt = ls // tm
    nr = tm // SSM_ROW
    by_rows = nr % 8 == 0
    if by_rows:
        ut_spec = pl.BlockSpec((SSM_WIDTH, nr, SSM_ROW), lambda b, i: (0, b * nt + i, 0))
        ut_shape = jax.ShapeDtypeStruct((SSM_WIDTH, bs * ls // SSM_ROW, SSM_ROW), F32)
        scratch = [pltpu.VMEM((SSM_WIDTH * nr, SSM_ROW), F32)]
    else:
        ut_spec = pl.BlockSpec((SSM_WIDTH, tm), lambda b, i: (0, b * nt + i))
        ut_shape = jax.ShapeDtypeStruct((SSM_WIDTH, bs * ls), F32)
        scratch = []
    if mod_index is None:
        mod_map = lambda b, i: (b, 0, 0)
    else:
        mod_map = lambda b, i: (mod_index, 0, 0)
    tok = lambda w: pl.BlockSpec((1, tm, w), lambda b, i: (b, i, 0))
    return pl.pallas_call(
        _inproj_kernel,
        grid=(bs, nt),
        in_specs=[
            tok(d),
            pl.BlockSpec((1, N_MOD, d), mod_map),
            _resident(gains.shape),
            _resident(wqkv.shape),
            _resident(wut.shape),
            _resident(bd.shape),
            _resident(qg.shape),
            _resident(kg.shape),
            pl.BlockSpec((tm, LANES), lambda b, i: (i, 0)),
            pl.BlockSpec((tm, LANES), lambda b, i: (i, 0)),
        ],
        out_specs=[tok(Q_WIDTH), tok(2 * KV_WIDTH), tok(2 * KV_WIDTH), ut_spec],
        out_shape=[
            jax.ShapeDtypeStruct((bs, ls, Q_WIDTH), BF16),
            jax.ShapeDtypeStruct((bs, ls, 2 * KV_WIDTH), BF16),
            jax.ShapeDtypeStruct((bs, ls, 2 * KV_WIDTH), BF16),
            ut_shape,
        ],
        scratch_shapes=scratch,
        compiler_params=_params("parallel", "parallel"),
        name="in_proj",
    )(s, mod, gains, wqkv, wut, bd, qg, kg, cos, sin)


def _rope_tables(n_lat):
    rows = n_lat // GRID_W
    row = jnp.repeat(jnp.arange(rows, dtype=F32), GRID_W)
    col = jnp.tile(jnp.arange(GRID_W, dtype=F32), rows)
    n_freq = HEAD_DIM // 4
    inv_freq = 1.0 / (ROPE_BASE ** (jnp.arange(n_freq, dtype=F32) / n_freq))
    ang = jnp.concatenate([row[:, None] * inv_freq, col[:, None] * inv_freq], axis=-1)
    cos, sin = jnp.cos(ang), jnp.sin(ang)
    cos_t = jnp.tile(cos, (1, LANES // (HEAD_DIM // 2)))
    sin_t = jnp.tile(jnp.concatenate([-sin, sin], axis=-1), (1, LANES // HEAD_DIM))
    return cos_t, sin_t


def _attn_kernel(sink_ref, q_ref, kk_ref, vv_ref, kkc_ref, vvc_ref, o_ref):
    blk = ATT_BLOCK
    per_step = q_ref.shape[1] // blk
    nb = pl.num_programs(1) * per_step
    for k in range(per_step):
        o_ref[0, k * blk:(k + 1) * blk, :] = _attn_block(
            pl.program_id(1) * per_step + k, nb, q_ref[0, k * blk:(k + 1) * blk, :],
            sink_ref, kk_ref, vv_ref, kkc_ref, vvc_ref)


def _attn_block(i, nb, q, sink_ref, kk_ref, vv_ref, kkc_ref, vvc_ref):
    blk = ATT_BLOCK
    lane = lax.broadcasted_iota(jnp.int32, (blk, LANES), 1)
    low = lane < HEAD_DIM
    rel = lane - lax.broadcasted_iota(jnp.int32, (blk, LANES), 0)
    ok_prev = rel >= jnp.where(i > 0, 0, 2 * blk)
    ok_next = rel <= jnp.where(i < nb - 1, 0, -2 * blk)
    j_prev = jnp.maximum(i - 1, 0)
    j_next = jnp.minimum(i + 1, nb - 1)
    zero = jnp.zeros_like(q[:, :LANES])
    cols = []
    for h in range(N_KV_HEADS):
        hs = slice(h * LANES, (h + 1) * LANES)

        def rows(ref, j):
            return ref[0, pl.ds(pl.multiple_of(j * blk, blk), blk), hs]

        keys = jnp.concatenate([rows(kk_ref, j_prev), rows(kk_ref, i), rows(kk_ref, j_next), kkc_ref[0, :, hs]], axis=0)
        vals = jnp.concatenate([rows(vv_ref, j_prev), rows(vv_ref, i), rows(vv_ref, j_next), vvc_ref[0, :, hs]], axis=0)
        qs = []
        for col in (2 * h, 2 * h + 1):
            qc = q[:, col * LANES:(col + 1) * LANES]
            qs.append(jnp.where(low, qc, zero))
            qs.append(jnp.where(low, zero, qc))
        s_all = _dot_nt(jnp.concatenate(qs, axis=0), keys)
        low_k = lax.broadcasted_iota(jnp.int32, vals.shape, 1) < HEAD_DIM
        v_ones = (jnp.where(low_k, vals, jnp.ones_like(vals)), jnp.where(low_k, jnp.ones_like(vals), vals))
        ps, tail = [], []
        for g in range(4):
            s = s_all[g * blk:(g + 1) * blk]
            parts = [jnp.where(ok_prev, s[:, :blk], NEG_INF), s[:, blk:2 * blk],
                     jnp.where(ok_next, s[:, 2 * blk:3 * blk], NEG_INF)]
            parts += [s[:, j * blk:(j + 1) * blk] for j in range(3, s.shape[1] // blk)]
            peak = parts[0]
            for part in parts[1:]:
                peak = jnp.maximum(peak, part)
            sink = sink_ref[4 * h + g]
            m = jnp.maximum(jnp.max(peak, axis=-1, keepdims=True), sink)
            ps.append(jnp.exp(jnp.concatenate([part - m for part in parts], axis=1).astype(BF16)))
            tail.append(jnp.exp(sink - m))
        og = []
        for par in range(2):
            o = _dot(jnp.concatenate([ps[par], ps[2 + par]], axis=0), v_ones[par])
            for k in range(2):
                ok = o[k * blk:(k + 1) * blk]
                den = (ok[:, HEAD_DIM:HEAD_DIM + 1] if par == 0 else ok[:, 0:1]) + tail[2 * k + par]
                og.append(ok * (1.0 / den))
        cols.append(jnp.where(low, og[0], og[2]))
        cols.append(jnp.where(low, og[1], og[3]))
    return jnp.concatenate(cols, axis=1).astype(BF16)


def _attention(q, kk, vv, kkc, vvc, sink):
    bs, ls, _ = q.shape
    tq = 4 * ATT_BLOCK
    n_ctx = kkc.shape[1]
    full = lambda n: pl.BlockSpec((1, n, 2 * KV_WIDTH), lambda b, i: (b, 0, 0))
    return pl.pallas_call(
        _attn_kernel,
        grid=(bs, ls // tq),
        in_specs=[
            pl.BlockSpec(memory_space=pltpu.SMEM),
            pl.BlockSpec((1, tq, Q_WIDTH), lambda b, i: (b, i, 0)),
            full(ls), full(ls), full(n_ctx), full(n_ctx),
        ],
        out_specs=pl.BlockSpec((1, tq, Q_WIDTH), lambda b, i: (b, i, 0)),
        out_shape=jax.ShapeDtypeStruct((bs, ls, Q_WIDTH), BF16),
        compiler_params=_params("parallel", "arbitrary"),
        name="window_attention",
    )(sink, q, kk, vv, kkc, vvc)


def _cpow(lr, li, dt, tau):
    mag = jnp.exp(lr * dt * tau)
    ang = li * dt * tau
    return mag * jnp.cos(ang), mag * jnp.sin(ang)


def _ssm_prep_kernel(lrr_ref, lir_ref, lrc_ref, lic_ref, ldt_ref, btr_ref, bti_ref, btrr_ref, btri_ref,
                     crr_ref, cri_ref, ctr_ref, cti_ref, kcat_ref, wst_ref, wcar_ref, arow_ref):
    p = SSM_STATE
    t = SSM_CHUNK
    lane_t = lax.broadcasted_iota(jnp.int32, (1, t), 1).astype(F32)
    sub_t = lax.broadcasted_iota(jnp.int32, (t, 1), 0).astype(F32)
    kparts = []
    for d in range(2):
        dt = jnp.exp(ldt_ref[0, d])
        lr, li = lrr_ref[0, d], lir_ref[0, d]
        lrc, lic = lrc_ref[0, d], lic_ref[0, d]
        ar, ai = _cpow(lr, li, dt, 1.0)
        nr = ar - 1.0
        den = lr * lr + li * li
        fr = (nr * lr + ai * li) / den
        fi = (ai * lr - nr * li) / den
        bt_r, bt_i = btr_ref[0, d], bti_ref[0, d]
        bbr = fr * bt_r - fi * bt_i
        bbi = fr * bt_i + fi * bt_r
        rep_r, rep_i = btrr_ref[0, d], btri_ref[0, d]
        bbr_rep = fr * rep_r - fi * rep_i
        bbi_rep = fr * rep_i + fi * rep_r
        c_r, c_i = crr_ref[0, d], cri_ref[0, d]
        m1 = c_r * bbr_rep - c_i * bbi_rep
        m2 = c_r * bbi_rep + c_i * bbr_rep
        tau = lane_t if d == 0 else (float(t) - lane_t)
        e_r, e_i = _cpow(lrc, lic, dt, tau)
        kmat = _dot3(m1, e_r) - _dot3(m2, e_i)
        if d == 0:
            kparts.append(kmat)
        else:
            lag0 = jnp.sum(m1, axis=1, keepdims=True)
            first = lax.broadcasted_iota(jnp.int32, (1, t), 1) == 0
            kparts[0] = kparts[0] + jnp.where(first, lag0, 0.0)
            kparts.insert(0, kmat)
        tau_s = (float(t - 1) - sub_t) if d == 0 else sub_t
        et_r, et_i = _cpow(lr, li, dt, tau_s)
        for i in range(SSM_GROUP):
            re = et_r * bbr[i:i + 1] - et_i * bbi[i:i + 1]
            im = et_r * bbi[i:i + 1] + et_i * bbr[i:i + 1]
            wst_ref[0, i * t:(i + 1) * t, d * 2 * p:(d + 1) * 2 * p] = jnp.concatenate([re, im], axis=1).astype(BF16)
        if d == 0:
            ec_r, ec_i = _cpow(lrc, lic, dt, lane_t + 1.0)
        else:
            ec_r, ec_i = e_r, e_i
        ct_r, ct_i = ctr_ref[0, d], cti_ref[0, d]
        car_re, car_im = [], []
        for jo in range(SSM_GROUP):
            cr = ct_r[:, jo:jo + 1]
            ci = ct_i[:, jo:jo + 1]
            car_re.append(cr * ec_r - ci * ec_i)
            car_im.append(-(cr * ec_i + ci * ec_r))
        wcar_ref[0, d * 2 * p:d * 2 * p + p, :] = jnp.concatenate(car_re, axis=1).astype(BF16)
        wcar_ref[0, d * 2 * p + p:(d + 1) * 2 * p, :] = jnp.concatenate(car_im, axis=1).astype(BF16)
        at_r, at_i = _cpow(lr, li, dt, float(t))
        arow_ref[0, d, 0:1, :] = jnp.concatenate([at_r, at_r], axis=1)
        arow_ref[0, d, 1:2, :] = jnp.concatenate([-at_i, at_i], axis=1)
    kcat_ref[0] = jnp.concatenate(kparts, axis=1)


def _ssm_prep(lam_re, lam_im, log_dt, b_re, b_im, c_re, c_im):
    g, p, n, t = SSM_GROUPS, SSM_STATE, SSM_GROUP, SSM_CHUNK
    gd = lambda a: jnp.swapaxes(a, 0, 1)
    lrr = gd(lam_re)[:, :, None, :]
    lir = gd(lam_im)[:, :, None, :]
    lrc = gd(lam_re)[:, :, :, None]
    lic = gd(lam_im)[:, :, :, None]
    ldt = gd(log_dt)[:, :, None, None]
    bt_r = jnp.swapaxes(gd(b_re), 2, 3)
    bt_i = jnp.swapaxes(gd(b_im), 2, 3)
    btr_r = jnp.tile(bt_r, (1, 1, n, 1))
    btr_i = jnp.tile(bt_i, (1, 1, n, 1))
    cr_r = jnp.repeat(gd(c_re), n, axis=2)
    cr_i = jnp.repeat(gd(c_im), n, axis=2)
    ct_r = jnp.swapaxes(gd(c_re), 2, 3)
    ct_i = jnp.swapaxes(gd(c_im), 2, 3)
    args = (lrr, lir, lrc, lic, ldt, bt_r, bt_i, btr_r, btr_i, cr_r, cr_i, ct_r, ct_i)
    spec = lambda a: pl.BlockSpec((1,) + a.shape[1:], lambda i: (i,) + (0,) * (a.ndim - 1))
    return pl.pallas_call(
        _ssm_prep_kernel,
        grid=(g,),
        in_specs=[spec(a) for a in args],
        out_specs=[
            pl.BlockSpec((1, n * n, 2 * t), lambda i: (i, 0, 0)),
            pl.BlockSpec((1, n * t, 4 * p), lambda i: (i, 0, 0)),
            pl.BlockSpec((1, 4 * p, n * t), lambda i: (i, 0, 0)),
            pl.BlockSpec((1, 2, 2, 2 * p), lambda i: (i, 0, 0, 0)),
        ],
        out_shape=[
            jax.ShapeDtypeStruct((g, n * n, 2 * t), F32),
            jax.ShapeDtypeStruct((g, n * t, 4 * p), BF16),
            jax.ShapeDtypeStruct((g, 4 * p, n * t), BF16),
            jax.ShapeDtypeStruct((g, 2, 2, 2 * p), F32),
        ],
        compiler_params=_params("parallel"),
        name="ssm_prep",
    )(*args)


def _halves_to_rows(a, b, low):
    return jnp.where(low, a, pltpu.roll(b, SSM_CHUNK, 1)), jnp.where(low, pltpu.roll(a, SSM_CHUNK, 1), b)


def _ssm_kernel(u_ref, uc_ref, kcat_ref, wst_ref, wcar_ref, arow_ref, d_ref, y_ref,
                tz_ref, s_ref, sc_ref, h_ref, *, nb, nc, ncc):
    n, t, p = SSM_GROUP, SSM_CHUNK, SSM_STATE
    rows, rows_c = u_ref.shape[1], uc_ref.shape[1]
    low = lax.broadcasted_iota(jnp.int32, (1, SSM_ROW), 1) < t

    def build(i, carry):
        for m in range(n // 2):
            even = jnp.broadcast_to(kcat_ref[0, pl.ds(2 * m * n + i, 1), :], (t, SSM_ROW))
            odd = jnp.broadcast_to(kcat_ref[0, pl.ds((2 * m + 1) * n + i, 1), :], (t, SSM_ROW))
            even = pltpu.roll(even, t, 1, stride=1, stride_axis=0)
            odd = pltpu.roll(odd, 0, 1, stride=1, stride_axis=0)
            tz_ref[pl.ds(pl.multiple_of(i * t, t), t), m * SSM_ROW:(m + 1) * SSM_ROW] = jnp.where(low, even, odd).astype(BF16)
        return carry

    lax.fori_loop(0, n, build, 0, unroll=True)

    def chunk_rows(ref):
        first, second = [], []
        for m in range(n // 2):
            a, b = _halves_to_rows(ref[2 * m], ref[2 * m + 1], low)
            first.append(a.astype(BF16))
            second.append(b.astype(BF16))
        return jnp.concatenate([jnp.concatenate(first, axis=1), jnp.concatenate(second, axis=1)], axis=0)

    x = chunk_rows(u_ref)
    xc = chunk_rows(uc_ref)
    s_all = _dot(x, wst_ref[0])
    sc_all = _dot(xc, wst_ref[0])
    for d in range(2):
        s_ref[d, 0] = s_all[:, d * 2 * p:(d + 1) * 2 * p]
        s_ref[d, 1] = pltpu.roll(s_all[:, d * 2 * p:(d + 1) * 2 * p], p, 1)
        sc_ref[d, 0] = sc_all[:, d * 2 * p:(d + 1) * 2 * p]
        sc_ref[d, 1] = pltpu.roll(sc_all[:, d * 2 * p:(d + 1) * 2 * p], p, 1)

    coef = [(arow_ref[0, d, 0:1, :], arow_ref[0, d, 1:2, :]) for d in range(2)]

    def advance(d, h, hs, s, ss):
        a1, a2 = coef[d]
        return a1 * h + a2 * hs + s, a1 * hs - a2 * h + ss

    state = []
    for d in range(2):
        h = hs = jnp.zeros((nb, 2 * p), F32)
        order = range(2 * ncc) if d == 0 else reversed(range(2 * ncc))
        for chunk in order:
            sel = pl.ds((chunk % 2) * rows_c + chunk // 2, nb, stride=ncc)
            h, hs = advance(d, h, hs, sc_ref[d, 0, sel, :], sc_ref[d, 1, sel, :])
        state += [h, hs]

    def scan(k, carry):
        out = []
        for d in range(2):
            h, hs = carry[2 * d], carry[2 * d + 1]
            row = k if d == 0 else nc - 1 - k
            for half in ((0, 1) if d == 0 else (1, 0)):
                sel = pl.ds(half * rows + row, nb, stride=nc)
                h_ref[d, sel, :] = h
                h, hs = advance(d, h, hs, s_ref[d, 0, sel, :], s_ref[d, 1, sel, :])
            out += [h, hs]
        return tuple(out)

    lax.fori_loop(0, nc, scan, tuple(state), unroll=True)

    hb = jnp.concatenate([h_ref[0], h_ref[1]], axis=1).astype(BF16)
    nblk = 2 * SSM_ROW
    for q in range(n * t // nblk):
        y = _dot(x, tz_ref[:, q * nblk:(q + 1) * nblk]) + _dot(hb, wcar_ref[0, :, q * nblk:(q + 1) * nblk])
        for mm in range(2):
            m = 2 * q + mm
            even, odd = _halves_to_rows(y[:rows, mm * SSM_ROW:(mm + 1) * SSM_ROW],
                                        y[rows:, mm * SSM_ROW:(mm + 1) * SSM_ROW], low)
            y_ref[2 * m] = even + d_ref[2 * m] * u_ref[2 * m]
            y_ref[2 * m + 1] = odd + d_ref[2 * m + 1] * u_ref[2 * m + 1]


def _ssm(ut, utc, kcat, wst, wcar, arow, d_skip, nb):
    n, t, p, g = SSM_GROUP, SSM_CHUNK, SSM_STATE, SSM_GROUPS
    rows = ut.shape[1]
    rows_c = utc.shape[1]
    nc, ncc = rows // nb, rows_c // nb
    d3 = jnp.broadcast_to(d_skip.astype(F32)[:, None, None], (SSM_WIDTH, 1, SSM_ROW))
    return pl.pallas_call(
        functools.partial(_ssm_kernel, nb=nb, nc=nc, ncc=ncc),
        grid=(g,),
        in_specs=[
            pl.BlockSpec((n, rows, SSM_ROW), lambda i: (i, 0, 0)),
            pl.BlockSpec((n, rows_c, SSM_ROW), lambda i: (i, 0, 0)),
            pl.BlockSpec((1, n * n, 2 * t), lambda i: (i, 0, 0)),
            pl.BlockSpec((1, n * t, 4 * p), lambda i: (i, 0, 0)),
            pl.BlockSpec((1, 4 * p, n * t), lambda i: (i, 0, 0)),
            pl.BlockSpec((1, 2, 2, 2 * p), lambda i: (i, 0, 0, 0)),
            pl.BlockSpec((n, 1, SSM_ROW), lambda i: (i, 0, 0)),
        ],
        out_specs=pl.BlockSpec((n, rows, SSM_ROW), lambda i: (i, 0, 0)),
        out_shape=jax.ShapeDtypeStruct((SSM_WIDTH, rows, SSM_ROW), F32),
        scratch_shapes=[
            pltpu.VMEM((n * t, n * t), BF16),
            pltpu.VMEM((2, 2, 2 * rows, 2 * p), F32),
            pltpu.VMEM((2, 2, 2 * rows_c, 2 * p), F32),
            pltpu.VMEM((2, 2 * rows, 2 * p), F32),
        ],
        compiler_params=_params("parallel"),
        name="ssm_toeplitz",
    )(ut, utc, kcat, wst, wcar, arow, d3)


def _glu_kernel(yt_ref, wglut_ref, s_ref, flat_ref):
    t = SSM_ROW
    nck = yt_ref.shape[1]

    flat_ref[...] = yt_ref[...].reshape(flat_ref.shape)
    for j in range(nck):
        g = jax.nn.gelu(flat_ref[pl.ds(j, SSM_WIDTH, stride=nck), :])
        z = _dot(wglut_ref[...], g.astype(BF16))
        gate = 0.5 * (1.0 + jnp.tanh(0.5 * z))
        s_ref[j * t:(j + 1) * t, :] = (g * gate).T.astype(BF16)


def _ssm_glu(yt, wglut, bs, ls):
    nck = 8
    rows = yt.shape[1]
    s = pl.pallas_call(
        _glu_kernel,
        grid=(rows // nck,),
        in_specs=[
            pl.BlockSpec((SSM_WIDTH, nck, SSM_ROW), lambda i: (0, i, 0)),
            _resident(wglut.shape),
        ],
        out_specs=pl.BlockSpec((nck * SSM_ROW, SSM_WIDTH), lambda i: (i, 0)),
        out_shape=jax.ShapeDtypeStruct((rows * SSM_ROW, SSM_WIDTH), BF16),
        scratch_shapes=[pltpu.VMEM((SSM_WIDTH * nck, SSM_ROW), F32)],
        compiler_params=_params("parallel"),
        name="ssm_glu",
    )(yt, wglut)
    return s.reshape(bs, ls, SSM_WIDTH)


def _fourier_kernel(x_ref, r_ref, mod_ref, gain_ref, lhs1_ref, twc_ref, tws_ref, lhs2_ref, cs_ref, z_ref,
                    h_ref, are_ref, aim_ref, y_ref):
    r, pitch = FFT_RADIX, FFT_PITCH
    halves = FOURIER_GROUP_WIDTH // LANES

    def slab(i):
        return pl.ds(pl.multiple_of(i * pitch, 8), r)

    def gather(ref, i):
        return jnp.concatenate([ref[hf, pl.ds(i, r, stride=pitch), :] for hf in range(halves)], axis=1)

    def put(ref, i, val):
        for hf in range(halves):
            ref[hf, slab(i), :] = val[:, hf * LANES:(hf + 1) * LANES]

    gain = gain_ref[1:2, :]
    shift, scale = mod_ref[0, 3:4, :], mod_ref[0, 4:5, :]

    def fill(n1, carry):
        rows = pl.ds(pl.multiple_of(n1 * r, r), r)
        rstd = jnp.concatenate([r_ref[0, rows, :]] * halves, axis=1)
        put(h_ref, n1, (x_ref[0, rows, :] * rstd * gain) * (1.0 + scale) + shift)
        return carry

    lax.fori_loop(0, r, fill, 0, unroll=4)

    def stage1(jg, carry):
        cols = [gather(h_ref, jg * 4 + j).astype(BF16) for j in range(4)]
        a = _dot(lhs1_ref[...], jnp.concatenate(cols, axis=0))
        ch, sh = a[:4 * r], a[4 * r:]
        trow = pl.ds(pl.multiple_of(jg * 4 * r, 4 * r), 4 * r)
        tc = jnp.concatenate([twc_ref[trow, :]] * halves, axis=1)
        ts = jnp.concatenate([tws_ref[trow, :]] * halves, axis=1)
        a_re = ch * tc - sh * ts
        a_im = sh * tc + ch * ts
        for j in range(4):
            put(are_ref, jg * 4 + j, a_re[j * r:(j + 1) * r])
            put(aim_ref, jg * 4 + j, a_im[j * r:(j + 1) * r])
        return carry

    lax.fori_loop(0, r // 4, stage1, 0, unroll=True)

    def stage2(kg, carry):
        parts = [gather(are_ref, kg * 4 + j).astype(BF16) for j in range(4)]
        parts += [gather(aim_ref, kg * 4 + j).astype(BF16) for j in range(4)]
        y = _dot(lhs2_ref[...], jnp.concatenate(parts, axis=0))
        rows = pl.ds(pl.multiple_of(kg * 4 * r, 4 * r), 4 * r)
        y_ref[rows, :] = jnp.concatenate([y[:4 * r], y[4 * r:]], axis=1).astype(BF16)
        return carry

    lax.fori_loop(0, r // 4, stage2, 0, unroll=True)

    def channels(kb, carry):
        rows = pl.ds(pl.multiple_of(kb * 8 * r, 8 * r), 8 * r)
        z = _dot(y_ref[rows, :], cs_ref[...])
        for j in range(8):
            put(h_ref, kb * 8 + j, z[j * r:(j + 1) * r])
        return carry

    lax.fori_loop(0, r // 8, channels, 0, unroll=True)

    def emit(k2, carry):
        z_ref[0, pl.ds(pl.multiple_of(k2 * r, r), r), :] = gather(h_ref, k2).astype(BF16)
        return carry

    lax.fori_loop(0, r, emit, 0, unroll=4)


def _fourier_tables():
    r = FFT_RADIX
    n = r * r
    k = jnp.arange(r, dtype=jnp.int32)
    ang = ((k[:, None] * k[None, :]) % r).astype(F32) * (2.0 * math.pi / r)
    wc = jnp.cos(ang) / 8.0
    ws = jnp.sin(ang) / 8.0
    eye4 = jnp.eye(4, dtype=F32)
    bc, bs = jnp.kron(eye4, wc), jnp.kron(eye4, ws)
    lhs1 = jnp.concatenate([bc, bs], axis=0).astype(BF16)
    lhs2 = jnp.concatenate([jnp.concatenate([bc, -bs], axis=1),
                            jnp.concatenate([-bs, -bc], axis=1)], axis=0).astype(BF16)
    tw = ((k[:, None] * k[None, :]) % n).astype(F32) * (2.0 * math.pi / n)
    twc = jnp.broadcast_to(jnp.cos(tw).reshape(n, 1), (n, LANES))
    tws = jnp.broadcast_to(jnp.sin(tw).reshape(n, 1), (n, LANES))
    m = jnp.arange(FOURIER_GROUP_WIDTH, dtype=jnp.int32)
    angc = ((m[:, None] * m[None, :]) % FOURIER_GROUP_WIDTH).astype(F32) * (2.0 * math.pi / FOURIER_GROUP_WIDTH)
    cs = jnp.concatenate([jnp.cos(angc), jnp.sin(angc)], axis=0) / 16.0
    return lhs1, lhs2, twc, tws, cs.astype(BF16)


def _fourier(x, rstd, mod, gains):
    bs, ls, d = x.shape
    r, gw = FFT_RADIX, FOURIER_GROUP_WIDTH
    assert ls == r * r
    lhs1, lhs2, twc, tws, cs = _fourier_tables()
    scratch = pltpu.VMEM((gw // LANES, r * FFT_PITCH, LANES), F32)
    return pl.pallas_call(
        _fourier_kernel,
        grid=(bs, d // gw),
        in_specs=[
            pl.BlockSpec((1, ls, gw), lambda b, g: (b, 0, g)),
            pl.BlockSpec((1, ls, LANES), lambda b, g: (b, 0, 0)),
            pl.BlockSpec((1, N_MOD, gw), lambda b, g: (b, 0, g)),
            pl.BlockSpec((gains.shape[0], gw), lambda b, g: (0, g)),
            _resident(lhs1.shape), _resident(twc.shape), _resident(tws.shape),
            _resident(lhs2.shape), _resident(cs.shape),
        ],
        out_specs=pl.BlockSpec((1, ls, gw), lambda b, g: (b, 0, g)),
        out_shape=jax.ShapeDtypeStruct((bs, ls, d), BF16),
        scratch_shapes=[scratch, scratch, scratch, pltpu.VMEM((ls, 2 * gw), BF16)],
        compiler_params=_params("parallel", "parallel"),
        name="fourier_mix",
    )(x, rstd, mod, gains, lhs1, twc, tws, lhs2, cs)


def _even_layer(x, s_ctx, mod, gains, ffn_w, layer, w_in, q_gain, k_gain, sink, ssm, d_skip, w_glu, w_out, e):
    bs, ls, d = x.shape
    ctx_index = bs
    x = _half_ffn(x, mod, 0, gains, 0, ffn_w, (layer, 0))
    n_ctx = s_ctx.shape[1]
    s_ctx = _half_ffn(s_ctx.reshape(1, bs * n_ctx, d), mod, 0, gains, 0, ffn_w, (layer, 0),
                      mod_index=ctx_index).reshape(bs, n_ctx, d)

    wqkv = w_in[:, :Q_WIDTH + 2 * KV_WIDTH]
    wut = w_in[:, Q_WIDTH + 2 * KV_WIDTH:].T.astype(BF16)
    bd = jnp.kron(jnp.eye(N_Q_HEADS, dtype=F32), jnp.ones((HEAD_DIM, HEAD_DIM), F32)).astype(BF16)
    qg = jnp.tile(q_gain.astype(F32), N_Q_HEADS)[None, :]
    kg = jnp.tile(k_gain.astype(F32), N_KV_HEADS)[None, :]
    cos, sin = _rope_tables(ls)
    n_ctx = s_ctx.shape[1]
    one = jnp.ones((n_ctx, LANES), F32)
    q, kk, vv, ut = _in_proj(x, mod, gains, wqkv, wut, bd, qg, kg, cos, sin)
    _, kkc, vvc, utc = _in_proj(s_ctx, mod, gains, wqkv, wut, bd, qg, kg, one, 0.0 * one, mod_index=ctx_index)

    attn = _attention(q, kk, vv, kkc, vvc, sink.astype(F32))

    t = SSM_ROW
    kcat, wst, wcar, arow = _ssm_prep(*ssm)
    yt = _ssm(ut, utc.reshape(SSM_WIDTH, bs * n_ctx // t, t), kcat, wst, wcar, arow, d_skip, bs)

    s = _ssm_glu(yt, w_glu.T.astype(BF16), bs, ls)
    mix = ((attn, w_out, _picked(w_out, (e,), (Q_WIDTH, d), (0, 0))),
           (s, w_out, _picked(w_out, (e,), (SSM_WIDTH, d), (1, 0))))
    return _half_ffn(x, mod, 6, gains, 2, ffn_w, (layer, 1), mix=mix)


def _odd_layer(x, mod, gains, ffn_w, layer, w_f, o):
    x, rstd = _half_ffn(x, mod, 0, gains, 0, ffn_w, (layer, 0), emit_rstd=True)
    z = _fourier(x, rstd, mod, gains)
    mix = ((z, w_f, _picked(w_f, (o,), w_f.shape[1:])),)
    return _half_ffn(x, mod, 6, gains, 2, ffn_w, (layer, 1), mix=mix)


def kernel(x, c, ctx, c_ctx, w_ada, b_ada, norm_gain, ffn_w1, ffn_w3, ffn_w2, w_in, q_gain, k_gain, sink_logit, ssm_lam_re, ssm_lam_im, ssm_log_dt, ssm_b_re, ssm_b_im, ssm_c_re, ssm_c_im, ssm_d, ssm_w_glu, w_out, fourier_w_out):
    bs, ls, d = x.shape
    depth = w_ada.shape[0]
    assert depth == 2 and d == D_MODEL
    rows = 16
    cond = jnp.concatenate([c, c_ctx[None, :], jnp.zeros((rows - bs - 1, d), F32)], axis=0)
    mods = _ada_mod(cond, w_ada, b_ada)[:, :bs + 1].reshape(depth, bs + 1, N_MOD, d)

    ffn_w = (ffn_w1, ffn_w3, ffn_w2)
    ssm = (ssm_lam_re[0], ssm_lam_im[0], ssm_log_dt[0], ssm_b_re[0], ssm_b_im[0], ssm_c_re[0], ssm_c_im[0])
    x = _even_layer(x, ctx, mods[0], norm_gain[0], ffn_w, 0, w_in[0], q_gain[0], k_gain[0],
                    sink_logit[0], ssm, ssm_d[0], ssm_w_glu[0], w_out, 0)
    return _odd_layer(x, mods[1], norm_gain[1], ffn_w, 1, fourier_w_out, 0)
```

```python
import functools
import math

import jax
import jax.numpy as jnp
from jax import lax
from jax.experimental import pallas as pl
from jax.experimental.pallas import tpu as pltpu

F32 = jnp.float32
BF16 = jnp.bfloat16

D_MODEL = 1024
N_MOD = 9
EPS = 1e-6
HEAD_DIM = 64
N_Q_HEADS = 8
N_KV_HEADS = 2
Q_WIDTH = N_Q_HEADS * HEAD_DIM
KV_WIDTH = N_KV_HEADS * HEAD_DIM
ATT_BLOCK = 128
ROPE_BASE = 10000.0
NEG_INF = -1e30
GRID_W = 64
SSM_WIDTH = 512
SSM_GROUP = 16
SSM_GROUPS = 32
SSM_STATE = 64
SSM_ROW = 128
SSM_CHUNK = 64
FOURIER_GROUPS = 4
FOURIER_GROUP_WIDTH = 256
FFT_RADIX = 64
FFT_PITCH = 72
FFN_CHUNK = 256
LANES = 128
VMEM_LIMIT = 56 * 1024 * 1024


def _dot(a, b):
    return jnp.dot(a, b, preferred_element_type=F32)


def _dot_w(a, w):
    return lax.dot_general(a, w, (((1,), (0,)), ((), ())), preferred_element_type=F32)


def _dot_nt(a, b):
    return lax.dot_general(a, b, (((1,), (1,)), ((), ())), preferred_element_type=F32)


def _split(a):
    hi = a.astype(BF16)
    lo = (a - hi.astype(F32)).astype(BF16)
    return hi, lo


def _dot3(a, b):
    a_hi, a_lo = _split(a)
    b_hi, b_lo = _split(b)
    return _dot(a_hi, b_hi) + _dot(a_lo, b_hi) + _dot(a_hi, b_lo)


def _norm_mod(x, gain, shift, scale):
    ms = jnp.mean(x * x, axis=-1, keepdims=True)
    return (x * lax.rsqrt(ms + EPS)) * (gain * (1.0 + scale)) + shift


def _params(*sem):
    return pltpu.CompilerParams(dimension_semantics=sem, vmem_limit_bytes=VMEM_LIMIT)


def _resident(shape):
    nd = len(shape)
    return pl.BlockSpec(shape, lambda *_: (0,) * nd, pipeline_mode=pl.Buffered(1))


def _ada_kernel(cond_ref, w_ref, b_ref, o_ref):
    a = cond_ref[...]
    a = a * jax.nn.sigmoid(a)
    o_ref[0] = _dot3(a, w_ref[0]) + b_ref[0]


def _ada_mod(cond, w_ada, b_ada):
    depth, d, nd = w_ada.shape
    rows = cond.shape[0]
    tn = nd // 8
    return pl.pallas_call(
        _ada_kernel,
        grid=(depth, nd // tn),
        in_specs=[
            pl.BlockSpec((rows, d), lambda l, j: (0, 0)),
            pl.BlockSpec((1, d, tn), lambda l, j: (l, 0, j)),
            pl.BlockSpec((1, 1, tn), lambda l, j: (l, 0, j)),
        ],
        out_specs=pl.BlockSpec((1, rows, tn), lambda l, j: (l, 0, j)),
        out_shape=jax.ShapeDtypeStruct((depth, rows, nd), F32),
        compiler_params=_params("parallel", "parallel"),
        name="ada_mod",
    )(cond, w_ada, b_ada.reshape(depth, 1, nd))


def _ffn_kernel(*refs, row0, grow, n_mix, emit_rstd):
    x_ref, mod_ref, gain_ref, w1_ref, w3_ref, w2_ref = refs[:6]
    mix_refs = refs[6:6 + 2 * n_mix]
    out_refs = refs[6 + 2 * n_mix:-1]
    acc_ref = refs[-1]
    o_ref = out_refs[0]
    x = x_ref[0]
    if n_mix:
        y = _dot_w(mix_refs[0][0], mix_refs[1][...])
        for k in range(1, n_mix):
            y = y + _dot_w(mix_refs[2 * k][0], mix_refs[2 * k + 1][...])
        x = x + mod_ref[0, 5:6, :] * y
        o_ref[0] = x
    shift = mod_ref[0, row0:row0 + 1, :]
    scale = mod_ref[0, row0 + 1:row0 + 2, :]
    gate = mod_ref[0, row0 + 2:row0 + 3, :]
    h = _norm_mod(x, gain_ref[grow:grow + 1, :], shift, scale)
    fc = FFN_CHUNK
    for c in range(w1_ref.shape[1] // fc):
        a = _dot_w(h, w1_ref[:, c * fc:(c + 1) * fc])
        b = _dot_w(h, w3_ref[:, c * fc:(c + 1) * fc])
        g = a * jax.nn.sigmoid(a) * b
        part = _dot_w(g, w2_ref[c * fc:(c + 1) * fc, :])
        if c == 0:
            acc_ref[...] = part
        else:
            acc_ref[...] += part
    base = o_ref[0] if n_mix else x_ref[0]
    o = base + (0.5 * gate) * acc_ref[...]
    o_ref[0] = o
    if emit_rstd:
        rstd = lax.rsqrt(jnp.mean(o * o, axis=-1, keepdims=True) + EPS)
        out_refs[1][0] = jnp.broadcast_to(rstd, out_refs[1].shape[1:])


def _picked(arr, lead, shape, start=None):
    start = start or (0,) * len(shape)
    return pl.BlockSpec((None,) * len(lead) + tuple(shape), lambda *_: tuple(lead) + tuple(start),
                        pipeline_mode=pl.Buffered(1))


def _half_ffn(s, mod, row0, gains, grow, ffn_w, lk, mod_index=None, mix=(), emit_rstd=False, tm=512):
    bs, ls, d = s.shape
    tm = min(tm, ls)
    if mod_index is None:
        mod_map = lambda b, i: (b, 0, 0)
    else:
        mod_map = lambda b, i: (mod_index, 0, 0)
    tok = lambda w: pl.BlockSpec((1, tm, w), lambda b, i: (b, i, 0))
    in_specs = [tok(d), pl.BlockSpec((1, N_MOD, d), mod_map), _resident(gains.shape)]
    in_specs += [_picked(w, lk, w.shape[2:]) for w in ffn_w]
    args = [s, mod, gains, *ffn_w]
    for z, w, w_spec in mix:
        in_specs += [tok(z.shape[2]), w_spec]
        args += [z, w]
    out_specs = [tok(d)]
    out_shape = [jax.ShapeDtypeStruct(s.shape, F32)]
    if emit_rstd:
        out_specs.append(tok(LANES))
        out_shape.append(jax.ShapeDtypeStruct((bs, ls, LANES), F32))
    outs = pl.pallas_call(
        functools.partial(_ffn_kernel, row0=row0, grow=grow, n_mix=len(mix), emit_rstd=emit_rstd),
        grid=(bs, ls // tm),
        in_specs=in_specs,
        out_specs=out_specs,
        out_shape=out_shape,
        scratch_shapes=[pltpu.VMEM((tm, d), F32)],
        compiler_params=_params("parallel", "parallel"),
        name="half_ffn",
    )(*args)
    return outs if emit_rstd else outs[0]


def _head_norm(t, bd, gain):
    ms = _dot((t * t).astype(BF16), bd) * (1.0 / HEAD_DIM)
    return t * lax.rsqrt(ms + EPS) * gain


def _rope(t, cos, sin):
    w = t.shape[1]
    half = HEAD_DIM // 2
    lane = lax.broadcasted_iota(jnp.int32, t.shape, 1)
    first = (lane % HEAD_DIM) < half
    rot = jnp.where(first, pltpu.roll(t, w - half, 1), pltpu.roll(t, half, 1))
    reps = w // LANES
    c = jnp.concatenate([cos] * reps, axis=1) if reps > 1 else cos
    s = jnp.concatenate([sin] * reps, axis=1) if reps > 1 else sin
    return t * c + rot * s


def _dup_heads(t):
    lane = lax.broadcasted_iota(jnp.int32, t.shape, 1)
    low = lane < HEAD_DIM
    r = pltpu.roll(t, HEAD_DIM, 1)
    return jnp.concatenate([jnp.where(low, t, r), jnp.where(low, r, t)], axis=1)


def _inproj_kernel(x_ref, mod_ref, gain_ref, wqkv_ref, wut_ref, bd_ref, qg_ref, kg_ref, cos_ref, sin_ref,
                   q_ref, kk_ref, vv_ref, ut_ref, *flat):
    x = x_ref[0]
    h = _norm_mod(x, gain_ref[1:2, :], mod_ref[0, 3:4, :], mod_ref[0, 4:5, :]).astype(BF16)
    qkv = _dot_w(h, wqkv_ref[...])
    q = qkv[:, :Q_WIDTH]
    k = qkv[:, Q_WIDTH:Q_WIDTH + KV_WIDTH]
    v = qkv[:, Q_WIDTH + KV_WIDTH:]
    cos = cos_ref[...]
    sin = sin_ref[...]
    q = _rope(_head_norm(q, bd_ref[...], qg_ref[...]), cos, sin)
    k = _rope(_head_norm(k, bd_ref[:KV_WIDTH, :KV_WIDTH], kg_ref[...]), cos, sin)
    q_ref[0] = (q * (HEAD_DIM ** -0.5)).astype(BF16)
    kk_ref[0] = _dup_heads(k).astype(BF16)
    vv_ref[0] = _dup_heads(v).astype(BF16)
    ut = _dot_nt(wut_ref[...], h)
    if not flat:
        ut_ref[...] = ut
        return
    flat_ref, = flat
    nr = ut_ref.shape[1]
    for j in range(nr):
        flat_ref[pl.ds(j, SSM_WIDTH, stride=nr), :] = ut[:, j * SSM_ROW:(j + 1) * SSM_ROW]

    ut_ref[...] = flat_ref[...].reshape(ut_ref.shape)


def _in_proj(s, mod, gains, wqkv, wut, bd, qg, kg, cos, sin, mod_index=None, tm=1024):
    bs, ls, d = s.shape
    tm = min(tm, ls)
    nt = ls // tm
    nr = tm // SSM_ROW
    by_rows = nr % 8 == 0
    if by_rows:
        ut_spec = pl.BlockSpec((SSM_WIDTH, nr, SSM_ROW), lambda b, i: (0, b * nt + i, 0))
        ut_shape = jax.ShapeDtypeStruct((SSM_WIDTH, bs * ls // SSM_ROW, SSM_ROW), F32)
        scratch = [pltpu.VMEM((SSM_WIDTH * nr, SSM_ROW), F32)]
    else:
        ut_spec = pl.BlockSpec((SSM_WIDTH, tm), lambda b, i: (0, b * nt + i))
        ut_shape = jax.ShapeDtypeStruct((SSM_WIDTH, bs * ls), F32)
        scratch = []
    if mod_index is None:
        mod_map = lambda b, i: (b, 0, 0)
    else:
        mod_map = lambda b, i: (mod_index, 0, 0)
    tok = lambda w: pl.BlockSpec((1, tm, w), lambda b, i: (b, i, 0))
    return pl.pallas_call(
        _inproj_kernel,
        grid=(bs, nt),
        in_specs=[
            tok(d),
            pl.BlockSpec((1, N_MOD, d), mod_map),
            _resident(gains.shape),
            _resident(wqkv.shape),
            _resident(wut.shape),
            _resident(bd.shape),
            _resident(qg.shape),
            _resident(kg.shape),
            pl.BlockSpec((tm, LANES), lambda b, i: (i, 0)),
            pl.BlockSpec((tm, LANES), lambda b, i: (i, 0)),
        ],
        out_specs=[tok(Q_WIDTH), tok(2 * KV_WIDTH), tok(2 * KV_WIDTH), ut_spec],
        out_shape=[
            jax.ShapeDtypeStruct((bs, ls, Q_WIDTH), BF16),
            jax.ShapeDtypeStruct((bs, ls, 2 * KV_WIDTH), BF16),
            jax.ShapeDtypeStruct((bs, ls, 2 * KV_WIDTH), BF16),
            ut_shape,
        ],
        scratch_shapes=scratch,
        compiler_params=_params("parallel", "parallel"),
        name="in_proj",
    )(s, mod, gains, wqkv, wut, bd, qg, kg, cos, sin)


def _rope_tables(n_lat):
    rows = n_lat // GRID_W
    row = jnp.repeat(jnp.arange(rows, dtype=F32), GRID_W)
    col = jnp.tile(jnp.arange(GRID_W, dtype=F32), rows)
    n_freq = HEAD_DIM // 4
    inv_freq = 1.0 / (ROPE_BASE ** (jnp.arange(n_freq, dtype=F32) / n_freq))
    ang = jnp.concatenate([row[:, None] * inv_freq, col[:, None] * inv_freq], axis=-1)
    cos, sin = jnp.cos(ang), jnp.sin(ang)
    cos_t = jnp.tile(cos, (1, LANES // (HEAD_DIM // 2)))
    sin_t = jnp.tile(jnp.concatenate([-sin, sin], axis=-1), (1, LANES // HEAD_DIM))
    return cos_t, sin_t


def _attn_kernel(sink_ref, q_ref, kk_ref, vv_ref, kkc_ref, vvc_ref, o_ref):
    blk = ATT_BLOCK
    per_step = q_ref.shape[1] // blk
    nb = pl.num_programs(1) * per_step
    for k in range(per_step):
        o_ref[0, k * blk:(k + 1) * blk, :] = _attn_block(
            pl.program_id(1) * per_step + k, nb, q_ref[0, k * blk:(k + 1) * blk, :],
            sink_ref, kk_ref, vv_ref, kkc_ref, vvc_ref)


def _attn_block(i, nb, q, sink_ref, kk_ref, vv_ref, kkc_ref, vvc_ref):
    blk = ATT_BLOCK
    lane = lax.broadcasted_iota(jnp.int32, (blk, LANES), 1)
    low = lane < HEAD_DIM
    rel = lane - lax.broadcasted_iota(jnp.int32, (blk, LANES), 0)
    ok_prev = rel >= jnp.where(i > 0, 0, 2 * blk)
    ok_next = rel <= jnp.where(i < nb - 1, 0, -2 * blk)
    j_prev = jnp.maximum(i - 1, 0)
    j_next = jnp.minimum(i + 1, nb - 1)
    zero = jnp.zeros_like(q[:, :LANES])
    cols = []
    for h in range(N_KV_HEADS):
        hs = slice(h * LANES, (h + 1) * LANES)

        def rows(ref, j):
            return ref[0, pl.ds(pl.multiple_of(j * blk, blk), blk), hs]

        keys = jnp.concatenate([rows(kk_ref, j_prev), rows(kk_ref, i), rows(kk_ref, j_next), kkc_ref[0, :, hs]], axis=0)
        vals = jnp.concatenate([rows(vv_ref, j_prev), rows(vv_ref, i), rows(vv_ref, j_next), vvc_ref[0, :, hs]], axis=0)
        qs = []
        for col in (2 * h, 2 * h + 1):
            qc = q[:, col * LANES:(col + 1) * LANES]
            qs.append(jnp.where(low, qc, zero))
            qs.append(jnp.where(low, zero, qc))
        s_all = _dot_nt(jnp.concatenate(qs, axis=0), keys)
        low_k = lax.broadcasted_iota(jnp.int32, vals.shape, 1) < HEAD_DIM
        v_ones = (jnp.where(low_k, vals, jnp.ones_like(vals)), jnp.where(low_k, jnp.ones_like(vals), vals))
        ps, tail = [], []
        for g in range(4):
            s = s_all[g * blk:(g + 1) * blk]
            parts = [jnp.where(ok_prev, s[:, :blk], NEG_INF), s[:, blk:2 * blk],
                     jnp.where(ok_next, s[:, 2 * blk:3 * blk], NEG_INF)]
            parts += [s[:, j * blk:(j + 1) * blk] for j in range(3, s.shape[1] // blk)]
            peak = parts[0]
            for part in parts[1:]:
                peak = jnp.maximum(peak, part)
            sink = sink_ref[4 * h + g]
            m = jnp.maximum(jnp.max(peak, axis=-1, keepdims=True), sink)
            ps.append(jnp.exp(jnp.concatenate([part - m for part in parts], axis=1).astype(BF16)))
            tail.append(jnp.exp(sink - m))
        og = []
        for par in range(2):
            o = _dot(jnp.concatenate([ps[par], ps[2 + par]], axis=0), v_ones[par])
            for k in range(2):
                ok = o[k * blk:(k + 1) * blk]
                den = (ok[:, HEAD_DIM:HEAD_DIM + 1] if par == 0 else ok[:, 0:1]) + tail[2 * k + par]
                og.append(ok * (1.0 / den))
        cols.append(jnp.where(low, og[0], og[2]))
        cols.append(jnp.where(low, og[1], og[3]))
    return jnp.concatenate(cols, axis=1).astype(BF16)


def _attention(q, kk, vv, kkc, vvc, sink):
    bs, ls, _ = q.shape
    tq = 8 * ATT_BLOCK
    n_ctx = kkc.shape[1]
    full = lambda n: pl.BlockSpec((1, n, 2 * KV_WIDTH), lambda b, i: (b, 0, 0))
    return pl.pallas_call(
        _attn_kernel,
        grid=(bs, ls // tq),
        in_specs=[
            pl.BlockSpec(memory_space=pltpu.SMEM),
            pl.BlockSpec((1, tq, Q_WIDTH), lambda b, i: (b, i, 0)),
            full(ls), full(ls), full(n_ctx), full(n_ctx),
        ],
        out_specs=pl.BlockSpec((1, tq, Q_WIDTH), lambda b, i: (b, i, 0)),
        out_shape=jax.ShapeDtypeStruct((bs, ls, Q_WIDTH), BF16),
        compiler_params=_params("parallel", "arbitrary"),
        name="window_attention",
    )(sink, q, kk, vv, kkc, vvc)


def _cpow(lr, li, dt, tau):
    mag = jnp.exp(lr * dt * tau)
    ang = li * dt * tau
    return mag * jnp.cos(ang), mag * jnp.sin(ang)


def _ssm_prep_kernel(lrr_ref, lir_ref, lrc_ref, lic_ref, ldt_ref, btr_ref, bti_ref, btrr_ref, btri_ref,
                     crr_ref, cri_ref, ctr_ref, cti_ref, kcat_ref, wst_ref, wcar_ref, arow_ref):
    p = SSM_STATE
    t = SSM_CHUNK
    lane_t = lax.broadcasted_iota(jnp.int32, (1, t), 1).astype(F32)
    sub_t = lax.broadcasted_iota(jnp.int32, (t, 1), 0).astype(F32)
    kparts = []
    for d in range(2):
        dt = jnp.exp(ldt_ref[0, d])
        lr, li = lrr_ref[0, d], lir_ref[0, d]
        lrc, lic = lrc_ref[0, d], lic_ref[0, d]
        ar, ai = _cpow(lr, li, dt, 1.0)
        nr = ar - 1.0
        den = lr * lr + li * li
        fr = (nr * lr + ai * li) / den
        fi = (ai * lr - nr * li) / den
        bt_r, bt_i = btr_ref[0, d], bti_ref[0, d]
        bbr = fr * bt_r - fi * bt_i
        bbi = fr * bt_i + fi * bt_r
        rep_r, rep_i = btrr_ref[0, d], btri_ref[0, d]
        bbr_rep = fr * rep_r - fi * rep_i
        bbi_rep = fr * rep_i + fi * rep_r
        c_r, c_i = crr_ref[0, d], cri_ref[0, d]
        m1 = c_r * bbr_rep - c_i * bbi_rep
        m2 = c_r * bbi_rep + c_i * bbr_rep
        tau = lane_t if d == 0 else (float(t) - lane_t)
        e_r, e_i = _cpow(lrc, lic, dt, tau)
        kmat = _dot3(m1, e_r) - _dot3(m2, e_i)
        if d == 0:
            kparts.append(kmat)
        else:
            lag0 = jnp.sum(m1, axis=1, keepdims=True)
            first = lax.broadcasted_iota(jnp.int32, (1, t), 1) == 0
            kparts[0] = kparts[0] + jnp.where(first, lag0, 0.0)
            kparts.insert(0, kmat)
        tau_s = (float(t - 1) - sub_t) if d == 0 else sub_t
        et_r, et_i = _cpow(lr, li, dt, tau_s)
        for i in range(SSM_GROUP):
            re = et_r * bbr[i:i + 1] - et_i * bbi[i:i + 1]
            im = et_r * bbi[i:i + 1] + et_i * bbr[i:i + 1]
            wst_ref[0, i * t:(i + 1) * t, d * 2 * p:(d + 1) * 2 * p] = jnp.concatenate([re, im], axis=1).astype(BF16)
        if d == 0:
            ec_r, ec_i = _cpow(lrc, lic, dt, lane_t + 1.0)
        else:
            ec_r, ec_i = e_r, e_i
        ct_r, ct_i = ctr_ref[0, d], cti_ref[0, d]
        car_re, car_im = [], []
        for jo in range(SSM_GROUP):
            cr = ct_r[:, jo:jo + 1]
            ci = ct_i[:, jo:jo + 1]
            car_re.append(cr * ec_r - ci * ec_i)
            car_im.append(-(cr * ec_i + ci * ec_r))
        wcar_ref[0, d * 2 * p:d * 2 * p + p, :] = jnp.concatenate(car_re, axis=1).astype(BF16)
        wcar_ref[0, d * 2 * p + p:(d + 1) * 2 * p, :] = jnp.concatenate(car_im, axis=1).astype(BF16)
        at_r, at_i = _cpow(lr, li, dt, float(t))
        arow_ref[0, d, 0:1, :] = jnp.concatenate([at_r, at_r], axis=1)
        arow_ref[0, d, 1:2, :] = jnp.concatenate([-at_i, at_i], axis=1)
    kcat_ref[0] = jnp.concatenate(kparts, axis=1)


def _ssm_prep(lam_re, lam_im, log_dt, b_re, b_im, c_re, c_im):
    g, p, n, t = SSM_GROUPS, SSM_STATE, SSM_GROUP, SSM_CHUNK
    gd = lambda a: jnp.swapaxes(a, 0, 1)
    lrr = gd(lam_re)[:, :, None, :]
    lir = gd(lam_im)[:, :, None, :]
    lrc = gd(lam_re)[:, :, :, None]
    lic = gd(lam_im)[:, :, :, None]
    ldt = gd(log_dt)[:, :, None, None]
    bt_r = jnp.swapaxes(gd(b_re), 2, 3)
    bt_i = jnp.swapaxes(gd(b_im), 2, 3)
    btr_r = jnp.tile(bt_r, (1, 1, n, 1))
    btr_i = jnp.tile(bt_i, (1, 1, n, 1))
    cr_r = jnp.repeat(gd(c_re), n, axis=2)
    cr_i = jnp.repeat(gd(c_im), n, axis=2)
    ct_r = jnp.swapaxes(gd(c_re), 2, 3)
    ct_i = jnp.swapaxes(gd(c_im), 2, 3)
    args = (lrr, lir, lrc, lic, ldt, bt_r, bt_i, btr_r, btr_i, cr_r, cr_i, ct_r, ct_i)
    spec = lambda a: pl.BlockSpec((1,) + a.shape[1:], lambda i: (i,) + (0,) * (a.ndim - 1))
    return pl.pallas_call(
        _ssm_prep_kernel,
        grid=(g,),
        in_specs=[spec(a) for a in args],
        out_specs=[
            pl.BlockSpec((1, n * n, 2 * t), lambda i: (i, 0, 0)),
            pl.BlockSpec((1, n * t, 4 * p), lambda i: (i, 0, 0)),
            pl.BlockSpec((1, 4 * p, n * t), lambda i: (i, 0, 0)),
            pl.BlockSpec((1, 2, 2, 2 * p), lambda i: (i, 0, 0, 0)),
        ],
        out_shape=[
            jax.ShapeDtypeStruct((g, n * n, 2 * t), F32),
            jax.ShapeDtypeStruct((g, n * t, 4 * p), BF16),
            jax.ShapeDtypeStruct((g, 4 * p, n * t), BF16),
            jax.ShapeDtypeStruct((g, 2, 2, 2 * p), F32),
        ],
        compiler_params=_params("parallel"),
        name="ssm_prep",
    )(*args)


def _halves_to_rows(a, b, low):
    return jnp.where(low, a, pltpu.roll(b, SSM_CHUNK, 1)), jnp.where(low, pltpu.roll(a, SSM_CHUNK, 1), b)


def _ssm_kernel(u_ref, uc_ref, kcat_ref, wst_ref, wcar_ref, arow_ref, d_ref, y_ref,
                tz_ref, s_ref, sc_ref, h_ref, *, nb, nc, ncc):
    n, t, p = SSM_GROUP, SSM_CHUNK, SSM_STATE
    rows, rows_c = u_ref.shape[1], uc_ref.shape[1]
    low = lax.broadcasted_iota(jnp.int32, (1, SSM_ROW), 1) < t

    def build(i, carry):
        for m in range(n // 2):
            even = jnp.broadcast_to(kcat_ref[0, pl.ds(2 * m * n + i, 1), :], (t, SSM_ROW))
            odd = jnp.broadcast_to(kcat_ref[0, pl.ds((2 * m + 1) * n + i, 1), :], (t, SSM_ROW))
            even = pltpu.roll(even, t, 1, stride=1, stride_axis=0)
            odd = pltpu.roll(odd, 0, 1, stride=1, stride_axis=0)
            tz_ref[pl.ds(pl.multiple_of(i * t, t), t), m * SSM_ROW:(m + 1) * SSM_ROW] = jnp.where(low, even, odd).astype(BF16)
        return carry

    lax.fori_loop(0, n, build, 0, unroll=True)

    def chunk_rows(ref):
        first, second = [], []
        for m in range(n // 2):
            a, b = _halves_to_rows(ref[2 * m], ref[2 * m + 1], low)
            first.append(a.astype(BF16))
            second.append(b.astype(BF16))
        return jnp.concatenate([jnp.concatenate(first, axis=1), jnp.concatenate(second, axis=1)], axis=0)

    x = chunk_rows(u_ref)
    xc = chunk_rows(uc_ref)
    s_all = _dot(x, wst_ref[0])
    sc_all = _dot(xc, wst_ref[0])
    for d in range(2):
        s_ref[d, 0] = s_all[:, d * 2 * p:(d + 1) * 2 * p]
        s_ref[d, 1] = pltpu.roll(s_all[:, d * 2 * p:(d + 1) * 2 * p], p, 1)
        sc_ref[d, 0] = sc_all[:, d * 2 * p:(d + 1) * 2 * p]
        sc_ref[d, 1] = pltpu.roll(sc_all[:, d * 2 * p:(d + 1) * 2 * p], p, 1)

    coef = [(arow_ref[0, d, 0:1, :], arow_ref[0, d, 1:2, :]) for d in range(2)]

    def advance(d, h, hs, s, ss):
        a1, a2 = coef[d]
        return a1 * h + a2 * hs + s, a1 * hs - a2 * h + ss

    state = []
    for d in range(2):
        h = hs = jnp.zeros((nb, 2 * p), F32)
        order = range(2 * ncc) if d == 0 else reversed(range(2 * ncc))
        for chunk in order:
            sel = pl.ds((chunk % 2) * rows_c + chunk // 2, nb, stride=ncc)
            h, hs = advance(d, h, hs, sc_ref[d, 0, sel, :], sc_ref[d, 1, sel, :])
        state += [h, hs]

    def scan(k, carry):
        out = []
        for d in range(2):
            h, hs = carry[2 * d], carry[2 * d + 1]
            row = k if d == 0 else nc - 1 - k
            for half in ((0, 1) if d == 0 else (1, 0)):
                sel = pl.ds(half * rows + row, nb, stride=nc)
                h_ref[d, sel, :] = h
                h, hs = advance(d, h, hs, s_ref[d, 0, sel, :], s_ref[d, 1, sel, :])
            out += [h, hs]
        return tuple(out)

    lax.fori_loop(0, nc, scan, tuple(state), unroll=True)

    hb = jnp.concatenate([h_ref[0], h_ref[1]], axis=1).astype(BF16)
    nblk = 2 * SSM_ROW
    for q in range(n * t // nblk):
        y = _dot(x, tz_ref[:, q * nblk:(q + 1) * nblk]) + _dot(hb, wcar_ref[0, :, q * nblk:(q + 1) * nblk])
        for mm in range(2):
            m = 2 * q + mm
            even, odd = _halves_to_rows(y[:rows, mm * SSM_ROW:(mm + 1) * SSM_ROW],
                                        y[rows:, mm * SSM_ROW:(mm + 1) * SSM_ROW], low)
            y_ref[2 * m] = even + d_ref[2 * m] * u_ref[2 * m]
            y_ref[2 * m + 1] = odd + d_ref[2 * m + 1] * u_ref[2 * m + 1]


def _ssm(ut, utc, kcat, wst, wcar, arow, d_skip, nb):
    n, t, p, g = SSM_GROUP, SSM_CHUNK, SSM_STATE, SSM_GROUPS
    rows = ut.shape[1]
    rows_c = utc.shape[1]
    nc, ncc = rows // nb, rows_c // nb
    d3 = jnp.broadcast_to(d_skip.astype(F32)[:, None, None], (SSM_WIDTH, 1, SSM_ROW))
    return pl.pallas_call(
        functools.partial(_ssm_kernel, nb=nb, nc=nc, ncc=ncc),
        grid=(g,),
        in_specs=[
            pl.BlockSpec((n, rows, SSM_ROW), lambda i: (i, 0, 0)),
            pl.BlockSpec((n, rows_c, SSM_ROW), lambda i: (i, 0, 0)),
            pl.BlockSpec((1, n * n, 2 * t), lambda i: (i, 0, 0)),
            pl.BlockSpec((1, n * t, 4 * p), lambda i: (i, 0, 0)),
            pl.BlockSpec((1, 4 * p, n * t), lambda i: (i, 0, 0)),
            pl.BlockSpec((1, 2, 2, 2 * p), lambda i: (i, 0, 0, 0)),
            pl.BlockSpec((n, 1, SSM_ROW), lambda i: (i, 0, 0)),
        ],
        out_specs=pl.BlockSpec((n, rows, SSM_ROW), lambda i: (i, 0, 0)),
        out_shape=jax.ShapeDtypeStruct((SSM_WIDTH, rows, SSM_ROW), F32),
        scratch_shapes=[
            pltpu.VMEM((n * t, n * t), BF16),
            pltpu.VMEM((2, 2, 2 * rows, 2 * p), F32),
            pltpu.VMEM((2, 2, 2 * rows_c, 2 * p), F32),
            pltpu.VMEM((2, 2 * rows, 2 * p), F32),
        ],
        compiler_params=_params("parallel"),
        name="ssm_toeplitz",
    )(ut, utc, kcat, wst, wcar, arow, d3)


def _glu_kernel(yt_ref, wglut_ref, s_ref, flat_ref):
    t = SSM_ROW
    nck = yt_ref.shape[1]

    flat_ref[...] = yt_ref[...].reshape(flat_ref.shape)
    for j in range(nck):
        g = jax.nn.gelu(flat_ref[pl.ds(j, SSM_WIDTH, stride=nck), :])
        z = _dot(wglut_ref[...], g.astype(BF16))
        gate = 0.5 * (1.0 + jnp.tanh(0.5 * z))
        s_ref[j * t:(j + 1) * t, :] = (g * gate).T.astype(BF16)


def _ssm_glu(yt, wglut, bs, ls):
    nck = 8
    rows = yt.shape[1]
    s = pl.pallas_call(
        _glu_kernel,
        grid=(rows // nck,),
        in_specs=[
            pl.BlockSpec((SSM_WIDTH, nck, SSM_ROW), lambda i: (0, i, 0)),
            _resident(wglut.shape),
        ],
        out_specs=pl.BlockSpec((nck * SSM_ROW, SSM_WIDTH), lambda i: (i, 0)),
        out_shape=jax.ShapeDtypeStruct((rows * SSM_ROW, SSM_WIDTH), BF16),
        scratch_shapes=[pltpu.VMEM((SSM_WIDTH * nck, SSM_ROW), F32)],
        compiler_params=_params("parallel"),
        name="ssm_glu",
    )(yt, wglut)
    return s.reshape(bs, ls, SSM_WIDTH)


def _fourier_kernel(x_ref, r_ref, mod_ref, gain_ref, lhs1_ref, twc_ref, tws_ref, lhs2_ref, cs_ref, z_ref,
                    h_ref, are_ref, aim_ref, y_ref):
    r, pitch = FFT_RADIX, FFT_PITCH
    halves = FOURIER_GROUP_WIDTH // LANES

    def slab(i):
        return pl.ds(pl.multiple_of(i * pitch, 8), r)

    def gather(ref, i):
        return jnp.concatenate([ref[hf, pl.ds(i, r, stride=pitch), :] for hf in range(halves)], axis=1)

    def put(ref, i, val):
        for hf in range(halves):
            ref[hf, slab(i), :] = val[:, hf * LANES:(hf + 1) * LANES]

    gain = gain_ref[1:2, :]
    shift, scale = mod_ref[0, 3:4, :], mod_ref[0, 4:5, :]

    def fill(n1, carry):
        rows = pl.ds(pl.multiple_of(n1 * r, r), r)
        rstd = jnp.concatenate([r_ref[0, rows, :]] * halves, axis=1)
        put(h_ref, n1, (x_ref[0, rows, :] * rstd * gain) * (1.0 + scale) + shift)
        return carry

    lax.fori_loop(0, r, fill, 0, unroll=4)

    def stage1(jg, carry):
        cols = [gather(h_ref, jg * 4 + j).astype(BF16) for j in range(4)]
        a = _dot(lhs1_ref[...], jnp.concatenate(cols, axis=0))
        ch, sh = a[:4 * r], a[4 * r:]
        trow = pl.ds(pl.multiple_of(jg * 4 * r, 4 * r), 4 * r)
        tc = jnp.concatenate([twc_ref[trow, :]] * halves, axis=1)
        ts = jnp.concatenate([tws_ref[trow, :]] * halves, axis=1)
        a_re = ch * tc - sh * ts
        a_im = sh * tc + ch * ts
        for j in range(4):
            put(are_ref, jg * 4 + j, a_re[j * r:(j + 1) * r])
            put(aim_ref, jg * 4 + j, a_im[j * r:(j + 1) * r])
        return carry

    lax.fori_loop(0, r // 4, stage1, 0, unroll=True)

    def stage2(kg, carry):
        parts = [gather(are_ref, kg * 4 + j).astype(BF16) for j in range(4)]
        parts += [gather(aim_ref, kg * 4 + j).astype(BF16) for j in range(4)]
        y = _dot(lhs2_ref[...], jnp.concatenate(parts, axis=0))
        rows = pl.ds(pl.multiple_of(kg * 4 * r, 4 * r), 4 * r)
        y_ref[rows, :] = jnp.concatenate([y[:4 * r], y[4 * r:]], axis=1).astype(BF16)
        return carry

    lax.fori_loop(0, r // 4, stage2, 0, unroll=True)

    def channels(kb, carry):
        rows = pl.ds(pl.multiple_of(kb * 8 * r, 8 * r), 8 * r)
        z = _dot(y_ref[rows, :], cs_ref[...])
        for j in range(8):
            put(h_ref, kb * 8 + j, z[j * r:(j + 1) * r])
        return carry

    lax.fori_loop(0, r // 8, channels, 0, unroll=True)

    def emit(k2, carry):
        z_ref[0, pl.ds(pl.multiple_of(k2 * r, r), r), :] = gather(h_ref, k2).astype(BF16)
        return carry

    lax.fori_loop(0, r, emit, 0, unroll=4)


def _fourier_tables():
    r = FFT_RADIX
    n = r * r
    k = jnp.arange(r, dtype=jnp.int32)
    ang = ((k[:, None] * k[None, :]) % r).astype(F32) * (2.0 * math.pi / r)
    wc = jnp.cos(ang) / 8.0
    ws = jnp.sin(ang) / 8.0
    eye4 = jnp.eye(4, dtype=F32)
    bc, bs = jnp.kron(eye4, wc), jnp.kron(eye4, ws)
    lhs1 = jnp.concatenate([bc, bs], axis=0).astype(BF16)
    lhs2 = jnp.concatenate([jnp.concatenate([bc, -bs], axis=1),
                            jnp.concatenate([-bs, -bc], axis=1)], axis=0).astype(BF16)
    tw = ((k[:, None] * k[None, :]) % n).astype(F32) * (2.0 * math.pi / n)
    twc = jnp.broadcast_to(jnp.cos(tw).reshape(n, 1), (n, LANES))
    tws = jnp.broadcast_to(jnp.sin(tw).reshape(n, 1), (n, LANES))
    m = jnp.arange(FOURIER_GROUP_WIDTH, dtype=jnp.int32)
    angc = ((m[:, None] * m[None, :]) % FOURIER_GROUP_WIDTH).astype(F32) * (2.0 * math.pi / FOURIER_GROUP_WIDTH)
    cs = jnp.concatenate([jnp.cos(angc), jnp.sin(angc)], axis=0) / 16.0
    return lhs1, lhs2, twc, tws, cs.astype(BF16)


def _fourier(x, rstd, mod, gains):
    bs, ls, d = x.shape
    r, gw = FFT_RADIX, FOURIER_GROUP_WIDTH
    assert ls == r * r
    lhs1, lhs2, twc, tws, cs = _fourier_tables()
    scratch = pltpu.VMEM((gw // LANES, r * FFT_PITCH, LANES), F32)
    return pl.pallas_call(
        _fourier_kernel,
        grid=(bs, d // gw),
        in_specs=[
            pl.BlockSpec((1, ls, gw), lambda b, g: (b, 0, g)),
            pl.BlockSpec((1, ls, LANES), lambda b, g: (b, 0, 0)),
            pl.BlockSpec((1, N_MOD, gw), lambda b, g: (b, 0, g)),
            pl.BlockSpec((gains.shape[0], gw), lambda b, g: (0, g)),
            _resident(lhs1.shape), _resident(twc.shape), _resident(tws.shape),
            _resident(lhs2.shape), _resident(cs.shape),
        ],
        out_specs=pl.BlockSpec((1, ls, gw), lambda b, g: (b, 0, g)),
        out_shape=jax.ShapeDtypeStruct((bs, ls, d), BF16),
        scratch_shapes=[scratch, scratch, scratch, pltpu.VMEM((ls, 2 * gw), BF16)],
        compiler_params=_params("parallel", "parallel"),
        name="fourier_mix",
    )(x, rstd, mod, gains, lhs1, twc, tws, lhs2, cs)


def _even_layer(x, s_ctx, mod, gains, ffn_w, layer, w_in, q_gain, k_gain, sink, ssm, d_skip, w_glu, w_out, e):
    bs, ls, d = x.shape
    ctx_index = bs
    x = _half_ffn(x, mod, 0, gains, 0, ffn_w, (layer, 0))
    n_ctx = s_ctx.shape[1]
    s_ctx = _half_ffn(s_ctx.reshape(1, bs * n_ctx, d), mod, 0, gains, 0, ffn_w, (layer, 0),
                      mod_index=ctx_index).reshape(bs, n_ctx, d)

    wqkv = w_in[:, :Q_WIDTH + 2 * KV_WIDTH]
    wut = w_in[:, Q_WIDTH + 2 * KV_WIDTH:].T.astype(BF16)
    bd = jnp.kron(jnp.eye(N_Q_HEADS, dtype=F32), jnp.ones((HEAD_DIM, HEAD_DIM), F32)).astype(BF16)
    qg = jnp.tile(q_gain.astype(F32), N_Q_HEADS)[None, :]
    kg = jnp.tile(k_gain.astype(F32), N_KV_HEADS)[None, :]
    cos, sin = _rope_tables(ls)
    n_ctx = s_ctx.shape[1]
    one = jnp.ones((n_ctx, LANES), F32)
    q, kk, vv, ut = _in_proj(x, mod, gains, wqkv, wut, bd, qg, kg, cos, sin)
    _, kkc, vvc, utc = _in_proj(s_ctx, mod, gains, wqkv, wut, bd, qg, kg, one, 0.0 * one, mod_index=ctx_index)

    attn = _attention(q, kk, vv, kkc, vvc, sink.astype(F32))

    t = SSM_ROW
    kcat, wst, wcar, arow = _ssm_prep(*ssm)
    yt = _ssm(ut, utc.reshape(SSM_WIDTH, bs * n_ctx // t, t), kcat, wst, wcar, arow, d_skip, bs)

    s = _ssm_glu(yt, w_glu.T.astype(BF16), bs, ls)
    mix = ((attn, w_out, _picked(w_out, (e,), (Q_WIDTH, d), (0, 0))),
           (s, w_out, _picked(w_out, (e,), (SSM_WIDTH, d), (1, 0))))
    return _half_ffn(x, mod, 6, gains, 2, ffn_w, (layer, 1), mix=mix)


def _odd_layer(x, mod, gains, ffn_w, layer, w_f, o):
    x, rstd = _half_ffn(x, mod, 0, gains, 0, ffn_w, (layer, 0), emit_rstd=True)
    z = _fourier(x, rstd, mod, gains)
    mix = ((z, w_f, _picked(w_f, (o,), w_f.shape[1:])),)
    return _half_ffn(x, mod, 6, gains, 2, ffn_w, (layer, 1), mix=mix)


def kernel(x, c, ctx, c_ctx, w_ada, b_ada, norm_gain, ffn_w1, ffn_w3, ffn_w2, w_in, q_gain, k_gain, sink_logit, ssm_lam_re, ssm_lam_im, ssm_log_dt, ssm_b_re, ssm_b_im, ssm_c_re, ssm_c_im, ssm_d, ssm_w_glu, w_out, fourier_w_out):
    bs, ls, d = x.shape
    depth = w_ada.shape[0]
    assert depth == 2 and d == D_MODEL
    rows = 16
    cond = jnp.concatenate([c, c_ctx[None, :], jnp.zeros((rows - bs - 1, d), F32)], axis=0)
    mods = _ada_mod(cond, w_ada, b_ada)[:, :bs + 1].reshape(depth, bs + 1, N_MOD, d)

    ffn_w = (ffn_w1, ffn_w3, ffn_w2)
    ssm = (ssm_lam_re[0], ssm_lam_im[0], ssm_log_dt[0], ssm_b_re[0], ssm_b_im[0], ssm_c_re[0], ssm_c_im[0])
    x = _even_layer(x, ctx, mods[0], norm_gain[0], ffn_w, 0, w_in[0], q_gain[0], k_gain[0],
                    sink_logit[0], ssm, ssm_d[0], ssm_w_glu[0], w_out, 0)
    return _odd_layer(x, mods[1], norm_gain[1], ffn_w, 1, fourier_w_out, 0)
```

```python
import functools
import math

import jax
import jax.numpy as jnp
from jax import lax
from jax.experimental import pallas as pl
from jax.experimental.pallas import tpu as pltpu

F32 = jnp.float32
BF16 = jnp.bfloat16

D_MODEL = 1024
N_MOD = 9
EPS = 1e-6
HEAD_DIM = 64
N_Q_HEADS = 8
N_KV_HEADS = 2
Q_WIDTH = N_Q_HEADS * HEAD_DIM
KV_WIDTH = N_KV_HEADS * HEAD_DIM
ATT_BLOCK = 128
ROPE_BASE = 10000.0
NEG_INF = -1e30
GRID_W = 64
SSM_WIDTH = 512
SSM_GROUP = 16
SSM_GROUPS = 32
SSM_STATE = 64
SSM_ROW = 128
SSM_CHUNK = 64
FOURIER_GROUPS = 4
FOURIER_GROUP_WIDTH = 256
FFT_RADIX = 64
FFT_PITCH = 72
FFN_CHUNK = 256
LANES = 128
VMEM_LIMIT = 56 * 1024 * 1024


def _dot(a, b):
    return jnp.dot(a, b, preferred_element_type=F32)


def _dot_w(a, w):
    return lax.dot_general(a, w, (((1,), (0,)), ((), ())), preferred_element_type=F32)


def _dot_nt(a, b):
    return lax.dot_general(a, b, (((1,), (1,)), ((), ())), preferred_element_type=F32)


def _split(a):
    hi = a.astype(BF16)
    lo = (a - hi.astype(F32)).astype(BF16)
    return hi, lo


def _dot3(a, b):
    a_hi, a_lo = _split(a)
    b_hi, b_lo = _split(b)
    return _dot(a_hi, b_hi) + _dot(a_lo, b_hi) + _dot(a_hi, b_lo)


def _norm_mod(x, gain, shift, scale):
    ms = jnp.mean(x * x, axis=-1, keepdims=True)
    return (x * lax.rsqrt(ms + EPS)) * (gain * (1.0 + scale)) + shift


def _params(*sem):
    return pltpu.CompilerParams(dimension_semantics=sem, vmem_limit_bytes=VMEM_LIMIT)


def _resident(shape):
    nd = len(shape)
    return pl.BlockSpec(shape, lambda *_: (0,) * nd, pipeline_mode=pl.Buffered(1))


def _ada_kernel(cond_ref, w_ref, b_ref, o_ref):
    a = cond_ref[...]
    a = a * jax.nn.sigmoid(a)
    o_ref[0] = _dot3(a, w_ref[0]) + b_ref[0]


def _ada_mod(cond, w_ada, b_ada):
    depth, d, nd = w_ada.shape
    rows = cond.shape[0]
    tn = nd // 8
    return pl.pallas_call(
        _ada_kernel,
        grid=(depth, nd // tn),
        in_specs=[
            pl.BlockSpec((rows, d), lambda l, j: (0, 0)),
            pl.BlockSpec((1, d, tn), lambda l, j: (l, 0, j)),
            pl.BlockSpec((1, 1, tn), lambda l, j: (l, 0, j)),
        ],
        out_specs=pl.BlockSpec((1, rows, tn), lambda l, j: (l, 0, j)),
        out_shape=jax.ShapeDtypeStruct((depth, rows, nd), F32),
        compiler_params=_params("parallel", "parallel"),
        name="ada_mod",
    )(cond, w_ada, b_ada.reshape(depth, 1, nd))


def _ffn_kernel(*refs, row0, grow, n_mix, emit_rstd):
    x_ref, mod_ref, gain_ref, w1_ref, w3_ref, w2_ref = refs[:6]
    mix_refs = refs[6:6 + 2 * n_mix]
    out_refs = refs[6 + 2 * n_mix:-1]
    acc_ref = refs[-1]
    o_ref = out_refs[0]
    x = x_ref[0]
    if n_mix:
        y = _dot_w(mix_refs[0][0], mix_refs[1][...])
        for k in range(1, n_mix):
            y = y + _dot_w(mix_refs[2 * k][0], mix_refs[2 * k + 1][...])
        x = x + mod_ref[0, 5:6, :] * y
        o_ref[0] = x
    shift = mod_ref[0, row0:row0 + 1, :]
    scale = mod_ref[0, row0 + 1:row0 + 2, :]
    gate = mod_ref[0, row0 + 2:row0 + 3, :]
    h = _norm_mod(x, gain_ref[grow:grow + 1, :], shift, scale)
    fc = FFN_CHUNK
    for c in range(w1_ref.shape[1] // fc):
        a = _dot_w(h, w1_ref[:, c * fc:(c + 1) * fc])
        b = _dot_w(h, w3_ref[:, c * fc:(c + 1) * fc])
        g = a * jax.nn.sigmoid(a) * b
        part = _dot_w(g, w2_ref[c * fc:(c + 1) * fc, :])
        if c == 0:
            acc_ref[...] = part
        else:
            acc_ref[...] += part
    base = o_ref[0] if n_mix else x_ref[0]
    o = base + (0.5 * gate) * acc_ref[...]
    o_ref[0] = o
    if emit_rstd:
        rstd = lax.rsqrt(jnp.mean(o * o, axis=-1, keepdims=True) + EPS)
        out_refs[1][0] = jnp.broadcast_to(rstd, out_refs[1].shape[1:])


def _picked(arr, lead, shape, start=None):
    start = start or (0,) * len(shape)
    return pl.BlockSpec((None,) * len(lead) + tuple(shape), lambda *_: tuple(lead) + tuple(start),
                        pipeline_mode=pl.Buffered(1))


def _half_ffn(s, mod, row0, gains, grow, ffn_w, lk, mod_index=None, mix=(), emit_rstd=False, tm=512):
    bs, ls, d = s.shape
    tm = min(tm, ls)
    if mod_index is None:
        mod_map = lambda b, i: (b, 0, 0)
    else:
        mod_map = lambda b, i: (mod_index, 0, 0)
    tok = lambda w: pl.BlockSpec((1, tm, w), lambda b, i: (b, i, 0))
    in_specs = [tok(d), pl.BlockSpec((1, N_MOD, d), mod_map), _resident(gains.shape)]
    in_specs += [_picked(w, lk, w.shape[2:]) for w in ffn_w]
    args = [s, mod, gains, *ffn_w]
    for z, w, w_spec in mix:
        in_specs += [tok(z.shape[2]), w_spec]
        args += [z, w]
    out_specs = [tok(d)]
    out_shape = [jax.ShapeDtypeStruct(s.shape, F32)]
    if emit_rstd:
        out_specs.append(tok(LANES))
        out_shape.append(jax.ShapeDtypeStruct((bs, ls, LANES), F32))
    outs = pl.pallas_call(
        functools.partial(_ffn_kernel, row0=row0, grow=grow, n_mix=len(mix), emit_rstd=emit_rstd),
        grid=(bs, ls // tm),
        in_specs=in_specs,
        out_specs=out_specs,
        out_shape=out_shape,
        scratch_shapes=[pltpu.VMEM((tm, d), F32)],
        compiler_params=_params("parallel", "parallel"),
        name="half_ffn",
    )(*args)
    return outs if emit_rstd else outs[0]


def _head_norm(t, bd, gain):
    ms = _dot((t * t).astype(BF16), bd) * (1.0 / HEAD_DIM)
    return t * lax.rsqrt(ms + EPS) * gain


def _rope(t, cos, sin):
    w = t.shape[1]
    half = HEAD_DIM // 2
    lane = lax.broadcasted_iota(jnp.int32, t.shape, 1)
    first = (lane % HEAD_DIM) < half
    rot = jnp.where(first, pltpu.roll(t, w - half, 1), pltpu.roll(t, half, 1))
    reps = w // LANES
    c = jnp.concatenate([cos] * reps, axis=1) if reps > 1 else cos
    s = jnp.concatenate([sin] * reps, axis=1) if reps > 1 else sin
    return t * c + rot * s


def _dup_heads(t):
    lane = lax.broadcasted_iota(jnp.int32, t.shape, 1)
    low = lane < HEAD_DIM
    r = pltpu.roll(t, HEAD_DIM, 1)
    return jnp.concatenate([jnp.where(low, t, r), jnp.where(low, r, t)], axis=1)


def _inproj_kernel(x_ref, mod_ref, gain_ref, wqkv_ref, wut_ref, bd_ref, qg_ref, kg_ref, cos_ref, sin_ref,
                   q_ref, kk_ref, vv_ref, ut_ref, *flat):
    x = x_ref[0]
    h = _norm_mod(x, gain_ref[1:2, :], mod_ref[0, 3:4, :], mod_ref[0, 4:5, :]).astype(BF16)
    qkv = _dot_w(h, wqkv_ref[...])
    q = qkv[:, :Q_WIDTH]
    k = qkv[:, Q_WIDTH:Q_WIDTH + KV_WIDTH]
    v = qkv[:, Q_WIDTH + KV_WIDTH:]
    cos = cos_ref[...]
    sin = sin_ref[...]
    q = _rope(_head_norm(q, bd_ref[...], qg_ref[...]), cos, sin)
    k = _rope(_head_norm(k, bd_ref[:KV_WIDTH, :KV_WIDTH], kg_ref[...]), cos, sin)
    q_ref[0] = (q * (HEAD_DIM ** -0.5)).astype(BF16)
    kk_ref[0] = _dup_heads(k).astype(BF16)
    vv_ref[0] = _dup_heads(v).astype(BF16)
    ut = _dot_nt(wut_ref[...], h)
    if not flat:
        ut_ref[...] = ut
        return
    flat_ref, = flat
    nr = ut_ref.shape[1]
    for j in range(nr):
        flat_ref[pl.ds(j, SSM_WIDTH, stride=nr), :] = ut[:, j * SSM_ROW:(j + 1) * SSM_ROW]

    ut_ref[...] = flat_ref[...].reshape(ut_ref.shape)


def _in_proj(s, mod, gains, wqkv, wut, bd, qg, kg, cos, sin, mod_index=None, tm=1024):
    bs, ls, d = s.shape
    tm = min(tm, ls)
    nt = ls // tm
    nr = tm // SSM_ROW
    by_rows = nr % 8 == 0
    if by_rows:
        ut_spec = pl.BlockSpec((SSM_WIDTH, nr, SSM_ROW), lambda b, i: (0, b * nt + i, 0))
        ut_shape = jax.ShapeDtypeStruct((SSM_WIDTH, bs * ls // SSM_ROW, SSM_ROW), F32)
        scratch = [pltpu.VMEM((SSM_WIDTH * nr, SSM_ROW), F32)]
    else:
        ut_spec = pl.BlockSpec((SSM_WIDTH, tm), lambda b, i: (0, b * nt + i))
        ut_shape = jax.ShapeDtypeStruct((SSM_WIDTH, bs * ls), F32)
        scratch = []
    if mod_index is None:
        mod_map = lambda b, i: (b, 0, 0)
    else:
        mod_map = lambda b, i: (mod_index, 0, 0)
    tok = lambda w: pl.BlockSpec((1, tm, w), lambda b, i: (b, i, 0))
    return pl.pallas_call(
        _inproj_kernel,
        grid=(bs, nt),
        in_specs=[
            tok(d),
            pl.BlockSpec((1, N_MOD, d), mod_map),
            _resident(gains.shape),
            _resident(wqkv.shape),
            _resident(wut.shape),
            _resident(bd.shape),
            _resident(qg.shape),
            _resident(kg.shape),
            pl.BlockSpec((tm, LANES), lambda b, i: (i, 0)),
            pl.BlockSpec((tm, LANES), lambda b, i: (i, 0)),
        ],
        out_specs=[tok(Q_WIDTH), tok(2 * KV_WIDTH), tok(2 * KV_WIDTH), ut_spec],
        out_shape=[
            jax.ShapeDtypeStruct((bs, ls, Q_WIDTH), BF16),
            jax.ShapeDtypeStruct((bs, ls, 2 * KV_WIDTH), BF16),
            jax.ShapeDtypeStruct((bs, ls, 2 * KV_WIDTH), BF16),
            ut_shape,
        ],
        scratch_shapes=scratch,
        compiler_params=_params("parallel", "parallel"),
        name="in_proj",
    )(s, mod, gains, wqkv, wut, bd, qg, kg, cos, sin)


def _rope_tables(n_lat):
    rows = n_lat // GRID_W
    row = jnp.repeat(jnp.arange(rows, dtype=F32), GRID_W)
    col = jnp.tile(jnp.arange(GRID_W, dtype=F32), rows)
    n_freq = HEAD_DIM // 4
    inv_freq = 1.0 / (ROPE_BASE ** (jnp.arange(n_freq, dtype=F32) / n_freq))
    ang = jnp.concatenate([row[:, None] * inv_freq, col[:, None] * inv_freq], axis=-1)
    cos, sin = jnp.cos(ang), jnp.sin(ang)
    cos_t = jnp.tile(cos, (1, LANES // (HEAD_DIM // 2)))
    sin_t = jnp.tile(jnp.concatenate([-sin, sin], axis=-1), (1, LANES // HEAD_DIM))
    return cos_t, sin_t


def _attn_kernel(sink_ref, q_ref, kk_ref, vv_ref, kkc_ref, vvc_ref, o_ref):
    blk = ATT_BLOCK
    per_step = q_ref.shape[1] // blk
    nb = pl.num_programs(1) * per_step
    for k in range(per_step):
        o_ref[0, k * blk:(k + 1) * blk, :] = _attn_block(
            pl.program_id(1) * per_step + k, nb, q_ref[0, k * blk:(k + 1) * blk, :],
            sink_ref, kk_ref, vv_ref, kkc_ref, vvc_ref)


def _attn_block(i, nb, q, sink_ref, kk_ref, vv_ref, kkc_ref, vvc_ref):
    blk = ATT_BLOCK
    lane = lax.broadcasted_iota(jnp.int32, (blk, LANES), 1)
    low = lane < HEAD_DIM
    rel = lane - lax.broadcasted_iota(jnp.int32, (blk, LANES), 0)
    ok_prev = rel >= jnp.where(i > 0, 0, 2 * blk)
    ok_next = rel <= jnp.where(i < nb - 1, 0, -2 * blk)
    j_prev = jnp.maximum(i - 1, 0)
    j_next = jnp.minimum(i + 1, nb - 1)
    zero = jnp.zeros_like(q[:, :LANES])
    cols = []
    for h in range(N_KV_HEADS):
        hs = slice(h * LANES, (h + 1) * LANES)

        def rows(ref, j):
            return ref[0, pl.ds(pl.multiple_of(j * blk, blk), blk), hs]

        keys = jnp.concatenate([rows(kk_ref, j_prev), rows(kk_ref, i), rows(kk_ref, j_next), kkc_ref[0, :, hs]], axis=0)
        vals = jnp.concatenate([rows(vv_ref, j_prev), rows(vv_ref, i), rows(vv_ref, j_next), vvc_ref[0, :, hs]], axis=0)
        qs = []
        for col in (2 * h, 2 * h + 1):
            qc = q[:, col * LANES:(col + 1) * LANES]
            qs.append(jnp.where(low, qc, zero))
            qs.append(jnp.where(low, zero, qc))
        s_all = _dot_nt(jnp.concatenate(qs, axis=0), keys)
        low_k = lax.broadcasted_iota(jnp.int32, vals.shape, 1) < HEAD_DIM
        v_ones = (jnp.where(low_k, vals, jnp.ones_like(vals)), jnp.where(low_k, jnp.ones_like(vals), vals))
        ps, tail = [], []
        for g in range(4):
            s = s_all[g * blk:(g + 1) * blk]
            parts = [jnp.where(ok_prev, s[:, :blk], NEG_INF), s[:, blk:2 * blk],
                     jnp.where(ok_next, s[:, 2 * blk:3 * blk], NEG_INF)]
            parts += [s[:, j * blk:(j + 1) * blk] for j in range(3, s.shape[1] // blk)]
            peak = parts[0]
            for part in parts[1:]:
                peak = jnp.maximum(peak, part)
            sink = sink_ref[4 * h + g]
            m = jnp.maximum(jnp.max(peak, axis=-1, keepdims=True), sink)
            ps.append(jnp.exp(jnp.concatenate([part - m for part in parts], axis=1).astype(BF16)))
            tail.append(jnp.exp(sink - m))
        og = []
        for par in range(2):
            o = _dot(jnp.concatenate([ps[par], ps[2 + par]], axis=0), v_ones[par])
            for k in range(2):
                ok = o[k * blk:(k + 1) * blk]
                den = (ok[:, HEAD_DIM:HEAD_DIM + 1] if par == 0 else ok[:, 0:1]) + tail[2 * k + par]
                og.append(ok * (1.0 / den))
        cols.append(jnp.where(low, og[0], og[2]))
        cols.append(jnp.where(low, og[1], og[3]))
    return jnp.concatenate(cols, axis=1).astype(BF16)


def _attention(q, kk, vv, kkc, vvc, sink):
    bs, ls, _ = q.shape
    tq = 8 * ATT_BLOCK
    n_ctx = kkc.shape[1]
    full = lambda n: pl.BlockSpec((1, n, 2 * KV_WIDTH), lambda b, i: (b, 0, 0))
    return pl.pallas_call(
        _attn_kernel,
        grid=(bs, ls // tq),
        in_specs=[
            pl.BlockSpec(memory_space=pltpu.SMEM),
            pl.BlockSpec((1, tq, Q_WIDTH), lambda b, i: (b, i, 0)),
            full(ls), full(ls), full(n_ctx), full(n_ctx),
        ],
        out_specs=pl.BlockSpec((1, tq, Q_WIDTH), lambda b, i: (b, i, 0)),
        out_shape=jax.ShapeDtypeStruct((bs, ls, Q_WIDTH), BF16),
        compiler_params=_params("parallel", "arbitrary"),
        name="window_attention",
    )(sink, q, kk, vv, kkc, vvc)


def _cpow(lr, li, dt, tau):
    mag = jnp.exp(lr * dt * tau)
    ang = li * dt * tau
    return mag * jnp.cos(ang), mag * jnp.sin(ang)


def _ssm_prep_kernel(lrr_ref, lir_ref, lrc_ref, lic_ref, ldt_ref, btr_ref, bti_ref, btrr_ref, btri_ref,
                     crr_ref, cri_ref, ctr_ref, cti_ref, kcat_ref, wst_ref, wcar_ref, arow_ref):
    p = SSM_STATE
    t = SSM_CHUNK
    lane_t = lax.broadcasted_iota(jnp.int32, (1, t), 1).astype(F32)
    sub_t = lax.broadcasted_iota(jnp.int32, (t, 1), 0).astype(F32)
    kparts = []
    for d in range(2):
        dt = jnp.exp(ldt_ref[0, d])
        lr, li = lrr_ref[0, d], lir_ref[0, d]
        lrc, lic = lrc_ref[0, d], lic_ref[0, d]
        ar, ai = _cpow(lr, li, dt, 1.0)
        nr = ar - 1.0
        den = lr * lr + li * li
        fr = (nr * lr + ai * li) / den
        fi = (ai * lr - nr * li) / den
        bt_r, bt_i = btr_ref[0, d], bti_ref[0, d]
        bbr = fr * bt_r - fi * bt_i
        bbi = fr * bt_i + fi * bt_r
        rep_r, rep_i = btrr_ref[0, d], btri_ref[0, d]
        bbr_rep = fr * rep_r - fi * rep_i
        bbi_rep = fr * rep_i + fi * rep_r
        c_r, c_i = crr_ref[0, d], cri_ref[0, d]
        m1 = c_r * bbr_rep - c_i * bbi_rep
        m2 = c_r * bbi_rep + c_i * bbr_rep
        tau = lane_t if d == 0 else (float(t) - lane_t)
        e_r, e_i = _cpow(lrc, lic, dt, tau)
        kmat = _dot3(m1, e_r) - _dot3(m2, e_i)
        if d == 0:
            kparts.append(kmat)
        else:
            lag0 = jnp.sum(m1, axis=1, keepdims=True)
            first = lax.broadcasted_iota(jnp.int32, (1, t), 1) == 0
            kparts[0] = kparts[0] + jnp.where(first, lag0, 0.0)
            kparts.insert(0, kmat)
        tau_s = (float(t - 1) - sub_t) if d == 0 else sub_t
        et_r, et_i = _cpow(lr, li, dt, tau_s)
        for i in range(SSM_GROUP):
            re = et_r * bbr[i:i + 1] - et_i * bbi[i:i + 1]
            im = et_r * bbi[i:i + 1] + et_i * bbr[i:i + 1]
            wst_ref[0, i * t:(i + 1) * t, d * 2 * p:(d + 1) * 2 * p] = jnp.concatenate([re, im], axis=1).astype(BF16)
        if d == 0:
            ec_r, ec_i = _cpow(lrc, lic, dt, lane_t + 1.0)
        else:
            ec_r, ec_i = e_r, e_i
        ct_r, ct_i = ctr_ref[0, d], cti_ref[0, d]
        car_re, car_im = [], []
        for jo in range(SSM_GROUP):
            cr = ct_r[:, jo:jo + 1]
            ci = ct_i[:, jo:jo + 1]
            car_re.append(cr * ec_r - ci * ec_i)
            car_im.append(-(cr * ec_i + ci * ec_r))
        wcar_ref[0, d * 2 * p:d * 2 * p + p, :] = jnp.concatenate(car_re, axis=1).astype(BF16)
        wcar_ref[0, d * 2 * p + p:(d + 1) * 2 * p, :] = jnp.concatenate(car_im, axis=1).astype(BF16)
        at_r, at_i = _cpow(lr, li, dt, float(t))
        arow_ref[0, d, 0:1, :] = jnp.concatenate([at_r, at_r], axis=1)
        arow_ref[0, d, 1:2, :] = jnp.concatenate([-at_i, at_i], axis=1)
    kcat_ref[0] = jnp.concatenate(kparts, axis=1)


def _ssm_prep(lam_re, lam_im, log_dt, b_re, b_im, c_re, c_im):
    g, p, n, t = SSM_GROUPS, SSM_STATE, SSM_GROUP, SSM_CHUNK
    gd = lambda a: jnp.swapaxes(a, 0, 1)
    lrr = gd(lam_re)[:, :, None, :]
    lir = gd(lam_im)[:, :, None, :]
    lrc = gd(lam_re)[:, :, :, None]
    lic = gd(lam_im)[:, :, :, None]
    ldt = gd(log_dt)[:, :, None, None]
    bt_r = jnp.swapaxes(gd(b_re), 2, 3)
    bt_i = jnp.swapaxes(gd(b_im), 2, 3)
    btr_r = jnp.tile(bt_r, (1, 1, n, 1))
    btr_i = jnp.tile(bt_i, (1, 1, n, 1))
    cr_r = jnp.repeat(gd(c_re), n, axis=2)
    cr_i = jnp.repeat(gd(c_im), n, axis=2)
    ct_r = jnp.swapaxes(gd(c_re), 2, 3)
    ct_i = jnp.swapaxes(gd(c_im), 2, 3)
    args = (lrr, lir, lrc, lic, ldt, bt_r, bt_i, btr_r, btr_i, cr_r, cr_i, ct_r, ct_i)
    spec = lambda a: pl.BlockSpec((1,) + a.shape[1:], lambda i: (i,) + (0,) * (a.ndim - 1))
    return pl.pallas_call(
        _ssm_prep_kernel,
        grid=(g,),
        in_specs=[spec(a) for a in args],
        out_specs=[
            pl.BlockSpec((1, n * n, 2 * t), lambda i: (i, 0, 0)),
            pl.BlockSpec((1, n * t, 4 * p), lambda i: (i, 0, 0)),
            pl.BlockSpec((1, 4 * p, n * t), lambda i: (i, 0, 0)),
            pl.BlockSpec((1, 2, 2, 2 * p), lambda i: (i, 0, 0, 0)),
        ],
        out_shape=[
            jax.ShapeDtypeStruct((g, n * n, 2 * t), F32),
            jax.ShapeDtypeStruct((g, n * t, 4 * p), BF16),
            jax.ShapeDtypeStruct((g, 4 * p, n * t), BF16),
            jax.ShapeDtypeStruct((g, 2, 2, 2 * p), F32),
        ],
        compiler_params=_params("parallel"),
        name="ssm_prep",
    )(*args)


def _halves_to_rows(a, b, low):
    return jnp.where(low, a, pltpu.roll(b, SSM_CHUNK, 1)), jnp.where(low, pltpu.roll(a, SSM_CHUNK, 1), b)


def _ssm_kernel(u_ref, uc_ref, kcat_ref, wst_ref, wcar_ref, arow_ref, d_ref, y_ref,
                tz_ref, s_ref, sc_ref, h_ref, *, nb, nc, ncc):
    n, t, p = SSM_GROUP, SSM_CHUNK, SSM_STATE
    rows, rows_c = u_ref.shape[1], uc_ref.shape[1]
    low = lax.broadcasted_iota(jnp.int32, (1, SSM_ROW), 1) < t

    def build(i, carry):
        for m in range(n // 2):
            even = jnp.broadcast_to(kcat_ref[0, pl.ds(2 * m * n + i, 1), :], (t, SSM_ROW))
            odd = jnp.broadcast_to(kcat_ref[0, pl.ds((2 * m + 1) * n + i, 1), :], (t, SSM_ROW))
            even = pltpu.roll(even, t, 1, stride=1, stride_axis=0)
            odd = pltpu.roll(odd, 0, 1, stride=1, stride_axis=0)
            tz_ref[pl.ds(pl.multiple_of(i * t, t), t), m * SSM_ROW:(m + 1) * SSM_ROW] = jnp.where(low, even, odd).astype(BF16)
        return carry

    lax.fori_loop(0, n, build, 0, unroll=True)

    def chunk_rows(ref):
        first, second = [], []
        for m in range(n // 2):
            a, b = _halves_to_rows(ref[2 * m], ref[2 * m + 1], low)
            first.append(a.astype(BF16))
            second.append(b.astype(BF16))
        return jnp.concatenate([jnp.concatenate(first, axis=1), jnp.concatenate(second, axis=1)], axis=0)

    x = chunk_rows(u_ref)
    xc = chunk_rows(uc_ref)
    s_all = _dot(x, wst_ref[0])
    sc_all = _dot(xc, wst_ref[0])
    for d in range(2):
        s_ref[d, 0] = s_all[:, d * 2 * p:(d + 1) * 2 * p]
        s_ref[d, 1] = pltpu.roll(s_all[:, d * 2 * p:(d + 1) * 2 * p], p, 1)
        sc_ref[d, 0] = sc_all[:, d * 2 * p:(d + 1) * 2 * p]
        sc_ref[d, 1] = pltpu.roll(sc_all[:, d * 2 * p:(d + 1) * 2 * p], p, 1)

    coef = [(arow_ref[0, d, 0:1, :], arow_ref[0, d, 1:2, :]) for d in range(2)]

    def advance(d, h, hs, s, ss):
        a1, a2 = coef[d]
        return a1 * h + a2 * hs + s, a1 * hs - a2 * h + ss

    state = []
    for d in range(2):
        h = hs = jnp.zeros((nb, 2 * p), F32)
        order = range(2 * ncc) if d == 0 else reversed(range(2 * ncc))
        for chunk in order:
            sel = pl.ds((chunk % 2) * rows_c + chunk // 2, nb, stride=ncc)
            h, hs = advance(d, h, hs, sc_ref[d, 0, sel, :], sc_ref[d, 1, sel, :])
        state += [h, hs]

    def scan(k, carry):
        out = []
        for d in range(2):
            h, hs = carry[2 * d], carry[2 * d + 1]
            row = k if d == 0 else nc - 1 - k
            for half in ((0, 1) if d == 0 else (1, 0)):
                sel = pl.ds(half * rows + row, nb, stride=nc)
                h_ref[d, sel, :] = h
                h, hs = advance(d, h, hs, s_ref[d, 0, sel, :], s_ref[d, 1, sel, :])
            out += [h, hs]
        return tuple(out)

    lax.fori_loop(0, nc, scan, tuple(state), unroll=True)

    hb = jnp.concatenate([h_ref[0], h_ref[1]], axis=1).astype(BF16)
    nblk = 2 * SSM_ROW
    for q in range(n * t // nblk):
        y = _dot(x, tz_ref[:, q * nblk:(q + 1) * nblk]) + _dot(hb, wcar_ref[0, :, q * nblk:(q + 1) * nblk])
        for mm in range(2):
            m = 2 * q + mm
            even, odd = _halves_to_rows(y[:rows, mm * SSM_ROW:(mm + 1) * SSM_ROW],
                                        y[rows:, mm * SSM_ROW:(mm + 1) * SSM_ROW], low)
            y_ref[2 * m] = even + d_ref[2 * m] * u_ref[2 * m]
            y_ref[2 * m + 1] = odd + d_ref[2 * m + 1] * u_ref[2 * m + 1]


def _ssm(ut, utc, kcat, wst, wcar, arow, d_skip, nb):
    n, t, p, g = SSM_GROUP, SSM_CHUNK, SSM_STATE, SSM_GROUPS
    rows = ut.shape[1]
    rows_c = utc.shape[1]
    nc, ncc = rows // nb, rows_c // nb
    d3 = jnp.broadcast_to(d_skip.astype(F32)[:, None, None], (SSM_WIDTH, 1, SSM_ROW))
    return pl.pallas_call(
        functools.partial(_ssm_kernel, nb=nb, nc=nc, ncc=ncc),
        grid=(g,),
        in_specs=[
            pl.BlockSpec((n, rows, SSM_ROW), lambda i: (i, 0, 0)),
            pl.BlockSpec((n, rows_c, SSM_ROW), lambda i: (i, 0, 0)),
            pl.BlockSpec((1, n * n, 2 * t), lambda i: (i, 0, 0)),
            pl.BlockSpec((1, n * t, 4 * p), lambda i: (i, 0, 0)),
            pl.BlockSpec((1, 4 * p, n * t), lambda i: (i, 0, 0)),
            pl.BlockSpec((1, 2, 2, 2 * p), lambda i: (i, 0, 0, 0)),
            pl.BlockSpec((n, 1, SSM_ROW), lambda i: (i, 0, 0)),
        ],
        out_specs=pl.BlockSpec((n, rows, SSM_ROW), lambda i: (i, 0, 0)),
        out_shape=jax.ShapeDtypeStruct((SSM_WIDTH, rows, SSM_ROW), F32),
        scratch_shapes=[
            pltpu.VMEM((n * t, n * t), BF16),
            pltpu.VMEM((2, 2, 2 * rows, 2 * p), F32),
            pltpu.VMEM((2, 2, 2 * rows_c, 2 * p), F32),
            pltpu.VMEM((2, 2 * rows, 2 * p), F32),
        ],
        compiler_params=_params("parallel"),
        name="ssm_toeplitz",
    )(ut, utc, kcat, wst, wcar, arow, d3)


def _glu_kernel(yt_ref, wglut_ref, s_ref, flat_ref):
    t = SSM_ROW
    nck = yt_ref.shape[1]

    flat_ref[...] = yt_ref[...].reshape(flat_ref.shape)
    for j in range(nck):
        g = jax.nn.gelu(flat_ref[pl.ds(j, SSM_WIDTH, stride=nck), :])
        z = _dot(wglut_ref[...], g.astype(BF16))
        gate = 0.5 * (1.0 + jnp.tanh(0.5 * z))
        s_ref[j * t:(j + 1) * t, :] = (g * gate).T.astype(BF16)


def _ssm_glu(yt, wglut, bs, ls):
    nck = 8
    rows = yt.shape[1]
    s = pl.pallas_call(
        _glu_kernel,
        grid=(rows // nck,),
        in_specs=[
            pl.BlockSpec((SSM_WIDTH, nck, SSM_ROW), lambda i: (0, i, 0)),
            _resident(wglut.shape),
        ],
        out_specs=pl.BlockSpec((nck * SSM_ROW, SSM_WIDTH), lambda i: (i, 0)),
        out_shape=jax.ShapeDtypeStruct((rows * SSM_ROW, SSM_WIDTH), BF16),
        scratch_shapes=[pltpu.VMEM((SSM_WIDTH * nck, SSM_ROW), F32)],
        compiler_params=_params("parallel"),
        name="ssm_glu",
    )(yt, wglut)
    return s.reshape(bs, ls, SSM_WIDTH)


def _fourier_kernel(x_ref, r_ref, mod_ref, gain_ref, lhs1_ref, twc_ref, tws_ref, lhs2_ref, cs_ref, z_ref,
                    h_ref, are_ref, aim_ref, y_ref):
    r, pitch = FFT_RADIX, FFT_PITCH
    halves = FOURIER_GROUP_WIDTH // LANES

    def slab(i):
        return pl.ds(pl.multiple_of(i * pitch, 8), r)

    def gather(ref, i):
        return jnp.concatenate([ref[hf, pl.ds(i, r, stride=pitch), :] for hf in range(halves)], axis=1)

    def put(ref, i, val):
        for hf in range(halves):
            ref[hf, slab(i), :] = val[:, hf * LANES:(hf + 1) * LANES]

    gain = gain_ref[1:2, :]
    shift, scale = mod_ref[0, 3:4, :], mod_ref[0, 4:5, :]

    def fill(n1, carry):
        rows = pl.ds(pl.multiple_of(n1 * r, r), r)
        rstd = jnp.concatenate([r_ref[0, rows, :]] * halves, axis=1)
        put(h_ref, n1, (x_ref[0, rows, :] * rstd * gain) * (1.0 + scale) + shift)
        return carry

    lax.fori_loop(0, r, fill, 0, unroll=4)

    def stage1(jg, carry):
        cols = [gather(h_ref, jg * 4 + j).astype(BF16) for j in range(4)]
        a = _dot(lhs1_ref[...], jnp.concatenate(cols, axis=0))
        ch, sh = a[:4 * r], a[4 * r:]
        trow = pl.ds(pl.multiple_of(jg * 4 * r, 4 * r), 4 * r)
        tc = jnp.concatenate([twc_ref[trow, :]] * halves, axis=1)
        ts = jnp.concatenate([tws_ref[trow, :]] * halves, axis=1)
        a_re = ch * tc - sh * ts
        a_im = sh * tc + ch * ts
        for j in range(4):
            put(are_ref, jg * 4 + j, a_re[j * r:(j + 1) * r])
            put(aim_ref, jg * 4 + j, a_im[j * r:(j + 1) * r])
        return carry

    lax.fori_loop(0, r // 4, stage1, 0, unroll=True)

    def stage2(kg, carry):
        for j in range(4):
            k1 = kg * 4 + j
            st = jnp.concatenate([gather(are_ref, k1), gather(aim_ref, k1)], axis=0).astype(BF16)
            y = _dot(lhs2_ref[...], st)
            rows = pl.ds(pl.multiple_of(k1 * r, r), r)
            y_ref[rows, :] = jnp.concatenate([y[:r], y[r:]], axis=1).astype(BF16)
        return carry

    lax.fori_loop(0, r // 4, stage2, 0, unroll=True)

    def channels(kb, carry):
        rows = pl.ds(pl.multiple_of(kb * 8 * r, 8 * r), 8 * r)
        z = _dot(y_ref[rows, :], cs_ref[...])
        for j in range(8):
            put(h_ref, kb * 8 + j, z[j * r:(j + 1) * r])
        return carry

    lax.fori_loop(0, r // 8, channels, 0, unroll=True)

    def emit(k2, carry):
        z_ref[0, pl.ds(pl.multiple_of(k2 * r, r), r), :] = gather(h_ref, k2).astype(BF16)
        return carry

    lax.fori_loop(0, r, emit, 0, unroll=4)


def _fourier_tables():
    r = FFT_RADIX
    n = r * r
    k = jnp.arange(r, dtype=jnp.int32)
    ang = ((k[:, None] * k[None, :]) % r).astype(F32) * (2.0 * math.pi / r)
    wc = jnp.cos(ang) / 8.0
    ws = jnp.sin(ang) / 8.0
    eye4 = jnp.eye(4, dtype=F32)
    bc, bs = jnp.kron(eye4, wc), jnp.kron(eye4, ws)
    lhs1 = jnp.concatenate([bc, bs], axis=0).astype(BF16)
    lhs2 = jnp.concatenate([jnp.concatenate([wc, -ws], axis=1),
                            jnp.concatenate([-ws, -wc], axis=1)], axis=0).astype(BF16)
    tw = ((k[:, None] * k[None, :]) % n).astype(F32) * (2.0 * math.pi / n)
    twc = jnp.broadcast_to(jnp.cos(tw).reshape(n, 1), (n, LANES))
    tws = jnp.broadcast_to(jnp.sin(tw).reshape(n, 1), (n, LANES))
    m = jnp.arange(FOURIER_GROUP_WIDTH, dtype=jnp.int32)
    angc = ((m[:, None] * m[None, :]) % FOURIER_GROUP_WIDTH).astype(F32) * (2.0 * math.pi / FOURIER_GROUP_WIDTH)
    cs = jnp.concatenate([jnp.cos(angc), jnp.sin(angc)], axis=0) / 16.0
    return lhs1, lhs2, twc, tws, cs.astype(BF16)


def _fourier(x, rstd, mod, gains):
    bs, ls, d = x.shape
    r, gw = FFT_RADIX, FOURIER_GROUP_WIDTH
    assert ls == r * r
    lhs1, lhs2, twc, tws, cs = _fourier_tables()
    scratch = pltpu.VMEM((gw // LANES, r * FFT_PITCH, LANES), F32)
    return pl.pallas_call(
        _fourier_kernel,
        grid=(bs, d // gw),
        in_specs=[
            pl.BlockSpec((1, ls, gw), lambda b, g: (b, 0, g)),
            pl.BlockSpec((1, ls, LANES), lambda b, g: (b, 0, 0)),
            pl.BlockSpec((1, N_MOD, gw), lambda b, g: (b, 0, g)),
            pl.BlockSpec((gains.shape[0], gw), lambda b, g: (0, g)),
            _resident(lhs1.shape), _resident(twc.shape), _resident(tws.shape),
            _resident(lhs2.shape), _resident(cs.shape),
        ],
        out_specs=pl.BlockSpec((1, ls, gw), lambda b, g: (b, 0, g)),
        out_shape=jax.ShapeDtypeStruct((bs, ls, d), BF16),
        scratch_shapes=[scratch, scratch, scratch, pltpu.VMEM((ls, 2 * gw), BF16)],
        compiler_params=_params("parallel", "parallel"),
        name="fourier_mix",
    )(x, rstd, mod, gains, lhs1, twc, tws, lhs2, cs)


def _even_layer(x, s_ctx, mod, gains, ffn_w, layer, w_in, q_gain, k_gain, sink, ssm, d_skip, w_glu, w_out, e):
    bs, ls, d = x.shape
    ctx_index = bs
    x = _half_ffn(x, mod, 0, gains, 0, ffn_w, (layer, 0))
    n_ctx = s_ctx.shape[1]
    s_ctx = _half_ffn(s_ctx.reshape(1, bs * n_ctx, d), mod, 0, gains, 0, ffn_w, (layer, 0),
                      mod_index=ctx_index).reshape(bs, n_ctx, d)

    wqkv = w_in[:, :Q_WIDTH + 2 * KV_WIDTH]
    wut = w_in[:, Q_WIDTH + 2 * KV_WIDTH:].T.astype(BF16)
    bd = jnp.kron(jnp.eye(N_Q_HEADS, dtype=F32), jnp.ones((HEAD_DIM, HEAD_DIM), F32)).astype(BF16)
    qg = jnp.tile(q_gain.astype(F32), N_Q_HEADS)[None, :]
    kg = jnp.tile(k_gain.astype(F32), N_KV_HEADS)[None, :]
    cos, sin = _rope_tables(ls)
    n_ctx = s_ctx.shape[1]
    one = jnp.ones((n_ctx, LANES), F32)
    q, kk, vv, ut = _in_proj(x, mod, gains, wqkv, wut, bd, qg, kg, cos, sin)
    _, kkc, vvc, utc = _in_proj(s_ctx, mod, gains, wqkv, wut, bd, qg, kg, one, 0.0 * one, mod_index=ctx_index)

    attn = _attention(q, kk, vv, kkc, vvc, sink.astype(F32))

    t = SSM_ROW
    kcat, wst, wcar, arow = _ssm_prep(*ssm)
    yt = _ssm(ut, utc.reshape(SSM_WIDTH, bs * n_ctx // t, t), kcat, wst, wcar, arow, d_skip, bs)

    s = _ssm_glu(yt, w_glu.T.astype(BF16), bs, ls)
    mix = ((attn, w_out, _picked(w_out, (e,), (Q_WIDTH, d), (0, 0))),
           (s, w_out, _picked(w_out, (e,), (SSM_WIDTH, d), (1, 0))))
    return _half_ffn(x, mod, 6, gains, 2, ffn_w, (layer, 1), mix=mix)


def _odd_layer(x, mod, gains, ffn_w, layer, w_f, o):
    x, rstd = _half_ffn(x, mod, 0, gains, 0, ffn_w, (layer, 0), emit_rstd=True)
    z = _fourier(x, rstd, mod, gains)
    mix = ((z, w_f, _picked(w_f, (o,), w_f.shape[1:])),)
    return _half_ffn(x, mod, 6, gains, 2, ffn_w, (layer, 1), mix=mix)


def kernel(x, c, ctx, c_ctx, w_ada, b_ada, norm_gain, ffn_w1, ffn_w3, ffn_w2, w_in, q_gain, k_gain, sink_logit, ssm_lam_re, ssm_lam_im, ssm_log_dt, ssm_b_re, ssm_b_im, ssm_c_re, ssm_c_im, ssm_d, ssm_w_glu, w_out, fourier_w_out):
    bs, ls, d = x.shape
    depth = w_ada.shape[0]
    assert depth == 2 and d == D_MODEL
    rows = 16
    cond = jnp.concatenate([c, c_ctx[None, :], jnp.zeros((rows - bs - 1, d), F32)], axis=0)
    mods = _ada_mod(cond, w_ada, b_ada)[:, :bs + 1].reshape(depth, bs + 1, N_MOD, d)

    ffn_w = (ffn_w1, ffn_w3, ffn_w2)
    ssm = (ssm_lam_re[0], ssm_lam_im[0], ssm_log_dt[0], ssm_b_re[0], ssm_b_im[0], ssm_c_re[0], ssm_c_im[0])
    x = _even_layer(x, ctx, mods[0], norm_gain[0], ffn_w, 0, w_in[0], q_gain[0], k_gain[0],
                    sink_logit[0], ssm, ssm_d[0], ssm_w_glu[0], w_out, 0)
    return _odd_layer(x, mods[1], norm_gain[1], ffn_w, 1, fourier_w_out, 0)
```

```python
import functools
import math

import jax
import jax.numpy as jnp
from jax import lax
from jax.experimental import pallas as pl
from jax.experimental.pallas import tpu as pltpu

F32 = jnp.float32
BF16 = jnp.bfloat16

D_MODEL = 1024
N_MOD = 9
EPS = 1e-6
HEAD_DIM = 64
N_Q_HEADS = 8
N_KV_HEADS = 2
Q_WIDTH = N_Q_HEADS * HEAD_DIM
KV_WIDTH = N_KV_HEADS * HEAD_DIM
ATT_BLOCK = 128
ROPE_BASE = 10000.0
NEG_INF = -1e30
GRID_W = 64
SSM_WIDTH = 512
SSM_GROUP = 16
SSM_GROUPS = 32
SSM_STATE = 64
SSM_ROW = 128
SSM_CHUNK = 64
FOURIER_GROUPS = 4
FOURIER_GROUP_WIDTH = 256
FFT_RADIX = 64
FFT_PITCH = 72
FFN_CHUNK = 256
LANES = 128
VMEM_LIMIT = 56 * 1024 * 1024


def _dot(a, b):
    return jnp.dot(a, b, preferred_element_type=F32)


def _dot_w(a, w):
    return lax.dot_general(a, w, (((1,), (0,)), ((), ())), preferred_element_type=F32)


def _dot_nt(a, b):
    return lax.dot_general(a, b, (((1,), (1,)), ((), ())), preferred_element_type=F32)


def _split(a):
    hi = a.astype(BF16)
    lo = (a - hi.astype(F32)).astype(BF16)
    return hi, lo


def _dot3(a, b):
    a_hi, a_lo = _split(a)
    b_hi, b_lo = _split(b)
    return _dot(a_hi, b_hi) + _dot(a_lo, b_hi) + _dot(a_hi, b_lo)


def _norm_mod(x, gain, shift, scale):
    ms = jnp.mean(x * x, axis=-1, keepdims=True)
    return (x * lax.rsqrt(ms + EPS)) * (gain * (1.0 + scale)) + shift


def _params(*sem):
    return pltpu.CompilerParams(dimension_semantics=sem, vmem_limit_bytes=VMEM_LIMIT)


def _resident(shape):
    nd = len(shape)
    return pl.BlockSpec(shape, lambda *_: (0,) * nd, pipeline_mode=pl.Buffered(1))


def _ada_kernel(cond_ref, w_ref, b_ref, o_ref):
    a = cond_ref[...]
    a = a * jax.nn.sigmoid(a)
    o_ref[0] = _dot3(a, w_ref[0]) + b_ref[0]


def _ada_mod(cond, w_ada, b_ada):
    depth, d, nd = w_ada.shape
    rows = cond.shape[0]
    tn = nd // 8
    return pl.pallas_call(
        _ada_kernel,
        grid=(depth, nd // tn),
        in_specs=[
            pl.BlockSpec((rows, d), lambda l, j: (0, 0)),
            pl.BlockSpec((1, d, tn), lambda l, j: (l, 0, j)),
            pl.BlockSpec((1, 1, tn), lambda l, j: (l, 0, j)),
        ],
        out_specs=pl.BlockSpec((1, rows, tn), lambda l, j: (l, 0, j)),
        out_shape=jax.ShapeDtypeStruct((depth, rows, nd), F32),
        compiler_params=_params("parallel", "parallel"),
        name="ada_mod",
    )(cond, w_ada, b_ada.reshape(depth, 1, nd))


def _ffn_kernel(*refs, row0, grow, n_mix, emit_rstd):
    x_ref, mod_ref, gain_ref, w1_ref, w3_ref, w2_ref = refs[:6]
    mix_refs = refs[6:6 + 2 * n_mix]
    out_refs = refs[6 + 2 * n_mix:-1]
    acc_ref = refs[-1]
    o_ref = out_refs[0]
    x = x_ref[0]
    if n_mix:
        y = _dot_w(mix_refs[0][0], mix_refs[1][...])
        for k in range(1, n_mix):
            y = y + _dot_w(mix_refs[2 * k][0], mix_refs[2 * k + 1][...])
        x = x + mod_ref[0, 5:6, :] * y
        o_ref[0] = x
    shift = mod_ref[0, row0:row0 + 1, :]
    scale = mod_ref[0, row0 + 1:row0 + 2, :]
    gate = mod_ref[0, row0 + 2:row0 + 3, :]
    h = _norm_mod(x, gain_ref[grow:grow + 1, :], shift, scale)
    fc = FFN_CHUNK
    for c in range(w1_ref.shape[1] // fc):
        a = _dot_w(h, w1_ref[:, c * fc:(c + 1) * fc])
        b = _dot_w(h, w3_ref[:, c * fc:(c + 1) * fc])
        g = a * jax.nn.sigmoid(a) * b
        part = _dot_w(g, w2_ref[c * fc:(c + 1) * fc, :])
        if c == 0:
            acc_ref[...] = part
        else:
            acc_ref[...] += part
    base = o_ref[0] if n_mix else x_ref[0]
    o = base + (0.5 * gate) * acc_ref[...]
    o_ref[0] = o
    if emit_rstd:
        rstd = lax.rsqrt(jnp.mean(o * o, axis=-1, keepdims=True) + EPS)
        out_refs[1][0] = jnp.broadcast_to(rstd, out_refs[1].shape[1:])


def _picked(arr, lead, shape, start=None):
    start = start or (0,) * len(shape)
    return pl.BlockSpec((None,) * len(lead) + tuple(shape), lambda *_: tuple(lead) + tuple(start),
                        pipeline_mode=pl.Buffered(1))


def _half_ffn(s, mod, row0, gains, grow, ffn_w, lk, mod_index=None, mix=(), emit_rstd=False, tm=512):
    bs, ls, d = s.shape
    tm = min(tm, ls)
    if mod_index is None:
        mod_map = lambda b, i: (b, 0, 0)
    else:
        mod_map = lambda b, i: (mod_index, 0, 0)
    tok = lambda w: pl.BlockSpec((1, tm, w), lambda b, i: (b, i, 0))
    in_specs = [tok(d), pl.BlockSpec((1, N_MOD, d), mod_map), _resident(gains.shape)]
    in_specs += [_picked(w, lk, w.shape[2:]) for w in ffn_w]
    args = [s, mod, gains, *ffn_w]
    for z, w, w_spec in mix:
        in_specs += [tok(z.shape[2]), w_spec]
        args += [z, w]
    out_specs = [tok(d)]
    out_shape = [jax.ShapeDtypeStruct(s.shape, F32)]
    if emit_rstd:
        out_specs.append(tok(LANES))
        out_shape.append(jax.ShapeDtypeStruct((bs, ls, LANES), F32))
    outs = pl.pallas_call(
        functools.partial(_ffn_kernel, row0=row0, grow=grow, n_mix=len(mix), emit_rstd=emit_rstd),
        grid=(bs, ls // tm),
        in_specs=in_specs,
        out_specs=out_specs,
        out_shape=out_shape,
        scratch_shapes=[pltpu.VMEM((tm, d), F32)],
        compiler_params=_params("parallel", "parallel"),
        name="half_ffn",
    )(*args)
    return outs if emit_rstd else outs[0]


def _head_norm(t, bd, gain):
    t2 = (t * t).astype(BF16)
    slab = min(t.shape[1], 2 * LANES)
    sums = [_dot(t2[:, s:s + slab], bd[:slab, :slab]) for s in range(0, t.shape[1], slab)]
    ms = (jnp.concatenate(sums, axis=1) if len(sums) > 1 else sums[0]) * (1.0 / HEAD_DIM)
    return t * lax.rsqrt(ms + EPS) * gain


def _rope(t, cos, sin):
    w = t.shape[1]
    half = HEAD_DIM // 2
    lane = lax.broadcasted_iota(jnp.int32, t.shape, 1)
    first = (lane % HEAD_DIM) < half
    rot = jnp.where(first, pltpu.roll(t, w - half, 1), pltpu.roll(t, half, 1))
    reps = w // LANES
    c = jnp.concatenate([cos] * reps, axis=1) if reps > 1 else cos
    s = jnp.concatenate([sin] * reps, axis=1) if reps > 1 else sin
    return t * c + rot * s


def _dup_heads(t):
    lane = lax.broadcasted_iota(jnp.int32, t.shape, 1)
    low = lane < HEAD_DIM
    r = pltpu.roll(t, HEAD_DIM, 1)
    return jnp.concatenate([jnp.where(low, t, r), jnp.where(low, r, t)], axis=1)


def _inproj_kernel(x_ref, mod_ref, gain_ref, wqkv_ref, wut_ref, bd_ref, qg_ref, kg_ref, cos_ref, sin_ref,
                   q_ref, kk_ref, vv_ref, ut_ref, *flat):
    x = x_ref[0]
    h = _norm_mod(x, gain_ref[1:2, :], mod_ref[0, 3:4, :], mod_ref[0, 4:5, :]).astype(BF16)
    qkv = _dot_w(h, wqkv_ref[...])
    q = qkv[:, :Q_WIDTH]
    k = qkv[:, Q_WIDTH:Q_WIDTH + KV_WIDTH]
    v = qkv[:, Q_WIDTH + KV_WIDTH:]
    cos = cos_ref[...]
    sin = sin_ref[...]
    q = _rope(_head_norm(q, bd_ref[...], qg_ref[...]), cos, sin)
    k = _rope(_head_norm(k, bd_ref[:KV_WIDTH, :KV_WIDTH], kg_ref[...]), cos, sin)
    q_ref[0] = (q * (HEAD_DIM ** -0.5)).astype(BF16)
    kk_ref[0] = _dup_heads(k).astype(BF16)
    vv_ref[0] = _dup_heads(v).astype(BF16)
    ut = _dot_nt(wut_ref[...], h)
    if not flat:
        ut_ref[...] = ut
        return
    flat_ref, = flat
    nr = ut_ref.shape[1]
    for j in range(nr):
        flat_ref[pl.ds(j, SSM_WIDTH, stride=nr), :] = ut[:, j * SSM_ROW:(j + 1) * SSM_ROW]

    ut_ref[...] = flat_ref[...].reshape(ut_ref.shape)


def _in_proj(s, mod, gains, wqkv, wut, bd, qg, kg, cos, sin, mod_index=None, tm=1024):
    bs, ls, d = s.shape
    tm = min(tm, ls)
    nt = ls // tm
    nr = tm // SSM_ROW
    by_rows = nr % 8 == 0
    if by_rows:
        ut_spec = pl.BlockSpec((SSM_WIDTH, nr, SSM_ROW), lambda b, i: (0, b * nt + i, 0))
        ut_shape = jax.ShapeDtypeStruct((SSM_WIDTH, bs * ls // SSM_ROW, SSM_ROW), F32)
        scratch = [pltpu.VMEM((SSM_WIDTH * nr, SSM_ROW), F32)]
    else:
        ut_spec = pl.BlockSpec((SSM_WIDTH, tm), lambda b, i: (0, b * nt + i))
        ut_shape = jax.ShapeDtypeStruct((SSM_WIDTH, bs * ls), F32)
        scratch = []
    if mod_index is None:
        mod_map = lambda b, i: (b, 0, 0)
    else:
        mod_map = lambda b, i: (mod_index, 0, 0)
    tok = lambda w: pl.BlockSpec((1, tm, w), lambda b, i: (b, i, 0))
    return pl.pallas_call(
        _inproj_kernel,
        grid=(bs, nt),
        in_specs=[
            tok(d),
            pl.BlockSpec((1, N_MOD, d), mod_map),
            _resident(gains.shape),
            _resident(wqkv.shape),
            _resident(wut.shape),
            _resident(bd.shape),
            _resident(qg.shape),
            _resident(kg.shape),
            pl.BlockSpec((tm, LANES), lambda b, i: (i, 0)),
            pl.BlockSpec((tm, LANES), lambda b, i: (i, 0)),
        ],
        out_specs=[tok(Q_WIDTH), tok(2 * KV_WIDTH), tok(2 * KV_WIDTH), ut_spec],
        out_shape=[
            jax.ShapeDtypeStruct((bs, ls, Q_WIDTH), BF16),
            jax.ShapeDtypeStruct((bs, ls, 2 * KV_WIDTH), BF16),
            jax.ShapeDtypeStruct((bs, ls, 2 * KV_WIDTH), BF16),
            ut_shape,
        ],
        scratch_shapes=scratch,
        compiler_params=_params("parallel", "parallel"),
        name="in_proj",
    )(s, mod, gains, wqkv, wut, bd, qg, kg, cos, sin)


def _rope_tables(n_lat):
    rows = n_lat // GRID_W
    row = jnp.repeat(jnp.arange(rows, dtype=F32), GRID_W)
    col = jnp.tile(jnp.arange(GRID_W, dtype=F32), rows)
    n_freq = HEAD_DIM // 4
    inv_freq = 1.0 / (ROPE_BASE ** (jnp.arange(n_freq, dtype=F32) / n_freq))
    ang = jnp.concatenate([row[:, None] * inv_freq, col[:, None] * inv_freq], axis=-1)
    cos, sin = jnp.cos(ang), jnp.sin(ang)
    cos_t = jnp.tile(cos, (1, LANES // (HEAD_DIM // 2)))
    sin_t = jnp.tile(jnp.concatenate([-sin, sin], axis=-1), (1, LANES // HEAD_DIM))
    return cos_t, sin_t


def _attn_kernel(sink_ref, q_ref, kk_ref, vv_ref, kkc_ref, vvc_ref, o_ref):
    blk = ATT_BLOCK
    per_step = q_ref.shape[1] // blk
    nb = pl.num_programs(1) * per_step
    for k in range(per_step):
        o_ref[0, k * blk:(k + 1) * blk, :] = _attn_block(
            pl.program_id(1) * per_step + k, nb, q_ref[0, k * blk:(k + 1) * blk, :],
            sink_ref, kk_ref, vv_ref, kkc_ref, vvc_ref)


def _attn_block(i, nb, q, sink_ref, kk_ref, vv_ref, kkc_ref, vvc_ref):
    blk = ATT_BLOCK
    lane = lax.broadcasted_iota(jnp.int32, (blk, LANES), 1)
    low = lane < HEAD_DIM
    rel = lane - lax.broadcasted_iota(jnp.int32, (blk, LANES), 0)
    ok_prev = rel >= jnp.where(i > 0, 0, 2 * blk)
    ok_next = rel <= jnp.where(i < nb - 1, 0, -2 * blk)
    j_prev = jnp.maximum(i - 1, 0)
    j_next = jnp.minimum(i + 1, nb - 1)
    zero = jnp.zeros_like(q[:, :LANES])
    cols = []
    for h in range(N_KV_HEADS):
        hs = slice(h * LANES, (h + 1) * LANES)

        def rows(ref, j):
            return ref[0, pl.ds(pl.multiple_of(j * blk, blk), blk), hs]

        keys = jnp.concatenate([rows(kk_ref, j_prev), rows(kk_ref, i), rows(kk_ref, j_next), kkc_ref[0, :, hs]], axis=0)
        vals = jnp.concatenate([rows(vv_ref, j_prev), rows(vv_ref, i), rows(vv_ref, j_next), vvc_ref[0, :, hs]], axis=0)
        qs = []
        for col in (2 * h, 2 * h + 1):
            qc = q[:, col * LANES:(col + 1) * LANES]
            qs.append(jnp.where(low, qc, zero))
            qs.append(jnp.where(low, zero, qc))
        s_all = _dot_nt(jnp.concatenate(qs, axis=0), keys)
        low_k = lax.broadcasted_iota(jnp.int32, vals.shape, 1) < HEAD_DIM
        v_ones = (jnp.where(low_k, vals, jnp.ones_like(vals)), jnp.where(low_k, jnp.ones_like(vals), vals))
        ps, tail = [], []
        for g in range(4):
            s = s_all[g * blk:(g + 1) * blk]
            parts = [jnp.where(ok_prev, s[:, :blk], NEG_INF), s[:, blk:2 * blk],
                     jnp.where(ok_next, s[:, 2 * blk:3 * blk], NEG_INF)]
            parts += [s[:, j * blk:(j + 1) * blk] for j in range(3, s.shape[1] // blk)]
            peak = parts[0]
            for part in parts[1:]:
                peak = jnp.maximum(peak, part)
            sink = sink_ref[4 * h + g]
            m = jnp.maximum(jnp.max(peak, axis=-1, keepdims=True), sink)
            ps.append(jnp.exp(jnp.concatenate([part - m for part in parts], axis=1).astype(BF16)))
            tail.append(jnp.exp(sink - m))
        og = []
        for par in range(2):
            o = _dot(jnp.concatenate([ps[par], ps[2 + par]], axis=0), v_ones[par])
            for k in range(2):
                ok = o[k * blk:(k + 1) * blk]
                den = (ok[:, HEAD_DIM:HEAD_DIM + 1] if par == 0 else ok[:, 0:1]) + tail[2 * k + par]
                og.append(ok * (1.0 / den))
        cols.append(jnp.where(low, og[0], og[2]))
        cols.append(jnp.where(low, og[1], og[3]))
    return jnp.concatenate(cols, axis=1).astype(BF16)


def _attention(q, kk, vv, kkc, vvc, sink):
    bs, ls, _ = q.shape
    tq = 8 * ATT_BLOCK
    n_ctx = kkc.shape[1]
    full = lambda n: pl.BlockSpec((1, n, 2 * KV_WIDTH), lambda b, i: (b, 0, 0))
    return pl.pallas_call(
        _attn_kernel,
        grid=(bs, ls // tq),
        in_specs=[
            pl.BlockSpec(memory_space=pltpu.SMEM),
            pl.BlockSpec((1, tq, Q_WIDTH), lambda b, i: (b, i, 0)),
            full(ls), full(ls), full(n_ctx), full(n_ctx),
        ],
        out_specs=pl.BlockSpec((1, tq, Q_WIDTH), lambda b, i: (b, i, 0)),
        out_shape=jax.ShapeDtypeStruct((bs, ls, Q_WIDTH), BF16),
        compiler_params=_params("parallel", "arbitrary"),
        name="window_attention",
    )(sink, q, kk, vv, kkc, vvc)


def _cpow(lr, li, dt, tau):
    mag = jnp.exp(lr * dt * tau)
    ang = li * dt * tau
    return mag * jnp.cos(ang), mag * jnp.sin(ang)


def _ssm_prep_kernel(lrr_ref, lir_ref, lrc_ref, lic_ref, ldt_ref, btr_ref, bti_ref, btrr_ref, btri_ref,
                     crr_ref, cri_ref, ctr_ref, cti_ref, kcat_ref, wst_ref, wcar_ref, arow_ref):
    p = SSM_STATE
    t = SSM_CHUNK
    lane_t = lax.broadcasted_iota(jnp.int32, (1, t), 1).astype(F32)
    sub_t = lax.broadcasted_iota(jnp.int32, (t, 1), 0).astype(F32)
    kparts = []
    for d in range(2):
        dt = jnp.exp(ldt_ref[0, d])
        lr, li = lrr_ref[0, d], lir_ref[0, d]
        lrc, lic = lrc_ref[0, d], lic_ref[0, d]
        ar, ai = _cpow(lr, li, dt, 1.0)
        nr = ar - 1.0
        den = lr * lr + li * li
        fr = (nr * lr + ai * li) / den
        fi = (ai * lr - nr * li) / den
        bt_r, bt_i = btr_ref[0, d], bti_ref[0, d]
        bbr = fr * bt_r - fi * bt_i
        bbi = fr * bt_i + fi * bt_r
        rep_r, rep_i = btrr_ref[0, d], btri_ref[0, d]
        bbr_rep = fr * rep_r - fi * rep_i
        bbi_rep = fr * rep_i + fi * rep_r
        c_r, c_i = crr_ref[0, d], cri_ref[0, d]
        m1 = c_r * bbr_rep - c_i * bbi_rep
        m2 = c_r * bbi_rep + c_i * bbr_rep
        tau = lane_t if d == 0 else (float(t) - lane_t)
        e_r, e_i = _cpow(lrc, lic, dt, tau)
        kmat = _dot3(m1, e_r) - _dot3(m2, e_i)
        if d == 0:
            kparts.append(kmat)
        else:
            lag0 = jnp.sum(m1, axis=1, keepdims=True)
            first = lax.broadcasted_iota(jnp.int32, (1, t), 1) == 0
            kparts[0] = kparts[0] + jnp.where(first, lag0, 0.0)
            kparts.insert(0, kmat)
        tau_s = (float(t - 1) - sub_t) if d == 0 else sub_t
        et_r, et_i = _cpow(lr, li, dt, tau_s)
        for i in range(SSM_GROUP):
            re = et_r * bbr[i:i + 1] - et_i * bbi[i:i + 1]
            im = et_r * bbi[i:i + 1] + et_i * bbr[i:i + 1]
            wst_ref[0, i * t:(i + 1) * t, d * 2 * p:(d + 1) * 2 * p] = jnp.concatenate([re, im], axis=1).astype(BF16)
        if d == 0:
            ec_r, ec_i = _cpow(lrc, lic, dt, lane_t + 1.0)
        else:
            ec_r, ec_i = e_r, e_i
        ct_r, ct_i = ctr_ref[0, d], cti_ref[0, d]
        car_re, car_im = [], []
        for jo in range(SSM_GROUP):
            cr = ct_r[:, jo:jo + 1]
            ci = ct_i[:, jo:jo + 1]
            car_re.append(cr * ec_r - ci * ec_i)
            car_im.append(-(cr * ec_i + ci * ec_r))
        wcar_ref[0, d * 2 * p:d * 2 * p + p, :] = jnp.concatenate(car_re, axis=1).astype(BF16)
        wcar_ref[0, d * 2 * p + p:(d + 1) * 2 * p, :] = jnp.concatenate(car_im, axis=1).astype(BF16)
        at_r, at_i = _cpow(lr, li, dt, float(t))
        arow_ref[0, d, 0:1, :] = jnp.concatenate([at_r, at_r], axis=1)
        arow_ref[0, d, 1:2, :] = jnp.concatenate([-at_i, at_i], axis=1)
    kcat_ref[0] = jnp.concatenate(kparts, axis=1)


def _ssm_prep(lam_re, lam_im, log_dt, b_re, b_im, c_re, c_im):
    g, p, n, t = SSM_GROUPS, SSM_STATE, SSM_GROUP, SSM_CHUNK
    gd = lambda a: jnp.swapaxes(a, 0, 1)
    lrr = gd(lam_re)[:, :, None, :]
    lir = gd(lam_im)[:, :, None, :]
    lrc = gd(lam_re)[:, :, :, None]
    lic = gd(lam_im)[:, :, :, None]
    ldt = gd(log_dt)[:, :, None, None]
    bt_r = jnp.swapaxes(gd(b_re), 2, 3)
    bt_i = jnp.swapaxes(gd(b_im), 2, 3)
    btr_r = jnp.tile(bt_r, (1, 1, n, 1))
    btr_i = jnp.tile(bt_i, (1, 1, n, 1))
    cr_r = jnp.repeat(gd(c_re), n, axis=2)
    cr_i = jnp.repeat(gd(c_im), n, axis=2)
    ct_r = jnp.swapaxes(gd(c_re), 2, 3)
    ct_i = jnp.swapaxes(gd(c_im), 2, 3)
    args = (lrr, lir, lrc, lic, ldt, bt_r, bt_i, btr_r, btr_i, cr_r, cr_i, ct_r, ct_i)
    spec = lambda a: pl.BlockSpec((1,) + a.shape[1:], lambda i: (i,) + (0,) * (a.ndim - 1))
    return pl.pallas_call(
        _ssm_prep_kernel,
        grid=(g,),
        in_specs=[spec(a) for a in args],
        out_specs=[
            pl.BlockSpec((1, n * n, 2 * t), lambda i: (i, 0, 0)),
            pl.BlockSpec((1, n * t, 4 * p), lambda i: (i, 0, 0)),
            pl.BlockSpec((1, 4 * p, n * t), lambda i: (i, 0, 0)),
            pl.BlockSpec((1, 2, 2, 2 * p), lambda i: (i, 0, 0, 0)),
        ],
        out_shape=[
            jax.ShapeDtypeStruct((g, n * n, 2 * t), F32),
            jax.ShapeDtypeStruct((g, n * t, 4 * p), BF16),
            jax.ShapeDtypeStruct((g, 4 * p, n * t), BF16),
            jax.ShapeDtypeStruct((g, 2, 2, 2 * p), F32),
        ],
        compiler_params=_params("parallel"),
        name="ssm_prep",
    )(*args)


def _halves_to_rows(a, b, low):
    return jnp.where(low, a, pltpu.roll(b, SSM_CHUNK, 1)), jnp.where(low, pltpu.roll(a, SSM_CHUNK, 1), b)


def _ssm_kernel(u_ref, uc_ref, kcat_ref, wst_ref, wcar_ref, arow_ref, d_ref, y_ref,
                tz_ref, s_ref, sc_ref, h_ref, *, nb, nc, ncc):
    n, t, p = SSM_GROUP, SSM_CHUNK, SSM_STATE
    rows, rows_c = u_ref.shape[1], uc_ref.shape[1]
    low = lax.broadcasted_iota(jnp.int32, (1, SSM_ROW), 1) < t

    def build(i, carry):
        for m in range(n // 2):
            even = jnp.broadcast_to(kcat_ref[0, pl.ds(2 * m * n + i, 1), :], (t, SSM_ROW))
            odd = jnp.broadcast_to(kcat_ref[0, pl.ds((2 * m + 1) * n + i, 1), :], (t, SSM_ROW))
            even = pltpu.roll(even, t, 1, stride=1, stride_axis=0)
            odd = pltpu.roll(odd, 0, 1, stride=1, stride_axis=0)
            tz_ref[pl.ds(pl.multiple_of(i * t, t), t), m * SSM_ROW:(m + 1) * SSM_ROW] = jnp.where(low, even, odd).astype(BF16)
        return carry

    lax.fori_loop(0, n, build, 0, unroll=True)

    def chunk_rows(ref):
        first, second = [], []
        for m in range(n // 2):
            a, b = _halves_to_rows(ref[2 * m], ref[2 * m + 1], low)
            first.append(a.astype(BF16))
            second.append(b.astype(BF16))
        return jnp.concatenate([jnp.concatenate(first, axis=1), jnp.concatenate(second, axis=1)], axis=0)

    x = chunk_rows(u_ref)
    xc = chunk_rows(uc_ref)
    s_all = _dot(x, wst_ref[0])
    sc_all = _dot(xc, wst_ref[0])
    for d in range(2):
        s_ref[d, 0] = s_all[:, d * 2 * p:(d + 1) * 2 * p]
        s_ref[d, 1] = pltpu.roll(s_all[:, d * 2 * p:(d + 1) * 2 * p], p, 1)
        sc_ref[d, 0] = sc_all[:, d * 2 * p:(d + 1) * 2 * p]
        sc_ref[d, 1] = pltpu.roll(sc_all[:, d * 2 * p:(d + 1) * 2 * p], p, 1)

    coef = [(arow_ref[0, d, 0:1, :], arow_ref[0, d, 1:2, :]) for d in range(2)]

    def advance(d, h, hs, s, ss):
        a1, a2 = coef[d]
        return a1 * h + a2 * hs + s, a1 * hs - a2 * h + ss

    state = []
    for d in range(2):
        h = hs = jnp.zeros((nb, 2 * p), F32)
        order = range(2 * ncc) if d == 0 else reversed(range(2 * ncc))
        for chunk in order:
            sel = pl.ds((chunk % 2) * rows_c + chunk // 2, nb, stride=ncc)
            h, hs = advance(d, h, hs, sc_ref[d, 0, sel, :], sc_ref[d, 1, sel, :])
        state += [h, hs]

    def scan(k, carry):
        out = []
        for d in range(2):
            h, hs = carry[2 * d], carry[2 * d + 1]
            row = k if d == 0 else nc - 1 - k
            for half in ((0, 1) if d == 0 else (1, 0)):
                sel = pl.ds(half * rows + row, nb, stride=nc)
                h_ref[d, sel, :] = h
                h, hs = advance(d, h, hs, s_ref[d, 0, sel, :], s_ref[d, 1, sel, :])
            out += [h, hs]
        return tuple(out)

    lax.fori_loop(0, nc, scan, tuple(state), unroll=True)

    hb = jnp.concatenate([h_ref[0], h_ref[1]], axis=1).astype(BF16)
    nblk = 2 * SSM_ROW
    for q in range(n * t // nblk):
        y = _dot(x, tz_ref[:, q * nblk:(q + 1) * nblk]) + _dot(hb, wcar_ref[0, :, q * nblk:(q + 1) * nblk])
        for mm in range(2):
            m = 2 * q + mm
            even, odd = _halves_to_rows(y[:rows, mm * SSM_ROW:(mm + 1) * SSM_ROW],
                                        y[rows:, mm * SSM_ROW:(mm + 1) * SSM_ROW], low)
            y_ref[2 * m] = even + d_ref[2 * m] * u_ref[2 * m]
            y_ref[2 * m + 1] = odd + d_ref[2 * m + 1] * u_ref[2 * m + 1]


def _ssm(ut, utc, kcat, wst, wcar, arow, d_skip, nb):
    n, t, p, g = SSM_GROUP, SSM_CHUNK, SSM_STATE, SSM_GROUPS
    rows = ut.shape[1]
    rows_c = utc.shape[1]
    nc, ncc = rows // nb, rows_c // nb
    d3 = jnp.broadcast_to(d_skip.astype(F32)[:, None, None], (SSM_WIDTH, 1, SSM_ROW))
    return pl.pallas_call(
        functools.partial(_ssm_kernel, nb=nb, nc=nc, ncc=ncc),
        grid=(g,),
        in_specs=[
            pl.BlockSpec((n, rows, SSM_ROW), lambda i: (i, 0, 0)),
            pl.BlockSpec((n, rows_c, SSM_ROW), lambda i: (i, 0, 0)),
            pl.BlockSpec((1, n * n, 2 * t), lambda i: (i, 0, 0)),
            pl.BlockSpec((1, n * t, 4 * p), lambda i: (i, 0, 0)),
            pl.BlockSpec((1, 4 * p, n * t), lambda i: (i, 0, 0)),
            pl.BlockSpec((1, 2, 2, 2 * p), lambda i: (i, 0, 0, 0)),
            pl.BlockSpec((n, 1, SSM_ROW), lambda i: (i, 0, 0)),
        ],
        out_specs=pl.BlockSpec((n, rows, SSM_ROW), lambda i: (i, 0, 0)),
        out_shape=jax.ShapeDtypeStruct((SSM_WIDTH, rows, SSM_ROW), F32),
        scratch_shapes=[
            pltpu.VMEM((n * t, n * t), BF16),
            pltpu.VMEM((2, 2, 2 * rows, 2 * p), F32),
            pltpu.VMEM((2, 2, 2 * rows_c, 2 * p), F32),
            pltpu.VMEM((2, 2 * rows, 2 * p), F32),
        ],
        compiler_params=_params("parallel"),
        name="ssm_toeplitz",
    )(ut, utc, kcat, wst, wcar, arow, d3)


def _glu_kernel(yt_ref, wglut_ref, s_ref, flat_ref):
    t = SSM_ROW
    nck = yt_ref.shape[1]

    flat_ref[...] = yt_ref[...].reshape(flat_ref.shape)
    for j in range(0, nck, 2):
        g = jnp.concatenate([jax.nn.gelu(flat_ref[pl.ds(j + k, SSM_WIDTH, stride=nck), :]) for k in range(2)],
                            axis=1)
        z = _dot(wglut_ref[...], g.astype(BF16))
        gate = 0.5 * (1.0 + jnp.tanh(0.5 * z))
        s_ref[j * t:(j + 2) * t, :] = (g * gate).T.astype(BF16)


def _ssm_glu(yt, wglut, bs, ls):
    nck = 8
    rows = yt.shape[1]
    s = pl.pallas_call(
        _glu_kernel,
        grid=(rows // nck,),
        in_specs=[
            pl.BlockSpec((SSM_WIDTH, nck, SSM_ROW), lambda i: (0, i, 0)),
            _resident(wglut.shape),
        ],
        out_specs=pl.BlockSpec((nck * SSM_ROW, SSM_WIDTH), lambda i: (i, 0)),
        out_shape=jax.ShapeDtypeStruct((rows * SSM_ROW, SSM_WIDTH), BF16),
        scratch_shapes=[pltpu.VMEM((SSM_WIDTH * nck, SSM_ROW), F32)],
        compiler_params=_params("parallel"),
        name="ssm_glu",
    )(yt, wglut)
    return s.reshape(bs, ls, SSM_WIDTH)


def _fourier_kernel(x_ref, r_ref, mod_ref, gain_ref, lhs1_ref, twc_ref, tws_ref, lhs2_ref, cs_ref, z_ref,
                    h_ref, are_ref, aim_ref, y_ref):
    r, pitch = FFT_RADIX, FFT_PITCH
    halves = FOURIER_GROUP_WIDTH // LANES

    def slab(i):
        return pl.ds(pl.multiple_of(i * pitch, 8), r)

    def gather(ref, i):
        return jnp.concatenate([ref[hf, pl.ds(i, r, stride=pitch), :] for hf in range(halves)], axis=1)

    def put(ref, i, val):
        for hf in range(halves):
            ref[hf, slab(i), :] = val[:, hf * LANES:(hf + 1) * LANES]

    gain = gain_ref[1:2, :]
    shift, scale = mod_ref[0, 3:4, :], mod_ref[0, 4:5, :]

    gs = gain * (1.0 + scale)

    def fill(n1, carry):
        rows = pl.ds(pl.multiple_of(n1 * r, r), r)
        rstd = r_ref[0, rows, :]
        for hf in range(halves):
            cols = slice(hf * LANES, (hf + 1) * LANES)
            h_ref[hf, slab(n1), :] = (x_ref[0, rows, cols] * rstd) * gs[:, cols] + shift[:, cols]
        return carry

    lax.fori_loop(0, r, fill, 0, unroll=8)

    def stage1(jg, carry):
        cols = [gather(h_ref, jg * 4 + j).astype(BF16) for j in range(4)]
        a = _dot(lhs1_ref[...], jnp.concatenate(cols, axis=0))
        ch, sh = a[:4 * r], a[4 * r:]
        trow = pl.ds(pl.multiple_of(jg * 4 * r, 4 * r), 4 * r)
        tc = jnp.concatenate([twc_ref[trow, :]] * halves, axis=1)
        ts = jnp.concatenate([tws_ref[trow, :]] * halves, axis=1)
        a_re = ch * tc - sh * ts
        a_im = sh * tc + ch * ts
        for j in range(4):
            put(are_ref, jg * 4 + j, a_re[j * r:(j + 1) * r])
            put(aim_ref, jg * 4 + j, a_im[j * r:(j + 1) * r])
        return carry

    lax.fori_loop(0, r // 4, stage1, 0, unroll=True)

    def stage2(kg, carry):
        for j in range(4):
            k1 = kg * 4 + j
            st = jnp.concatenate([gather(are_ref, k1), gather(aim_ref, k1)], axis=0).astype(BF16)
            y = _dot(lhs2_ref[...], st)
            rows = pl.ds(pl.multiple_of(k1 * r, r), r)
            y_ref[rows, :] = jnp.concatenate([y[:r], y[r:]], axis=1).astype(BF16)
        return carry

    lax.fori_loop(0, r // 4, stage2, 0, unroll=True)

    def channels(kb, carry):
        rows = pl.ds(pl.multiple_of(kb * 8 * r, 8 * r), 8 * r)
        z = _dot(y_ref[rows, :], cs_ref[...])
        for j in range(8):
            put(h_ref, kb * 8 + j, z[j * r:(j + 1) * r])
        return carry

    lax.fori_loop(0, r // 8, channels, 0, unroll=True)

    def emit(k2, carry):
        z_ref[0, pl.ds(pl.multiple_of(k2 * r, r), r), :] = gather(h_ref, k2).astype(BF16)
        return carry

    lax.fori_loop(0, r, emit, 0, unroll=8)


def _fourier_tables():
    r = FFT_RADIX
    n = r * r
    k = jnp.arange(r, dtype=jnp.int32)
    ang = ((k[:, None] * k[None, :]) % r).astype(F32) * (2.0 * math.pi / r)
    wc = jnp.cos(ang) / 8.0
    ws = jnp.sin(ang) / 8.0
    eye4 = jnp.eye(4, dtype=F32)
    bc, bs = jnp.kron(eye4, wc), jnp.kron(eye4, ws)
    lhs1 = jnp.concatenate([bc, bs], axis=0).astype(BF16)
    lhs2 = jnp.concatenate([jnp.concatenate([wc, -ws], axis=1),
                            jnp.concatenate([-ws, -wc], axis=1)], axis=0).astype(BF16)
    tw = ((k[:, None] * k[None, :]) % n).astype(F32) * (2.0 * math.pi / n)
    twc = jnp.broadcast_to(jnp.cos(tw).reshape(n, 1), (n, LANES))
    tws = jnp.broadcast_to(jnp.sin(tw).reshape(n, 1), (n, LANES))
    m = jnp.arange(FOURIER_GROUP_WIDTH, dtype=jnp.int32)
    angc = ((m[:, None] * m[None, :]) % FOURIER_GROUP_WIDTH).astype(F32) * (2.0 * math.pi / FOURIER_GROUP_WIDTH)
    cs = jnp.concatenate([jnp.cos(angc), jnp.sin(angc)], axis=0) / 16.0
    return lhs1, lhs2, twc, tws, cs.astype(BF16)


def _fourier(x, rstd, mod, gains):
    bs, ls, d = x.shape
    r, gw = FFT_RADIX, FOURIER_GROUP_WIDTH
    assert ls == r * r
    lhs1, lhs2, twc, tws, cs = _fourier_tables()
    scratch = pltpu.VMEM((gw // LANES, r * FFT_PITCH, LANES), F32)
    return pl.pallas_call(
        _fourier_kernel,
        grid=(bs, d // gw),
        in_specs=[
            pl.BlockSpec((1, ls, gw), lambda b, g: (b, 0, g)),
            pl.BlockSpec((1, ls, LANES), lambda b, g: (b, 0, 0)),
            pl.BlockSpec((1, N_MOD, gw), lambda b, g: (b, 0, g)),
            pl.BlockSpec((gains.shape[0], gw), lambda b, g: (0, g)),
            _resident(lhs1.shape), _resident(twc.shape), _resident(tws.shape),
            _resident(lhs2.shape), _resident(cs.shape),
        ],
        out_specs=pl.BlockSpec((1, ls, gw), lambda b, g: (b, 0, g)),
        out_shape=jax.ShapeDtypeStruct((bs, ls, d), BF16),
        scratch_shapes=[scratch, scratch, scratch, pltpu.VMEM((ls, 2 * gw), BF16)],
        compiler_params=_params("parallel", "parallel"),
        name="fourier_mix",
    )(x, rstd, mod, gains, lhs1, twc, tws, lhs2, cs)


def _even_layer(x, s_ctx, mod, gains, ffn_w, layer, w_in, q_gain, k_gain, sink, ssm, d_skip, w_glu, w_out, e):
    bs, ls, d = x.shape
    ctx_index = bs
    x = _half_ffn(x, mod, 0, gains, 0, ffn_w, (layer, 0))
    n_ctx = s_ctx.shape[1]
    s_ctx = _half_ffn(s_ctx.reshape(1, bs * n_ctx, d), mod, 0, gains, 0, ffn_w, (layer, 0),
                      mod_index=ctx_index).reshape(bs, n_ctx, d)

    wqkv = w_in[:, :Q_WIDTH + 2 * KV_WIDTH]
    wut = w_in[:, Q_WIDTH + 2 * KV_WIDTH:].T.astype(BF16)
    bd = jnp.kron(jnp.eye(N_Q_HEADS, dtype=F32), jnp.ones((HEAD_DIM, HEAD_DIM), F32)).astype(BF16)
    qg = jnp.tile(q_gain.astype(F32), N_Q_HEADS)[None, :]
    kg = jnp.tile(k_gain.astype(F32), N_KV_HEADS)[None, :]
    cos, sin = _rope_tables(ls)
    n_ctx = s_ctx.shape[1]
    one = jnp.ones((n_ctx, LANES), F32)
    q, kk, vv, ut = _in_proj(x, mod, gains, wqkv, wut, bd, qg, kg, cos, sin)
    _, kkc, vvc, utc = _in_proj(s_ctx, mod, gains, wqkv, wut, bd, qg, kg, one, 0.0 * one, mod_index=ctx_index)

    attn = _attention(q, kk, vv, kkc, vvc, sink.astype(F32))

    t = SSM_ROW
    kcat, wst, wcar, arow = _ssm_prep(*ssm)
    yt = _ssm(ut, utc.reshape(SSM_WIDTH, bs * n_ctx // t, t), kcat, wst, wcar, arow, d_skip, bs)

    s = _ssm_glu(yt, w_glu.T.astype(BF16), bs, ls)
    mix = ((attn, w_out, _picked(w_out, (e,), (Q_WIDTH, d), (0, 0))),
           (s, w_out, _picked(w_out, (e,), (SSM_WIDTH, d), (1, 0))))
    return _half_ffn(x, mod, 6, gains, 2, ffn_w, (layer, 1), mix=mix)


def _odd_layer(x, mod, gains, ffn_w, layer, w_f, o):
    x, rstd = _half_ffn(x, mod, 0, gains, 0, ffn_w, (layer, 0), emit_rstd=True)
    z = _fourier(x, rstd, mod, gains)
    mix = ((z, w_f, _picked(w_f, (o,), w_f.shape[1:])),)
    return _half_ffn(x, mod, 6, gains, 2, ffn_w, (layer, 1), mix=mix)


def kernel(x, c, ctx, c_ctx, w_ada, b_ada, norm_gain, ffn_w1, ffn_w3, ffn_w2, w_in, q_gain, k_gain, sink_logit, ssm_lam_re, ssm_lam_im, ssm_log_dt, ssm_b_re, ssm_b_im, ssm_c_re, ssm_c_im, ssm_d, ssm_w_glu, w_out, fourier_w_out):
    bs, ls, d = x.shape
    depth = w_ada.shape[0]
    assert depth == 2 and d == D_MODEL
    rows = 16
    cond = jnp.concatenate([c, c_ctx[None, :], jnp.zeros((rows - bs - 1, d), F32)], axis=0)
    mods = _ada_mod(cond, w_ada, b_ada)[:, :bs + 1].reshape(depth, bs + 1, N_MOD, d)

    ffn_w = (ffn_w1, ffn_w3, ffn_w2)
    ssm = (ssm_lam_re[0], ssm_lam_im[0], ssm_log_dt[0], ssm_b_re[0], ssm_b_im[0], ssm_c_re[0], ssm_c_im[0])
    x = _even_layer(x, ctx, mods[0], norm_gain[0], ffn_w, 0, w_in[0], q_gain[0], k_gain[0],
                    sink_logit[0], ssm, ssm_d[0], ssm_w_glu[0], w_out, 0)
    return _odd_layer(x, mods[1], norm_gain[1], ffn_w, 1, fourier_w_out, 0)
```

```python
import functools
import math

import jax
import jax.numpy as jnp
from jax import lax
from jax.experimental import pallas as pl
from jax.experimental.pallas import tpu as pltpu

F32 = jnp.float32
BF16 = jnp.bfloat16

D_MODEL = 1024
N_MOD = 9
EPS = 1e-6
HEAD_DIM = 64
N_Q_HEADS = 8
N_KV_HEADS = 2
Q_WIDTH = N_Q_HEADS * HEAD_DIM
KV_WIDTH = N_KV_HEADS * HEAD_DIM
ATT_BLOCK = 128
ROPE_BASE = 10000.0
NEG_INF = -1e30
GRID_W = 64
SSM_WIDTH = 512
SSM_GROUP = 16
SSM_GROUPS = 32
SSM_STATE = 64
SSM_ROW = 128
SSM_CHUNK = 64
FOURIER_GROUPS = 4
FOURIER_GROUP_WIDTH = 256
FFT_RADIX = 64
FFT_PITCH = 72
FFN_CHUNK = 256
LANES = 128
VMEM_LIMIT = 56 * 1024 * 1024


def _dot(a, b):
    return jnp.dot(a, b, preferred_element_type=F32)


def _dot_w(a, w):
    return lax.dot_general(a, w, (((1,), (0,)), ((), ())), preferred_element_type=F32)


def _dot_nt(a, b):
    return lax.dot_general(a, b, (((1,), (1,)), ((), ())), preferred_element_type=F32)


def _split(a):
    hi = a.astype(BF16)
    lo = (a - hi.astype(F32)).astype(BF16)
    return hi, lo


def _dot3(a, b):
    a_hi, a_lo = _split(a)
    b_hi, b_lo = _split(b)
    return _dot(a_hi, b_hi) + _dot(a_lo, b_hi) + _dot(a_hi, b_lo)


def _norm_mod(x, gain, shift, scale):
    ms = jnp.mean(x * x, axis=-1, keepdims=True)
    return (x * lax.rsqrt(ms + EPS)) * (gain * (1.0 + scale)) + shift


def _params(*sem):
    return pltpu.CompilerParams(dimension_semantics=sem, vmem_limit_bytes=VMEM_LIMIT)


def _resident(shape):
    nd = len(shape)
    return pl.BlockSpec(shape, lambda *_: (0,) * nd, pipeline_mode=pl.Buffered(1))


def _ada_kernel(cond_ref, w_ref, b_ref, o_ref):
    a = cond_ref[...]
    a = a * jax.nn.sigmoid(a)
    o_ref[0] = _dot3(a, w_ref[0]) + b_ref[0]


def _ada_mod(cond, w_ada, b_ada):
    depth, d, nd = w_ada.shape
    rows = cond.shape[0]
    tn = nd // 8
    return pl.pallas_call(
        _ada_kernel,
        grid=(depth, nd // tn),
        in_specs=[
            pl.BlockSpec((rows, d), lambda l, j: (0, 0)),
            pl.BlockSpec((1, d, tn), lambda l, j: (l, 0, j)),
            pl.BlockSpec((1, 1, tn), lambda l, j: (l, 0, j)),
        ],
        out_specs=pl.BlockSpec((1, rows, tn), lambda l, j: (l, 0, j)),
        out_shape=jax.ShapeDtypeStruct((depth, rows, nd), F32),
        compiler_params=_params("parallel", "parallel"),
        name="ada_mod",
    )(cond, w_ada, b_ada.reshape(depth, 1, nd))


def _ffn_kernel(*refs, row0, grow, n_mix, emit_rstd):
    x_ref, mod_ref, gain_ref, w1_ref, w3_ref, w2_ref = refs[:6]
    mix_refs = refs[6:6 + 2 * n_mix]
    out_refs = refs[6 + 2 * n_mix:-1]
    acc_ref = refs[-1]
    o_ref = out_refs[0]
    x = x_ref[0]
    if n_mix:
        y = _dot_w(mix_refs[0][0], mix_refs[1][...])
        for k in range(1, n_mix):
            y = y + _dot_w(mix_refs[2 * k][0], mix_refs[2 * k + 1][...])
        x = x + mod_ref[0, 5:6, :] * y
        o_ref[0] = x
    shift = mod_ref[0, row0:row0 + 1, :]
    scale = mod_ref[0, row0 + 1:row0 + 2, :]
    gate = mod_ref[0, row0 + 2:row0 + 3, :]
    h = _norm_mod(x, gain_ref[grow:grow + 1, :], shift, scale)
    fc = FFN_CHUNK
    for c in range(w1_ref.shape[1] // fc):
        a = _dot_w(h, w1_ref[:, c * fc:(c + 1) * fc])
        b = _dot_w(h, w3_ref[:, c * fc:(c + 1) * fc])
        g = a * jax.nn.sigmoid(a) * b
        part = _dot_w(g, w2_ref[c * fc:(c + 1) * fc, :])
        if c == 0:
            acc_ref[...] = part
        else:
            acc_ref[...] += part
    base = o_ref[0] if n_mix else x_ref[0]
    o = base + (0.5 * gate) * acc_ref[...]
    o_ref[0] = o
    if emit_rstd:
        rstd = lax.rsqrt(jnp.mean(o * o, axis=-1, keepdims=True) + EPS)
        out_refs[1][0] = jnp.broadcast_to(rstd, out_refs[1].shape[1:])


def _picked(arr, lead, shape, start=None):
    start = start or (0,) * len(shape)
    return pl.BlockSpec((None,) * len(lead) + tuple(shape), lambda *_: tuple(lead) + tuple(start),
                        pipeline_mode=pl.Buffered(1))


def _half_ffn(s, mod, row0, gains, grow, ffn_w, lk, mod_index=None, mix=(), emit_rstd=False, tm=512):
    bs, ls, d = s.shape
    tm = min(tm, ls)
    if mod_index is None:
        mod_map = lambda b, i: (b, 0, 0)
    else:
        mod_map = lambda b, i: (mod_index, 0, 0)
    tok = lambda w: pl.BlockSpec((1, tm, w), lambda b, i: (b, i, 0))
    in_specs = [tok(d), pl.BlockSpec((1, N_MOD, d), mod_map), _resident(gains.shape)]
    in_specs += [_picked(w, lk, w.shape[2:]) for w in ffn_w]
    args = [s, mod, gains, *ffn_w]
    for z, w, w_spec in mix:
        in_specs += [tok(z.shape[2]), w_spec]
        args += [z, w]
    out_specs = [tok(d)]
    out_shape = [jax.ShapeDtypeStruct(s.shape, F32)]
    if emit_rstd:
        out_specs.append(tok(LANES))
        out_shape.append(jax.ShapeDtypeStruct((bs, ls, LANES), F32))
    outs = pl.pallas_call(
        functools.partial(_ffn_kernel, row0=row0, grow=grow, n_mix=len(mix), emit_rstd=emit_rstd),
        grid=(bs, ls // tm),
        in_specs=in_specs,
        out_specs=out_specs,
        out_shape=out_shape,
        scratch_shapes=[pltpu.VMEM((tm, d), F32)],
        compiler_params=_params("parallel", "parallel"),
        name="half_ffn",
    )(*args)
    return outs if emit_rstd else outs[0]


def _head_norm(t, bd, gain):
    t2 = (t * t).astype(BF16)
    slab = min(t.shape[1], 2 * LANES)
    sums = [_dot(t2[:, s:s + slab], bd[:slab, :slab]) for s in range(0, t.shape[1], slab)]
    ms = (jnp.concatenate(sums, axis=1) if len(sums) > 1 else sums[0]) * (1.0 / HEAD_DIM)
    return t * lax.rsqrt(ms + EPS) * gain


def _rope(t, cos, sin):
    w = t.shape[1]
    half = HEAD_DIM // 2
    lane = lax.broadcasted_iota(jnp.int32, t.shape, 1)
    first = (lane % HEAD_DIM) < half
    rot = jnp.where(first, pltpu.roll(t, w - half, 1), pltpu.roll(t, half, 1))
    reps = w // LANES
    c = jnp.concatenate([cos] * reps, axis=1) if reps > 1 else cos
    s = jnp.concatenate([sin] * reps, axis=1) if reps > 1 else sin
    return t * c + rot * s


def _dup_heads(t):
    lane = lax.broadcasted_iota(jnp.int32, t.shape, 1)
    low = lane < HEAD_DIM
    r = pltpu.roll(t, HEAD_DIM, 1)
    return jnp.concatenate([jnp.where(low, t, r), jnp.where(low, r, t)], axis=1)


def _inproj_kernel(x_ref, mod_ref, gain_ref, wqkv_ref, wut_ref, bd_ref, qg_ref, kg_ref, cos_ref, sin_ref,
                   q_ref, kk_ref, vv_ref, ut_ref, *flat):
    x = x_ref[0]
    h = _norm_mod(x, gain_ref[1:2, :], mod_ref[0, 3:4, :], mod_ref[0, 4:5, :]).astype(BF16)
    qkv = _dot_w(h, wqkv_ref[...])
    q = qkv[:, :Q_WIDTH]
    k = qkv[:, Q_WIDTH:Q_WIDTH + KV_WIDTH]
    v = qkv[:, Q_WIDTH + KV_WIDTH:]
    cos = cos_ref[...]
    sin = sin_ref[...]
    q = _rope(_head_norm(q, bd_ref[...], qg_ref[...]), cos, sin)
    k = _rope(_head_norm(k, bd_ref[:KV_WIDTH, :KV_WIDTH], kg_ref[...]), cos, sin)
    q_ref[0] = (q * (HEAD_DIM ** -0.5)).astype(BF16)
    kk_ref[0] = _dup_heads(k).astype(BF16)
    vv_ref[0] = _dup_heads(v).astype(BF16)
    ut = _dot_nt(wut_ref[...], h)
    if not flat:
        ut_ref[...] = ut
        return
    flat_ref, = flat
    nr = ut_ref.shape[1]
    for j in range(nr):
        flat_ref[pl.ds(j, SSM_WIDTH, stride=nr), :] = ut[:, j * SSM_ROW:(j + 1) * SSM_ROW]

    ut_ref[...] = flat_ref[...].reshape(ut_ref.shape)


def _in_proj(s, mod, gains, wqkv, wut, bd, qg, kg, cos, sin, mod_index=None, tm=1024):
    bs, ls, d = s.shape
    tm = min(tm, ls)
    nt = ls // tm
    nr = tm // SSM_ROW
    by_rows = nr % 8 == 0
    if by_rows:
        ut_spec = pl.BlockSpec((SSM_WIDTH, nr, SSM_ROW), lambda b, i: (0, b * nt + i, 0))
        ut_shape = jax.ShapeDtypeStruct((SSM_WIDTH, bs * ls // SSM_ROW, SSM_ROW), F32)
        scratch = [pltpu.VMEM((SSM_WIDTH * nr, SSM_ROW), F32)]
    else:
        ut_spec = pl.BlockSpec((SSM_WIDTH, tm), lambda b, i: (0, b * nt + i))
        ut_shape = jax.ShapeDtypeStruct((SSM_WIDTH, bs * ls), F32)
        scratch = []
    if mod_index is None:
        mod_map = lambda b, i: (b, 0, 0)
    else:
        mod_map = lambda b, i: (mod_index, 0, 0)
    tok = lambda w: pl.BlockSpec((1, tm, w), lambda b, i: (b, i, 0))
    return pl.pallas_call(
        _inproj_kernel,
        grid=(bs, nt),
        in_specs=[
            tok(d),
            pl.BlockSpec((1, N_MOD, d), mod_map),
            _resident(gains.shape),
            _resident(wqkv.shape),
            _resident(wut.shape),
            _resident(bd.shape),
            _resident(qg.shape),
            _resident(kg.shape),
            pl.BlockSpec((tm, LANES), lambda b, i: (i, 0)),
            pl.BlockSpec((tm, LANES), lambda b, i: (i, 0)),
        ],
        out_specs=[tok(Q_WIDTH), tok(2 * KV_WIDTH), tok(2 * KV_WIDTH), ut_spec],
        out_shape=[
            jax.ShapeDtypeStruct((bs, ls, Q_WIDTH), BF16),
            jax.ShapeDtypeStruct((bs, ls, 2 * KV_WIDTH), BF16),
            jax.ShapeDtypeStruct((bs, ls, 2 * KV_WIDTH), BF16),
            ut_shape,
        ],
        scratch_shapes=scratch,
        compiler_params=_params("parallel", "parallel"),
        name="in_proj",
    )(s, mod, gains, wqkv, wut, bd, qg, kg, cos, sin)


def _rope_tables(n_lat):
    rows = n_lat // GRID_W
    row = jnp.repeat(jnp.arange(rows, dtype=F32), GRID_W)
    col = jnp.tile(jnp.arange(GRID_W, dtype=F32), rows)
    n_freq = HEAD_DIM // 4
    inv_freq = 1.0 / (ROPE_BASE ** (jnp.arange(n_freq, dtype=F32) / n_freq))
    ang = jnp.concatenate([row[:, None] * inv_freq, col[:, None] * inv_freq], axis=-1)
    cos, sin = jnp.cos(ang), jnp.sin(ang)
    cos_t = jnp.tile(cos, (1, LANES // (HEAD_DIM // 2)))
    sin_t = jnp.tile(jnp.concatenate([-sin, sin], axis=-1), (1, LANES // HEAD_DIM))
    return cos_t, sin_t


def _attn_kernel(sink_ref, q_ref, kk_ref, vv_ref, kkc_ref, vvc_ref, o_ref):
    blk = ATT_BLOCK
    per_step = q_ref.shape[1] // blk
    nb = pl.num_programs(1) * per_step
    for k in range(per_step):
        o_ref[0, k * blk:(k + 1) * blk, :] = _attn_block(
            pl.program_id(1) * per_step + k, nb, q_ref[0, k * blk:(k + 1) * blk, :],
            sink_ref, kk_ref, vv_ref, kkc_ref, vvc_ref)


def _attn_block(i, nb, q, sink_ref, kk_ref, vv_ref, kkc_ref, vvc_ref):
    blk = ATT_BLOCK
    lane = lax.broadcasted_iota(jnp.int32, (blk, LANES), 1)
    low = lane < HEAD_DIM
    rel = lane - lax.broadcasted_iota(jnp.int32, (blk, LANES), 0)
    ok_prev = rel >= jnp.where(i > 0, 0, 2 * blk)
    ok_next = rel <= jnp.where(i < nb - 1, 0, -2 * blk)
    j_prev = jnp.maximum(i - 1, 0)
    j_next = jnp.minimum(i + 1, nb - 1)
    zero = jnp.zeros_like(q[:, :LANES])
    cols = []
    for h in range(N_KV_HEADS):
        hs = slice(h * LANES, (h + 1) * LANES)

        def rows(ref, j):
            return ref[0, pl.ds(pl.multiple_of(j * blk, blk), blk), hs]

        keys = jnp.concatenate([rows(kk_ref, j_prev), rows(kk_ref, i), rows(kk_ref, j_next), kkc_ref[0, :, hs]], axis=0)
        vals = jnp.concatenate([rows(vv_ref, j_prev), rows(vv_ref, i), rows(vv_ref, j_next), vvc_ref[0, :, hs]], axis=0)
        qs = []
        for col in (2 * h, 2 * h + 1):
            qc = q[:, col * LANES:(col + 1) * LANES]
            qs.append(jnp.where(low, qc, zero))
            qs.append(jnp.where(low, zero, qc))
        s_all = _dot_nt(jnp.concatenate(qs, axis=0), keys)
        low_k = lax.broadcasted_iota(jnp.int32, vals.shape, 1) < HEAD_DIM
        v_ones = (jnp.where(low_k, vals, jnp.ones_like(vals)), jnp.where(low_k, jnp.ones_like(vals), vals))
        ps, tail = [], []
        for g in range(4):
            s = s_all[g * blk:(g + 1) * blk]
            parts = [jnp.where(ok_prev, s[:, :blk], NEG_INF), s[:, blk:2 * blk],
                     jnp.where(ok_next, s[:, 2 * blk:3 * blk], NEG_INF)]
            parts += [s[:, j * blk:(j + 1) * blk] for j in range(3, s.shape[1] // blk)]
            peak = parts[0]
            for part in parts[1:]:
                peak = jnp.maximum(peak, part)
            sink = sink_ref[4 * h + g]
            m = jnp.maximum(jnp.max(peak, axis=-1, keepdims=True), sink)
            ps.append(jnp.exp(jnp.concatenate([part - m for part in parts], axis=1).astype(BF16)))
            tail.append(jnp.exp(sink - m))
        og = []
        for par in range(2):
            o = _dot(jnp.concatenate([ps[par], ps[2 + par]], axis=0), v_ones[par])
            for k in range(2):
                ok = o[k * blk:(k + 1) * blk]
                den = (ok[:, HEAD_DIM:HEAD_DIM + 1] if par == 0 else ok[:, 0:1]) + tail[2 * k + par]
                og.append(ok * (1.0 / den))
        cols.append(jnp.where(low, og[0], og[2]))
        cols.append(jnp.where(low, og[1], og[3]))
    return jnp.concatenate(cols, axis=1).astype(BF16)


def _attention(q, kk, vv, kkc, vvc, sink):
    bs, ls, _ = q.shape
    tq = 8 * ATT_BLOCK
    n_ctx = kkc.shape[1]
    full = lambda n: pl.BlockSpec((1, n, 2 * KV_WIDTH), lambda b, i: (b, 0, 0))
    return pl.pallas_call(
        _attn_kernel,
        grid=(bs, ls // tq),
        in_specs=[
            pl.BlockSpec(memory_space=pltpu.SMEM),
            pl.BlockSpec((1, tq, Q_WIDTH), lambda b, i: (b, i, 0)),
            full(ls), full(ls), full(n_ctx), full(n_ctx),
        ],
        out_specs=pl.BlockSpec((1, tq, Q_WIDTH), lambda b, i: (b, i, 0)),
        out_shape=jax.ShapeDtypeStruct((bs, ls, Q_WIDTH), BF16),
        compiler_params=_params("parallel", "arbitrary"),
        name="window_attention",
    )(sink, q, kk, vv, kkc, vvc)


def _cpow(lr, li, dt, tau):
    mag = jnp.exp(lr * dt * tau)
    ang = li * dt * tau
    return mag * jnp.cos(ang), mag * jnp.sin(ang)


def _ssm_prep_kernel(lrr_ref, lir_ref, lrc_ref, lic_ref, ldt_ref, btr_ref, bti_ref, btrr_ref, btri_ref,
                     crr_ref, cri_ref, ctr_ref, cti_ref, kcat_ref, wst_ref, wcar_ref, arow_ref):
    p = SSM_STATE
    t = SSM_CHUNK
    lane_t = lax.broadcasted_iota(jnp.int32, (1, t), 1).astype(F32)
    sub_t = lax.broadcasted_iota(jnp.int32, (t, 1), 0).astype(F32)
    kparts = []
    for d in range(2):
        dt = jnp.exp(ldt_ref[0, d])
        lr, li = lrr_ref[0, d], lir_ref[0, d]
        lrc, lic = lrc_ref[0, d], lic_ref[0, d]
        ar, ai = _cpow(lr, li, dt, 1.0)
        nr = ar - 1.0
        den = lr * lr + li * li
        fr = (nr * lr + ai * li) / den
        fi = (ai * lr - nr * li) / den
        bt_r, bt_i = btr_ref[0, d], bti_ref[0, d]
        bbr = fr * bt_r - fi * bt_i
        bbi = fr * bt_i + fi * bt_r
        rep_r, rep_i = btrr_ref[0, d], btri_ref[0, d]
        bbr_rep = fr * rep_r - fi * rep_i
        bbi_rep = fr * rep_i + fi * rep_r
        c_r, c_i = crr_ref[0, d], cri_ref[0, d]
        m1 = c_r * bbr_rep - c_i * bbi_rep
        m2 = c_r * bbi_rep + c_i * bbr_rep
        tau = lane_t if d == 0 else (float(t) - lane_t)
        e_r, e_i = _cpow(lrc, lic, dt, tau)
        kmat = _dot3(m1, e_r) - _dot3(m2, e_i)
        if d == 0:
            kparts.append(kmat)
        else:
            lag0 = jnp.sum(m1, axis=1, keepdims=True)
            first = lax.broadcasted_iota(jnp.int32, (1, t), 1) == 0
            kparts[0] = kparts[0] + jnp.where(first, lag0, 0.0)
            kparts.insert(0, kmat)
        tau_s = (float(t - 1) - sub_t) if d == 0 else sub_t
        et_r, et_i = _cpow(lr, li, dt, tau_s)
        for i in range(SSM_GROUP):
            re = et_r * bbr[i:i + 1] - et_i * bbi[i:i + 1]
            im = et_r * bbi[i:i + 1] + et_i * bbr[i:i + 1]
            wst_ref[0, i * t:(i + 1) * t, d * 2 * p:(d + 1) * 2 * p] = jnp.concatenate([re, im], axis=1).astype(BF16)
        if d == 0:
            ec_r, ec_i = _cpow(lrc, lic, dt, lane_t + 1.0)
        else:
            ec_r, ec_i = e_r, e_i
        ct_r, ct_i = ctr_ref[0, d], cti_ref[0, d]
        car_re, car_im = [], []
        for jo in range(SSM_GROUP):
            cr = ct_r[:, jo:jo + 1]
            ci = ct_i[:, jo:jo + 1]
            car_re.append(cr * ec_r - ci * ec_i)
            car_im.append(-(cr * ec_i + ci * ec_r))
        wcar_ref[0, d * 2 * p:d * 2 * p + p, :] = jnp.concatenate(car_re, axis=1).astype(BF16)
        wcar_ref[0, d * 2 * p + p:(d + 1) * 2 * p, :] = jnp.concatenate(car_im, axis=1).astype(BF16)
        at_r, at_i = _cpow(lr, li, dt, float(t))
        arow_ref[0, d, 0:1, :] = jnp.concatenate([at_r, at_r], axis=1)
        arow_ref[0, d, 1:2, :] = jnp.concatenate([-at_i, at_i], axis=1)
    kcat_ref[0] = jnp.concatenate(kparts, axis=1)


def _ssm_prep(lam_re, lam_im, log_dt, b_re, b_im, c_re, c_im):
    g, p, n, t = SSM_GROUPS, SSM_STATE, SSM_GROUP, SSM_CHUNK
    gd = lambda a: jnp.swapaxes(a, 0, 1)
    lrr = gd(lam_re)[:, :, None, :]
    lir = gd(lam_im)[:, :, None, :]
    lrc = gd(lam_re)[:, :, :, None]
    lic = gd(lam_im)[:, :, :, None]
    ldt = gd(log_dt)[:, :, None, None]
    bt_r = jnp.swapaxes(gd(b_re), 2, 3)
    bt_i = jnp.swapaxes(gd(b_im), 2, 3)
    btr_r = jnp.tile(bt_r, (1, 1, n, 1))
    btr_i = jnp.tile(bt_i, (1, 1, n, 1))
    cr_r = jnp.repeat(gd(c_re), n, axis=2)
    cr_i = jnp.repeat(gd(c_im), n, axis=2)
    ct_r = jnp.swapaxes(gd(c_re), 2, 3)
    ct_i = jnp.swapaxes(gd(c_im), 2, 3)
    args = (lrr, lir, lrc, lic, ldt, bt_r, bt_i, btr_r, btr_i, cr_r, cr_i, ct_r, ct_i)
    spec = lambda a: pl.BlockSpec((1,) + a.shape[1:], lambda i: (i,) + (0,) * (a.ndim - 1))
    return pl.pallas_call(
        _ssm_prep_kernel,
        grid=(g,),
        in_specs=[spec(a) for a in args],
        out_specs=[
            pl.BlockSpec((1, n * n, 2 * t), lambda i: (i, 0, 0)),
            pl.BlockSpec((1, n * t, 4 * p), lambda i: (i, 0, 0)),
            pl.BlockSpec((1, 4 * p, n * t), lambda i: (i, 0, 0)),
            pl.BlockSpec((1, 2, 2, 2 * p), lambda i: (i, 0, 0, 0)),
        ],
        out_shape=[
            jax.ShapeDtypeStruct((g, n * n, 2 * t), F32),
            jax.ShapeDtypeStruct((g, n * t, 4 * p), BF16),
            jax.ShapeDtypeStruct((g, 4 * p, n * t), BF16),
            jax.ShapeDtypeStruct((g, 2, 2, 2 * p), F32),
        ],
        compiler_params=_params("parallel"),
        name="ssm_prep",
    )(*args)


def _halves_to_rows(a, b, low):
    return jnp.where(low, a, pltpu.roll(b, SSM_CHUNK, 1)), jnp.where(low, pltpu.roll(a, SSM_CHUNK, 1), b)


def _ssm_kernel(u_ref, uc_ref, kcat_ref, wst_ref, wcar_ref, arow_ref, d_ref, y_ref,
                tz_ref, s_ref, sc_ref, h_ref, *, nb, nc, ncc):
    n, t, p = SSM_GROUP, SSM_CHUNK, SSM_STATE
    rows, rows_c = u_ref.shape[1], uc_ref.shape[1]
    low = lax.broadcasted_iota(jnp.int32, (1, SSM_ROW), 1) < t

    def build(i, carry):
        for m in range(n // 2):
            even = jnp.broadcast_to(kcat_ref[0, pl.ds(2 * m * n + i, 1), :], (t, SSM_ROW))
            odd = jnp.broadcast_to(kcat_ref[0, pl.ds((2 * m + 1) * n + i, 1), :], (t, SSM_ROW))
            even = pltpu.roll(even, t, 1, stride=1, stride_axis=0)
            odd = pltpu.roll(odd, 0, 1, stride=1, stride_axis=0)
            tz_ref[pl.ds(pl.multiple_of(i * t, t), t), m * SSM_ROW:(m + 1) * SSM_ROW] = jnp.where(low, even, odd).astype(BF16)
        return carry

    lax.fori_loop(0, n, build, 0, unroll=True)

    def chunk_rows(ref):
        first, second = [], []
        for m in range(n // 2):
            a, b = _halves_to_rows(ref[2 * m], ref[2 * m + 1], low)
            first.append(a.astype(BF16))
            second.append(b.astype(BF16))
        return jnp.concatenate([jnp.concatenate(first, axis=1), jnp.concatenate(second, axis=1)], axis=0)

    x = chunk_rows(u_ref)
    xc = chunk_rows(uc_ref)
    s_all = _dot(x, wst_ref[0])
    sc_all = _dot(xc, wst_ref[0])
    for d in range(2):
        s_ref[d, 0] = s_all[:, d * 2 * p:(d + 1) * 2 * p]
        s_ref[d, 1] = pltpu.roll(s_all[:, d * 2 * p:(d + 1) * 2 * p], p, 1)
        sc_ref[d, 0] = sc_all[:, d * 2 * p:(d + 1) * 2 * p]
        sc_ref[d, 1] = pltpu.roll(sc_all[:, d * 2 * p:(d + 1) * 2 * p], p, 1)

    coef = [(arow_ref[0, d, 0:1, :], arow_ref[0, d, 1:2, :]) for d in range(2)]

    def advance(d, h, hs, s, ss):
        a1, a2 = coef[d]
        return a1 * h + a2 * hs + s, a1 * hs - a2 * h + ss

    state = []
    for d in range(2):
        h = hs = jnp.zeros((nb, 2 * p), F32)
        order = range(2 * ncc) if d == 0 else reversed(range(2 * ncc))
        for chunk in order:
            sel = pl.ds((chunk % 2) * rows_c + chunk // 2, nb, stride=ncc)
            h, hs = advance(d, h, hs, sc_ref[d, 0, sel, :], sc_ref[d, 1, sel, :])
        state += [h, hs]

    def scan(k, carry):
        out = []
        for d in range(2):
            h, hs = carry[2 * d], carry[2 * d + 1]
            row = k if d == 0 else nc - 1 - k
            for half in ((0, 1) if d == 0 else (1, 0)):
                sel = pl.ds(half * rows + row, nb, stride=nc)
                h_ref[d, sel, :] = h
                h, hs = advance(d, h, hs, s_ref[d, 0, sel, :], s_ref[d, 1, sel, :])
            out += [h, hs]
        return tuple(out)

    lax.fori_loop(0, nc, scan, tuple(state), unroll=True)

    hb = jnp.concatenate([h_ref[0], h_ref[1]], axis=1).astype(BF16)
    nblk = 2 * SSM_ROW
    for q in range(n * t // nblk):
        y = _dot(x, tz_ref[:, q * nblk:(q + 1) * nblk]) + _dot(hb, wcar_ref[0, :, q * nblk:(q + 1) * nblk])
        for mm in range(2):
            m = 2 * q + mm
            even, odd = _halves_to_rows(y[:rows, mm * SSM_ROW:(mm + 1) * SSM_ROW],
                                        y[rows:, mm * SSM_ROW:(mm + 1) * SSM_ROW], low)
            for jo, val in ((2 * m, even), (2 * m + 1, odd)):
                y_ref[:, jo] = (val + d_ref[jo] * u_ref[jo]).reshape(rows // 8, 8, SSM_ROW)


def _ssm(ut, utc, kcat, wst, wcar, arow, d_skip, nb):
    n, t, p, g = SSM_GROUP, SSM_CHUNK, SSM_STATE, SSM_GROUPS
    rows = ut.shape[1]
    rows_c = utc.shape[1]
    nc, ncc = rows // nb, rows_c // nb
    d3 = jnp.broadcast_to(d_skip.astype(F32)[:, None, None], (SSM_WIDTH, 1, SSM_ROW))
    return pl.pallas_call(
        functools.partial(_ssm_kernel, nb=nb, nc=nc, ncc=ncc),
        grid=(g,),
        in_specs=[
            pl.BlockSpec((n, rows, SSM_ROW), lambda i: (i, 0, 0)),
            pl.BlockSpec((n, rows_c, SSM_ROW), lambda i: (i, 0, 0)),
            pl.BlockSpec((1, n * n, 2 * t), lambda i: (i, 0, 0)),
            pl.BlockSpec((1, n * t, 4 * p), lambda i: (i, 0, 0)),
            pl.BlockSpec((1, 4 * p, n * t), lambda i: (i, 0, 0)),
            pl.BlockSpec((1, 2, 2, 2 * p), lambda i: (i, 0, 0, 0)),
            pl.BlockSpec((n, 1, SSM_ROW), lambda i: (i, 0, 0)),
        ],
        out_specs=pl.BlockSpec((rows // 8, n, 8, SSM_ROW), lambda i: (0, i, 0, 0)),
        out_shape=jax.ShapeDtypeStruct((rows // 8, SSM_WIDTH, 8, SSM_ROW), F32),
        scratch_shapes=[
            pltpu.VMEM((n * t, n * t), BF16),
            pltpu.VMEM((2, 2, 2 * rows, 2 * p), F32),
            pltpu.VMEM((2, 2, 2 * rows_c, 2 * p), F32),
            pltpu.VMEM((2, 2 * rows, 2 * p), F32),
        ],
        compiler_params=_params("parallel"),
        name="ssm_toeplitz",
    )(ut, utc, kcat, wst, wcar, arow, d3)


def _glu_kernel(yt_ref, wglut_ref, s_ref, flat_ref):
    t = SSM_ROW
    nck = yt_ref.shape[2]

    flat_ref[...] = yt_ref[0].reshape(flat_ref.shape)
    for j in range(0, nck, 2):
        g = jnp.concatenate([jax.nn.gelu(flat_ref[pl.ds(j + k, SSM_WIDTH, stride=nck), :]) for k in range(2)],
                            axis=1)
        z = _dot(wglut_ref[...], g.astype(BF16))
        gate = 0.5 * (1.0 + jnp.tanh(0.5 * z))
        s_ref[j * t:(j + 2) * t, :] = (g * gate).T.astype(BF16)


def _ssm_glu(yt, wglut, bs, ls):
    nck = yt.shape[2]
    rows = yt.shape[0] * nck
    s = pl.pallas_call(
        _glu_kernel,
        grid=(rows // nck,),
        in_specs=[
            pl.BlockSpec((1, SSM_WIDTH, nck, SSM_ROW), lambda i: (i, 0, 0, 0)),
            _resident(wglut.shape),
        ],
        out_specs=pl.BlockSpec((nck * SSM_ROW, SSM_WIDTH), lambda i: (i, 0)),
        out_shape=jax.ShapeDtypeStruct((rows * SSM_ROW, SSM_WIDTH), BF16),
        scratch_shapes=[pltpu.VMEM((SSM_WIDTH * nck, SSM_ROW), F32)],
        compiler_params=_params("parallel"),
        name="ssm_glu",
    )(yt, wglut)
    return s.reshape(bs, ls, SSM_WIDTH)


def _fourier_kernel(x_ref, r_ref, mod_ref, gain_ref, lhs1_ref, twc_ref, tws_ref, lhs2_ref, cs_ref, z_ref,
                    h_ref, are_ref, aim_ref, y_ref):
    r, pitch = FFT_RADIX, FFT_PITCH
    halves = FOURIER_GROUP_WIDTH // LANES

    def slab(i):
        return pl.ds(pl.multiple_of(i * pitch, 8), r)

    def gather(ref, i):
        return jnp.concatenate([ref[hf, pl.ds(i, r, stride=pitch), :] for hf in range(halves)], axis=1)

    def put(ref, i, val):
        for hf in range(halves):
            ref[hf, slab(i), :] = val[:, hf * LANES:(hf + 1) * LANES]

    gain = gain_ref[1:2, :]
    shift, scale = mod_ref[0, 3:4, :], mod_ref[0, 4:5, :]

    gs = gain * (1.0 + scale)

    def fill(n1, carry):
        rows = pl.ds(pl.multiple_of(n1 * r, r), r)
        rstd = r_ref[0, rows, :]
        for hf in range(halves):
            cols = slice(hf * LANES, (hf + 1) * LANES)
            h_ref[hf, slab(n1), :] = (x_ref[0, rows, cols] * rstd) * gs[:, cols] + shift[:, cols]
        return carry

    lax.fori_loop(0, r, fill, 0, unroll=8)

    def stage1(jg, carry):
        cols = [gather(h_ref, jg * 4 + j).astype(BF16) for j in range(4)]
        a = _dot(lhs1_ref[...], jnp.concatenate(cols, axis=0))
        ch, sh = a[:4 * r], a[4 * r:]
        trow = pl.ds(pl.multiple_of(jg * 4 * r, 4 * r), 4 * r)
        tc = jnp.concatenate([twc_ref[trow, :]] * halves, axis=1)
        ts = jnp.concatenate([tws_ref[trow, :]] * halves, axis=1)
        a_re = ch * tc - sh * ts
        a_im = sh * tc + ch * ts
        for j in range(4):
            put(are_ref, jg * 4 + j, a_re[j * r:(j + 1) * r])
            put(aim_ref, jg * 4 + j, a_im[j * r:(j + 1) * r])
        return carry

    lax.fori_loop(0, r // 4, stage1, 0, unroll=True)

    def stage2(kg, carry):
        for j in range(4):
            k1 = kg * 4 + j
            st = jnp.concatenate([gather(are_ref, k1), gather(aim_ref, k1)], axis=0).astype(BF16)
            y = _dot(lhs2_ref[...], st)
            rows = pl.ds(pl.multiple_of(k1 * r, r), r)
            y_ref[rows, :] = jnp.concatenate([y[:r], y[r:]], axis=1).astype(BF16)
        return carry

    lax.fori_loop(0, r // 4, stage2, 0, unroll=True)

    def channels(kb, carry):
        rows = pl.ds(pl.multiple_of(kb * 8 * r, 8 * r), 8 * r)
        z = _dot(y_ref[rows, :], cs_ref[...])
        for j in range(8):
            put(h_ref, kb * 8 + j, z[j * r:(j + 1) * r])
        return carry

    lax.fori_loop(0, r // 8, channels, 0, unroll=True)

    def emit(k2, carry):
        z_ref[0, pl.ds(pl.multiple_of(k2 * r, r), r), :] = gather(h_ref, k2).astype(BF16)
        return carry

    lax.fori_loop(0, r, emit, 0, unroll=8)


def _fourier_tables():
    r = FFT_RADIX
    n = r * r
    k = jnp.arange(r, dtype=jnp.int32)
    ang = ((k[:, None] * k[None, :]) % r).astype(F32) * (2.0 * math.pi / r)
    wc = jnp.cos(ang) / 8.0
    ws = jnp.sin(ang) / 8.0
    eye4 = jnp.eye(4, dtype=F32)
    bc, bs = jnp.kron(eye4, wc), jnp.kron(eye4, ws)
    lhs1 = jnp.concatenate([bc, bs], axis=0).astype(BF16)
    lhs2 = jnp.concatenate([jnp.concatenate([wc, -ws], axis=1),
                            jnp.concatenate([-ws, -wc], axis=1)], axis=0).astype(BF16)
    tw = ((k[:, None] * k[None, :]) % n).astype(F32) * (2.0 * math.pi / n)
    twc = jnp.broadcast_to(jnp.cos(tw).reshape(n, 1), (n, LANES))
    tws = jnp.broadcast_to(jnp.sin(tw).reshape(n, 1), (n, LANES))
    m = jnp.arange(FOURIER_GROUP_WIDTH, dtype=jnp.int32)
    angc = ((m[:, None] * m[None, :]) % FOURIER_GROUP_WIDTH).astype(F32) * (2.0 * math.pi / FOURIER_GROUP_WIDTH)
    cs = jnp.concatenate([jnp.cos(angc), jnp.sin(angc)], axis=0) / 16.0
    return lhs1, lhs2, twc, tws, cs.astype(BF16)


def _fourier(x, rstd, mod, gains):
    bs, ls, d = x.shape
    r, gw = FFT_RADIX, FOURIER_GROUP_WIDTH
    assert ls == r * r
    lhs1, lhs2, twc, tws, cs = _fourier_tables()
    scratch = pltpu.VMEM((gw // LANES, r * FFT_PITCH, LANES), F32)
    return pl.pallas_call(
        _fourier_kernel,
        grid=(bs, d // gw),
        in_specs=[
            pl.BlockSpec((1, ls, gw), lambda b, g: (b, 0, g)),
            pl.BlockSpec((1, ls, LANES), lambda b, g: (b, 0, 0)),
            pl.BlockSpec((1, N_MOD, gw), lambda b, g: (b, 0, g)),
            pl.BlockSpec((gains.shape[0], gw), lambda b, g: (0, g)),
            _resident(lhs1.shape), _resident(twc.shape), _resident(tws.shape),
            _resident(lhs2.shape), _resident(cs.shape),
        ],
        out_specs=pl.BlockSpec((1, ls, gw), lambda b, g: (b, 0, g)),
        out_shape=jax.ShapeDtypeStruct((bs, ls, d), BF16),
        scratch_shapes=[scratch, scratch, scratch, pltpu.VMEM((ls, 2 * gw), BF16)],
        compiler_params=_params("parallel", "parallel"),
        name="fourier_mix",
    )(x, rstd, mod, gains, lhs1, twc, tws, lhs2, cs)


def _even_layer(x, s_ctx, mod, gains, ffn_w, layer, w_in, q_gain, k_gain, sink, ssm, d_skip, w_glu, w_out, e):
    bs, ls, d = x.shape
    ctx_index = bs
    x = _half_ffn(x, mod, 0, gains, 0, ffn_w, (layer, 0))
    n_ctx = s_ctx.shape[1]
    s_ctx = _half_ffn(s_ctx.reshape(1, bs * n_ctx, d), mod, 0, gains, 0, ffn_w, (layer, 0),
                      mod_index=ctx_index).reshape(bs, n_ctx, d)

    wqkv = w_in[:, :Q_WIDTH + 2 * KV_WIDTH]
    wut = w_in[:, Q_WIDTH + 2 * KV_WIDTH:].T.astype(BF16)
    bd = jnp.kron(jnp.eye(N_Q_HEADS, dtype=F32), jnp.ones((HEAD_DIM, HEAD_DIM), F32)).astype(BF16)
    qg = jnp.tile(q_gain.astype(F32), N_Q_HEADS)[None, :]
    kg = jnp.tile(k_gain.astype(F32), N_KV_HEADS)[None, :]
    cos, sin = _rope_tables(ls)
    n_ctx = s_ctx.shape[1]
    one = jnp.ones((n_ctx, LANES), F32)
    q, kk, vv, ut = _in_proj(x, mod, gains, wqkv, wut, bd, qg, kg, cos, sin)
    _, kkc, vvc, utc = _in_proj(s_ctx, mod, gains, wqkv, wut, bd, qg, kg, one, 0.0 * one, mod_index=ctx_index)

    attn = _attention(q, kk, vv, kkc, vvc, sink.astype(F32))

    t = SSM_ROW
    kcat, wst, wcar, arow = _ssm_prep(*ssm)
    yt = _ssm(ut, utc.reshape(SSM_WIDTH, bs * n_ctx // t, t), kcat, wst, wcar, arow, d_skip, bs)

    s = _ssm_glu(yt, w_glu.T.astype(BF16), bs, ls)
    mix = ((attn, w_out, _picked(w_out, (e,), (Q_WIDTH, d), (0, 0))),
           (s, w_out, _picked(w_out, (e,), (SSM_WIDTH, d), (1, 0))))
    return _half_ffn(x, mod, 6, gains, 2, ffn_w, (layer, 1), mix=mix)


def _odd_layer(x, mod, gains, ffn_w, layer, w_f, o):
    x, rstd = _half_ffn(x, mod, 0, gains, 0, ffn_w, (layer, 0), emit_rstd=True)
    z = _fourier(x, rstd, mod, gains)
    mix = ((z, w_f, _picked(w_f, (o,), w_f.shape[1:])),)
    return _half_ffn(x, mod, 6, gains, 2, ffn_w, (layer, 1), mix=mix)


def kernel(x, c, ctx, c_ctx, w_ada, b_ada, norm_gain, ffn_w1, ffn_w3, ffn_w2, w_in, q_gain, k_gain, sink_logit, ssm_lam_re, ssm_lam_im, ssm_log_dt, ssm_b_re, ssm_b_im, ssm_c_re, ssm_c_im, ssm_d, ssm_w_glu, w_out, fourier_w_out):
    bs, ls, d = x.shape
    depth = w_ada.shape[0]
    assert depth == 2 and d == D_MODEL
    rows = 16
    cond = jnp.concatenate([c, c_ctx[None, :], jnp.zeros((rows - bs - 1, d), F32)], axis=0)
    mods = _ada_mod(cond, w_ada, b_ada)[:, :bs + 1].reshape(depth, bs + 1, N_MOD, d)

    ffn_w = (ffn_w1, ffn_w3, ffn_w2)
    ssm = (ssm_lam_re[0], ssm_lam_im[0], ssm_log_dt[0], ssm_b_re[0], ssm_b_im[0], ssm_c_re[0], ssm_c_im[0])
    x = _even_layer(x, ctx, mods[0], norm_gain[0], ffn_w, 0, w_in[0], q_gain[0], k_gain[0],
                    sink_logit[0], ssm, ssm_d[0], ssm_w_glu[0], w_out, 0)
    return _odd_layer(x, mods[1], norm_gain[1], ffn_w, 1, fourier_w_out, 0)
```
